```python
import math
import jax, jax.numpy as jnp
from jax import lax
import numpy as np

D_MODEL = 2048
BATCH = 16
SEQ = 256
DEPTH = 2
DEC_BATCH = 2
DEC_SEQ = 2048
PAST_LEN = 256

GRID_W = 64
N_AF = (DEPTH + 1) // 2
N_DN = DEPTH // 2
EPS = 1e-6

HEAD_DIM = 128
N_HEADS = 8
KV_HEADS = 2
FOURIER_GROUPS = 4
FOURIER_GROUP_DIM = 256
FOURIER_WIDTH = FOURIER_GROUPS * FOURIER_GROUP_DIM
ATTN_WIDTH = N_HEADS * HEAD_DIM
KV_WIDTH = KV_HEADS * HEAD_DIM
AF_IN = FOURIER_WIDTH + ATTN_WIDTH + 2 * KV_WIDTH
AF_OUT = FOURIER_WIDTH + ATTN_WIDTH
Q_BLOCK = 128
ROPE_THETA = 10000.0
ROPE_AXIS_DIM = HEAD_DIM // 2

DN_QK_HEADS = 16
DN_V_HEADS = 32
DN_K_DIM = 128
DN_V_DIM = 128
DN_QK_WIDTH = DN_QK_HEADS * DN_K_DIM
DN_V_WIDTH = DN_V_HEADS * DN_V_DIM
DN_CONV_CH = 2 * DN_QK_WIDTH + DN_V_WIDTH
DN_IN = DN_CONV_CH + DN_V_WIDTH + 4 * DN_V_HEADS
CONV_K = 4
CONV_PAD = (2, 1)
DN_CHUNK = 64

PEER_HEADS = 8
PEER_N_KEYS = 128
PEER_N_EXPERTS = PEER_N_KEYS * PEER_N_KEYS
PEER_QDIM = 256
PEER_HALF = PEER_QDIM // 2
PEER_TOPK = 16
PEER_TOKEN_BLOCK = 128

kernel_name = "hybrid_fourier_gqa_gdn_peer_diffusion_step"


def rms_norm(x, gain):
    xf = x.astype(jnp.float32)
    y = xf * lax.rsqrt(jnp.mean(xf * xf, axis=-1, keepdims=True) + EPS)
    return (y * gain.astype(jnp.float32)).astype(x.dtype)


def l2_normalize(x):
    xf = x.astype(jnp.float32)
    return (xf * lax.rsqrt(jnp.sum(xf * xf, axis=-1, keepdims=True) + EPS)).astype(x.dtype)


def ada_params(cond, w, b):
    m = jax.nn.silu(cond) @ w + b
    return jnp.split(m[..., None, :], 6, axis=-1)


def modulate(x, gain, shift, scale):
    return rms_norm(x, gain) * (1 + scale) + shift


def grid_angles(n_tokens):
    n_rows = n_tokens // GRID_W
    rows = jnp.repeat(jnp.arange(n_rows), GRID_W).astype(jnp.float32)
    cols = jnp.tile(jnp.arange(GRID_W), n_rows).astype(jnp.float32)
    n_freq = ROPE_AXIS_DIM // 2
    inv = ROPE_THETA ** (-jnp.arange(n_freq, dtype=jnp.float32) / n_freq)
    return rows[:, None] * inv, cols[:, None] * inv


def rotate(x, ang):
    half = x.shape[-1] // 2
    x1, x2 = x[..., :half], x[..., half:]
    cos = jnp.cos(ang)[None, :, None, :].astype(x.dtype)
    sin = jnp.sin(ang)[None, :, None, :].astype(x.dtype)
    return jnp.concatenate([x1 * cos - x2 * sin, x1 * sin + x2 * cos], axis=-1)


def axial_rope(x, ang_row, ang_col):
    return jnp.concatenate([rotate(x[..., :ROPE_AXIS_DIM], ang_row),
                            rotate(x[..., ROPE_AXIS_DIM:], ang_col)], axis=-1)


def blocked_attention(q, k, v):
    b, sq, h, dh = q.shape
    kvh = k.shape[2]
    grp = h // kvh
    nb = sq // Q_BLOCK
    qb = q.reshape(b, nb, Q_BLOCK, kvh, grp, dh).transpose(1, 0, 2, 3, 4, 5)
    scale = dh ** -0.5

    def one_block(qblk):
        s = jnp.einsum('bqhgd,bkhd->bhgqk', qblk, k).astype(jnp.float32) * scale
        p = jax.nn.softmax(s, axis=-1).astype(v.dtype)
        return jnp.einsum('bhgqk,bkhd->bqhgd', p, v)

    o = lax.map(one_block, qb)
    return o.transpose(1, 0, 2, 3, 4, 5).reshape(b, sq, h * dh)


def fourier_mix(xa):
    b, s, _ = xa.shape
    xg = xa.reshape(b, s, FOURIER_GROUPS, FOURIER_GROUP_DIM).astype(jnp.float32)
    y = jnp.fft.fft2(xg, axes=(1, 3), norm='ortho').real
    return y.reshape(b, s, FOURIER_WIDTH).astype(xa.dtype)


def af_project(u, w_in, q_gain, k_gain):
    b, s, _ = u.shape
    p = u @ w_in
    xa, q, k, v = jnp.split(p, [FOURIER_WIDTH, FOURIER_WIDTH + ATTN_WIDTH,
                                FOURIER_WIDTH + ATTN_WIDTH + KV_WIDTH], axis=-1)
    q = rms_norm(q.reshape(b, s, N_HEADS, HEAD_DIM), q_gain)
    k = rms_norm(k.reshape(b, s, KV_HEADS, HEAD_DIM), k_gain)
    v = v.reshape(b, s, KV_HEADS, HEAD_DIM)
    return xa, q, k, v


def af_mixer_context(u, w_in, q_gain, k_gain, w_out):
    xa, q, k, v = af_project(u, w_in, q_gain, k_gain)
    o = jnp.concatenate([fourier_mix(xa), blocked_attention(q, k, v)], axis=-1) @ w_out
    return o, k, v


def af_mixer_latent(u, ctx_k, ctx_v, w_in, q_gain, k_gain, w_out):
    xa, q, k, v = af_project(u, w_in, q_gain, k_gain)
    ang_row, ang_col = grid_angles(u.shape[1])
    q = axial_rope(q, ang_row, ang_col)
    k = axial_rope(k, ang_row, ang_col)
    keys = jnp.concatenate([k, ctx_k.astype(k.dtype)], axis=1)
    vals = jnp.concatenate([v, ctx_v.astype(v.dtype)], axis=1)
    o = jnp.concatenate([fourier_mix(xa), blocked_attention(q, keys, vals)], axis=-1) @ w_out
    return o


def short_conv(x, w):
    y = lax.conv_general_dilated(x, w[:, None, :].astype(x.dtype), window_strides=(1,),
                                 padding=[CONV_PAD], dimension_numbers=('NWC', 'WIO', 'NWC'),
                                 feature_group_count=x.shape[-1])
    return jax.nn.silu(y)


def dn_project(u, w_in, conv_w, a_log, dt_bias):
    b, t, _ = u.shape
    p = u @ w_in
    qkv, z, ba = jnp.split(p, [DN_CONV_CH, DN_CONV_CH + DN_V_WIDTH], axis=-1)
    qkv = short_conv(qkv, conv_w)
    q, k, v = jnp.split(qkv, [DN_QK_WIDTH, 2 * DN_QK_WIDTH], axis=-1)
    rep = DN_V_HEADS // DN_QK_HEADS
    q = jnp.repeat(l2_normalize(q.reshape(b, t, DN_QK_HEADS, DN_K_DIM)), rep, axis=2) * (DN_K_DIM ** -0.5)
    k = jnp.repeat(l2_normalize(k.reshape(b, t, DN_QK_HEADS, DN_K_DIM)), rep, axis=2)
    v = v.reshape(b, t, DN_V_HEADS, DN_V_DIM)
    z = z.reshape(b, t, DN_V_HEADS, DN_V_DIM)
    ba = ba.reshape(b, t, 2, 2, DN_V_HEADS)
    beta = jax.nn.sigmoid(ba[:, :, 0].astype(jnp.float32))
    g = -jnp.exp(a_log.astype(jnp.float32)) * jax.nn.softplus(
        ba[:, :, 1].astype(jnp.float32) + dt_bias.astype(jnp.float32))
    return q, k, v, z, g, beta


def gated_delta_chunked(q, k, v, g, beta, s0):
    b, t, h, dk = q.shape
    dv = v.shape[-1]
    c = DN_CHUNK
    n = t // c

    def chunks(x):
        x = x.astype(jnp.float32).reshape((b, n, c, h) + x.shape[3:])
        return jnp.moveaxis(x, (1, 3), (0, 2))

    qc, kc, vc, bc = chunks(q), chunks(k), chunks(v), chunks(beta)
    gc = jnp.cumsum(chunks(g), axis=-1)
    tril = jnp.tril(jnp.ones((c, c), bool))
    strict = jnp.tril(jnp.ones((c, c), bool), -1)
    diff = gc[..., :, None] - gc[..., None, :]
    decay = jnp.where(tril, jnp.exp(jnp.where(tril, diff, 0.0)), 0.0)
    kb = kc * bc[..., None]
    lower = jnp.where(strict, jnp.einsum('nbhcd,nbhsd->nbhcs', kb, kc) * decay, 0.0)
    rhs = jnp.concatenate([vc * bc[..., None], kb * jnp.exp(gc)[..., None]], axis=-1)
    sol = lax.linalg.triangular_solve(jnp.eye(c, dtype=jnp.float32) + lower, rhs,
                                      left_side=True, lower=True, unit_diagonal=True)
    u_c, w_c = sol[..., :dv], sol[..., dv:]
    a_qk = jnp.einsum('nbhcd,nbhsd->nbhcs', qc, kc) * decay
    q_dec = qc * jnp.exp(gc)[..., None]
    k_tail = kc * jnp.exp(gc[..., -1:] - gc)[..., None]
    g_last = jnp.exp(gc[..., -1])

    def step(s, inp):
        a, qd, kt, uu, ww, gl = inp
        v_new = uu - jnp.einsum('bhck,bhkv->bhcv', ww, s)
        o = jnp.einsum('bhck,bhkv->bhcv', qd, s) + jnp.einsum('bhcs,bhsv->bhcv', a, v_new)
        s = s * gl[..., None, None] + jnp.einsum('bhck,bhcv->bhkv', kt, v_new)
        return s, o

    s_final, o = lax.scan(step, s0.astype(jnp.float32), (a_qk, q_dec, k_tail, u_c, w_c, g_last))
    o = jnp.moveaxis(o, (0, 2), (1, 3)).reshape(b, t, h, dv)
    return o.astype(v.dtype), s_final.astype(s0.dtype)


def bidirectional_delta(q, k, v, g, beta, s_fwd0, s_bwd0):
    flip = lambda x: jnp.flip(x, axis=1)
    o_f, s_f = gated_delta_chunked(q, k, v, g[:, :, 0], beta[:, :, 0], s_fwd0)
    o_b, s_b = gated_delta_chunked(flip(q), flip(k), flip(v), flip(g[:, :, 1]), flip(beta[:, :, 1]), s_bwd0)
    return o_f + flip(o_b), s_f, s_b


def dn_output(o, z, o_gain, w_out):
    b, t = o.shape[:2]
    on = rms_norm(o, o_gain) * jax.nn.silu(z)
    return on.reshape(b, t, DN_V_WIDTH) @ w_out


def dn_mixer_context(u, w_in, conv_w, a_log, dt_bias, o_gain, w_out):
    q, k, v, z, g, beta = dn_project(u, w_in, conv_w, a_log, dt_bias)
    zero = jnp.zeros((u.shape[0], DN_V_HEADS, DN_K_DIM, DN_V_DIM), u.dtype)
    o, s_f, s_b = bidirectional_delta(q, k, v, g, beta, zero, zero)
    return dn_output(o, z, o_gain, w_out), s_f, s_b


def dn_mixer_latent(u, s_f0, s_b0, w_in, conv_w, a_log, dt_bias, o_gain, w_out):
    q, k, v, z, g, beta = dn_project(u, w_in, conv_w, a_log, dt_bias)
    o, _, _ = bidirectional_delta(q, k, v, g, beta, s_f0, s_b0)
    return dn_output(o, z, o_gain, w_out)


def peer(x, w_q, sub_keys, w_down, w_up):
    b, s, d = x.shape
    xt = x.reshape(b * s, d)
    q = (xt @ w_q).reshape(-1, PEER_HEADS, 2, PEER_HALF)
    sc = jnp.einsum('thpd,pnd->thpn', q, sub_keys).astype(jnp.float32)
    top_s, top_i = lax.top_k(sc, PEER_TOPK)
    cand_s = top_s[:, :, 0, :, None] + top_s[:, :, 1, None, :]
    cand_i = top_i[:, :, 0, :, None] * PEER_N_KEYS + top_i[:, :, 1, None, :]
    n_cand = PEER_TOPK * PEER_TOPK
    best_s, pos = lax.top_k(cand_s.reshape(-1, PEER_HEADS, n_cand), PEER_TOPK)
    idx = jnp.take_along_axis(cand_i.reshape(-1, PEER_HEADS, n_cand), pos, axis=-1)
    gate = jax.nn.softmax(best_s, axis=-1).astype(x.dtype)
    nblk = xt.shape[0] // PEER_TOKEN_BLOCK

    def block(args):
        xb, ib, gb = args
        hpre = jnp.einsum('td,thkd->thk', xb, w_down[ib])
        act = jax.nn.gelu(hpre, approximate=False) * gb
        return jnp.einsum('thk,thkd->td', act, w_up[ib])

    out = lax.map(block, (xt.reshape(nblk, PEER_TOKEN_BLOCK, d),
                          idx.reshape(nblk, PEER_TOKEN_BLOCK, PEER_HEADS, PEER_TOPK),
                          gate.reshape(nblk, PEER_TOKEN_BLOCK, PEER_HEADS, PEER_TOPK)))
    return out.reshape(b, s, d)


def setup_inputs(seed: int = 0) -> dict:
    key = jax.random.key(seed)
    ks = iter(jax.random.split(key, 40))
    f32 = jnp.float32

    def nrm(shape, scale):
        return jax.random.normal(next(ks), shape, f32) * scale

    def gain(shape):
        return 1.0 + 0.02 * jax.random.normal(next(ks), shape, f32)

    D = D_MODEL
    dt = jnp.exp(jax.random.uniform(next(ks), (N_DN, 2, DN_V_HEADS), f32,
                                    minval=math.log(1e-3), maxval=math.log(1e-1)))
    return {
        "x_prompt": nrm((BATCH, SEQ, D), 1.0),
        "x_sample": nrm((DEC_BATCH, DEC_SEQ, D), 1.0),
        "cache_k": nrm((DEC_BATCH, N_AF, PAST_LEN, KV_HEADS, HEAD_DIM), 1.0),
        "cache_v": nrm((DEC_BATCH, N_AF, PAST_LEN, KV_HEADS, HEAD_DIM), 1.0),
        "state_fwd": nrm((DEC_BATCH, N_DN, DN_V_HEADS, DN_K_DIM, DN_V_DIM), 0.1),
        "state_bwd": nrm((DEC_BATCH, N_DN, DN_V_HEADS, DN_K_DIM, DN_V_DIM), 0.1),
        "c": nrm((DEC_BATCH, D), 1.0),
        "c_ctx": nrm((D,), 1.0),
        "ada_w": nrm((DEPTH, D, 6 * D), D ** -0.5),
        "ada_b": nrm((DEPTH, 6 * D), 0.02),
        "norm1": gain((DEPTH, D)),
        "norm2": gain((DEPTH, D)),
        "af_w_in": nrm((N_AF, D, AF_IN), D ** -0.5),
        "af_q_norm": gain((N_AF, HEAD_DIM)),
        "af_k_norm": gain((N_AF, HEAD_DIM)),
        "af_w_out": nrm((N_AF, AF_OUT, D), AF_OUT ** -0.5),
        "dn_w_in": nrm((N_DN, D, DN_IN), D ** -0.5),
        "dn_conv_w": nrm((N_DN, CONV_K, DN_CONV_CH), 0.5),
        "dn_a_log": jnp.log(jax.random.uniform(next(ks), (N_DN, 2, DN_V_HEADS), f32, minval=1.0, maxval=16.0)),
        "dn_dt_bias": jnp.log(jnp.expm1(dt)),
        "dn_o_norm": gain((N_DN, DN_V_DIM)),
        "dn_w_out": nrm((N_DN, DN_V_WIDTH, D), DN_V_WIDTH ** -0.5),
        "peer_w_q": nrm((DEPTH, D, PEER_HEADS * PEER_QDIM), D ** -0.5),
        "peer_sub_keys": nrm((DEPTH, 2, PEER_N_KEYS, PEER_HALF), PEER_HALF ** -0.5),
        "peer_w_down": nrm((DEPTH, PEER_N_EXPERTS, D), D ** -0.5),
        "peer_w_up": nrm((DEPTH, PEER_N_EXPERTS, D), 0.5),
        "final_norm": gain((D,)),
    }


def reference(x_prompt, x_sample, cache_k, cache_v, state_fwd, state_bwd, c, c_ctx,
              ada_w, ada_b, norm1, norm2, af_w_in, af_q_norm, af_k_norm, af_w_out,
              dn_w_in, dn_conv_w, dn_a_log, dn_dt_bias, dn_o_norm, dn_w_out,
              peer_w_q, peer_sub_keys, peer_w_down, peer_w_up, final_norm):
    xp, xs = x_prompt, x_sample
    new_k, new_v, new_sf, new_sb = [], [], [], []
    for i in range(DEPTH):
        j = i // 2
        sh1c, sc1c, g1c, sh2c, sc2c, g2c = ada_params(c_ctx, ada_w[i], ada_b[i])
        sh1, sc1, g1, sh2, sc2, g2 = ada_params(c, ada_w[i], ada_b[i])
        up = modulate(xp, norm1[i], sh1c, sc1c)
        us = modulate(xs, norm1[i], sh1, sc1)
        if i % 2 == 0:
            mp, k_ctx, v_ctx = af_mixer_context(up, af_w_in[j], af_q_norm[j], af_k_norm[j], af_w_out[j])
            ms = af_mixer_latent(us, cache_k[:, j], cache_v[:, j], af_w_in[j], af_q_norm[j],
                                 af_k_norm[j], af_w_out[j])
            new_k.append(k_ctx)
            new_v.append(v_ctx)
        else:
            mp, s_f, s_b = dn_mixer_context(up, dn_w_in[j], dn_conv_w[j], dn_a_log[j], dn_dt_bias[j],
                                            dn_o_norm[j], dn_w_out[j])
            ms = dn_mixer_latent(us, state_fwd[:, j], state_bwd[:, j], dn_w_in[j], dn_conv_w[j],
                                 dn_a_log[j], dn_dt_bias[j], dn_o_norm[j], dn_w_out[j])
            new_sf.append(s_f)
            new_sb.append(s_b)
        xp = xp + g1c * mp
        xs = xs + g1 * ms
        xp = xp + g2c * peer(modulate(xp, norm2[i], sh2c, sc2c), peer_w_q[i], peer_sub_keys[i],
                             peer_w_down[i], peer_w_up[i])
        xs = xs + g2 * peer(modulate(xs, norm2[i], sh2, sc2), peer_w_q[i], peer_sub_keys[i],
                            peer_w_down[i], peer_w_up[i])
    y_prompt = rms_norm(xp, final_norm)
    y_sample = rms_norm(xs, final_norm)
    return (y_prompt, y_sample, jnp.stack(new_k, axis=1), jnp.stack(new_v, axis=1),
            jnp.stack(new_sf, axis=1), jnp.stack(new_sb, axis=1))
```

```python
import functools
import math

import jax
import jax.numpy as jnp
import numpy as np
from jax import lax
from jax.experimental import pallas as pl
from jax.experimental.pallas import tpu as pltpu

F32 = jnp.float32
BF16 = jnp.bfloat16

EPS = 1e-6
HEAD_DIM = 128
N_HEADS = 8
KV_HEADS = 2
GQA_GROUP = N_HEADS // KV_HEADS
FOURIER_GROUPS = 4
FOURIER_GROUP_DIM = 256
FOURIER_WIDTH = FOURIER_GROUPS * FOURIER_GROUP_DIM
ATTN_WIDTH = N_HEADS * HEAD_DIM
KV_WIDTH = KV_HEADS * HEAD_DIM
GRID_W = 64
ROPE_THETA = 10000.0
ROPE_AXIS_DIM = HEAD_DIM // 2

DN_QK_HEADS = 16
DN_V_HEADS = 32
DN_K_DIM = 128
DN_V_DIM = 128
DN_QK_WIDTH = DN_QK_HEADS * DN_K_DIM
DN_V_WIDTH = DN_V_HEADS * DN_V_DIM
DN_CONV_CH = 2 * DN_QK_WIDTH + DN_V_WIDTH
DN_CHUNK = 64
CONV_PAD = (2, 1)

PEER_HEADS = 8
PEER_N_KEYS = 128
PEER_HALF = 128
PEER_TOPK = 16

VMEM_LIMIT = 48 * 1024 * 1024


def _cparams(sem):
    return pltpu.CompilerParams(dimension_semantics=sem, vmem_limit_bytes=VMEM_LIMIT)


def _bdot(a, b):
    return jnp.dot(a.astype(BF16), b.astype(BF16), preferred_element_type=F32)


def _bdot_nt(a, b):
    return lax.dot_general(a.astype(BF16), b.astype(BF16), (((1,), (1,)), ((), ())),
                           preferred_element_type=F32)


def _bdot_tn(a, b):
    return lax.dot_general(a.astype(BF16), b.astype(BF16), (((0,), (0,)), ((), ())),
                           preferred_element_type=F32)


def _proj_kernel(x_ref, w_ref, o_ref):
    o_ref[...] = _bdot(x_ref[...], w_ref[...]).astype(o_ref.dtype)


def proj(x, w, *, tm, tn, col0=0, ncols=None, out_dtype=F32, name="proj"):
    m, k = x.shape
    ncols = w.shape[1] - col0 if ncols is None else ncols
    tm = min(tm, m)
    tn = min(tn, ncols)
    assert m % tm == 0 and ncols % tn == 0 and col0 % tn == 0
    cb = col0 // tn
    return pl.pallas_call(
        _proj_kernel,
        grid=(m // tm, ncols // tn),
        in_specs=[pl.BlockSpec((tm, k), lambda i, j: (i, 0)),
                  pl.BlockSpec((k, tn), lambda i, j: (0, j + cb))],
        out_specs=pl.BlockSpec((tm, tn), lambda i, j: (i, j)),
        out_shape=jax.ShapeDtypeStruct((m, ncols), out_dtype),
        compiler_params=_cparams(("parallel", "arbitrary")),
        name=name,
    )(x, w)


def _proj_res_kernel(x_ref, w_ref, res_ref, gate_ref, o_ref):
    o_ref[...] = res_ref[...] + gate_ref[0] * _bdot(x_ref[...], w_ref[...])


def proj_residual(x, w, res, gates, group_of_tile, *, tm, tn, name="proj_res"):
    m, k = x.shape
    n = w.shape[1]
    tm = min(tm, m)
    tn = min(tn, n)
    assert m % tm == 0 and n % tn == 0
    return pl.pallas_call(
        _proj_res_kernel,
        grid=(m // tm, n // tn),
        in_specs=[pl.BlockSpec((tm, k), lambda i, j: (i, 0)),
                  pl.BlockSpec((k, tn), lambda i, j: (0, j)),
                  pl.BlockSpec((tm, tn), lambda i, j: (i, j)),
                  pl.BlockSpec((1, 1, tn), lambda i, j: (group_of_tile(i, tm), 0, j))],
        out_specs=pl.BlockSpec((tm, tn), lambda i, j: (i, j)),
        out_shape=jax.ShapeDtypeStruct((m, n), F32),
        compiler_params=_cparams(("parallel", "arbitrary")),
        name=name,
    )(x, w, res, gates)


def _modulate_kernel(x_ref, gain_ref, shift_ref, scale_ref, o_ref):
    x = x_ref[...]
    y = x * lax.rsqrt(jnp.mean(x * x, axis=-1, keepdims=True) + EPS)
    o_ref[...] = (y * gain_ref[...] * (1.0 + scale_ref[0]) + shift_ref[0]).astype(o_ref.dtype)


def modulate(x, gain, shift, scale, group_of_tile, *, tm, out_dtype=BF16):
    m, d = x.shape
    tm = min(tm, m)
    gspec = pl.BlockSpec((1, 1, d), lambda i: (group_of_tile(i, tm), 0, 0))
    return pl.pallas_call(
        _modulate_kernel,
        grid=(m // tm,),
        in_specs=[pl.BlockSpec((tm, d), lambda i: (i, 0)),
                  pl.BlockSpec((1, d), lambda i: (0, 0)), gspec, gspec],
        out_specs=pl.BlockSpec((tm, d), lambda i: (i, 0)),
        out_shape=jax.ShapeDtypeStruct((m, d), out_dtype),
        compiler_params=_cparams(("parallel",)),
        name="modulate",
    )(x, gain.reshape(1, d), shift, scale)


def _rmsnorm_kernel(x_ref, gain_ref, o_ref):
    x = x_ref[...]
    o_ref[...] = x * lax.rsqrt(jnp.mean(x * x, axis=-1, keepdims=True) + EPS) * gain_ref[...]


def rmsnorm_rows(x, gain, *, tm):
    m, d = x.shape
    tm = min(tm, m)
    return pl.pallas_call(
        _rmsnorm_kernel,
        grid=(m // tm,),
        in_specs=[pl.BlockSpec((tm, d), lambda i: (i, 0)), pl.BlockSpec((1, d), lambda i: (0, 0))],
        out_specs=pl.BlockSpec((tm, d), lambda i: (i, 0)),
        out_shape=jax.ShapeDtypeStruct((m, d), F32),
        compiler_params=_cparams(("parallel",)),
        name="final_norm",
    )(x, gain.reshape(1, d))


def _ada_kernel(c_ref, w_ref, b_ref, o_ref):
    c = c_ref[...]
    o_ref[0] = _bdot(c * jax.nn.sigmoid(c), w_ref[0]) + b_ref[0]


def ada_all(conds, ada_w, ada_b, *, tn=1024):
    depth, d, n = ada_w.shape
    r = conds.shape[0]
    return pl.pallas_call(
        _ada_kernel,
        grid=(depth, n // tn),
        in_specs=[pl.BlockSpec((r, d), lambda l, j: (0, 0)),
                  pl.BlockSpec((1, d, tn), lambda l, j: (l, 0, j)),
                  pl.BlockSpec((1, 1, tn), lambda l, j: (l, 0, j))],
        out_specs=pl.BlockSpec((1, r, tn), lambda l, j: (l, 0, j)),
        out_shape=jax.ShapeDtypeStruct((depth, r, n), F32),
        compiler_params=_cparams(("parallel", "arbitrary")),
        name="ada",
    )(conds, ada_w, ada_b.reshape(depth, 1, n))


def _dft_tables(n, scale):
    idx = np.arange(n, dtype=np.int64)
    ang = 2.0 * np.pi * ((idx[:, None] * idx[None, :]) % n).astype(np.float64) / n
    return np.stack([np.cos(ang) * scale, np.sin(ang) * scale]).astype(np.float32)


def _chan_dft_kernel(x_ref, t_ref, o_ref):
    o_ref[0] = _bdot(x_ref[...], t_ref[0]).astype(o_ref.dtype)


def _seq_dft_kernel(t_ref, r_ref, o_ref):
    o_ref[...] = (_bdot(t_ref[0], r_ref[0]) - _bdot(t_ref[1], r_ref[1])).astype(o_ref.dtype)


def fourier_mix(p, n_seq, seq_len, *, tm, out_dtype=BF16):
    t = n_seq * seq_len
    gd = FOURIER_GROUP_DIM
    tm = min(tm, t)
    ctab = jnp.asarray(_dft_tables(gd, gd ** -0.5), BF16)
    r = pl.pallas_call(
        _chan_dft_kernel,
        grid=(t // tm, FOURIER_GROUPS, 2),
        in_specs=[pl.BlockSpec((tm, gd), lambda i, g, s: (i, g)),
                  pl.BlockSpec((1, gd, gd), lambda i, g, s: (s, 0, 0))],
        out_specs=pl.BlockSpec((1, tm, gd), lambda i, g, s: (s, i, g)),
        out_shape=jax.ShapeDtypeStruct((2, t, FOURIER_WIDTH), BF16),
        compiler_params=_cparams(("parallel", "arbitrary", "arbitrary")),
        name="chan_dft",
    )(p, ctab)
    stab = jnp.asarray(_dft_tables(seq_len, seq_len ** -0.5), BF16)
    ts = min(512, seq_len)
    tn = 512
    nrow = seq_len // ts
    return pl.pallas_call(
        _seq_dft_kernel,
        grid=(n_seq, nrow, FOURIER_WIDTH // tn),
        in_specs=[pl.BlockSpec((2, ts, seq_len), lambda b, i, j: (0, i, 0)),
                  pl.BlockSpec((2, seq_len, tn), lambda b, i, j: (0, b, j))],
        out_specs=pl.BlockSpec((ts, tn), lambda b, i, j: (b * nrow + i, j)),
        out_shape=jax.ShapeDtypeStruct((t, FOURIER_WIDTH), out_dtype),
        compiler_params=_cparams(("parallel", "arbitrary", "arbitrary")),
        name="seq_dft",
    )(stab, r)


def _attn_kernel(q_ref, k_ref, v_ref, o_ref):
    k = k_ref[0].astype(BF16)
    v = v_ref[0].astype(BF16)
    scale = HEAD_DIM ** -0.5
    for g in range(GQA_GROUP):
        q = q_ref[0, :, g * HEAD_DIM:(g + 1) * HEAD_DIM]
        s = _bdot_nt(q, k) * scale
        s = s - jnp.max(s, axis=-1, keepdims=True)
        e = jnp.exp(s)
        p = e / jnp.sum(e, axis=-1, keepdims=True)
        o_ref[0, :, g * HEAD_DIM:(g + 1) * HEAD_DIM] = _bdot(p, v).astype(o_ref.dtype)


def attention(q, k, v, *, tq=256, out_dtype=BF16):
    b, sq, _ = q.shape
    sk = k.shape[1]
    tq = min(tq, sq)
    gw = GQA_GROUP * HEAD_DIM
    return pl.pallas_call(
        _attn_kernel,
        grid=(b, KV_HEADS, sq // tq),
        in_specs=[pl.BlockSpec((1, tq, gw), lambda bb, h, i: (bb, i, h)),
                  pl.BlockSpec((1, sk, HEAD_DIM), lambda bb, h, i: (bb, 0, h)),
                  pl.BlockSpec((1, sk, HEAD_DIM), lambda bb, h, i: (bb, 0, h))],
        out_specs=pl.BlockSpec((1, tq, gw), lambda bb, h, i: (bb, i, h)),
        out_shape=jax.ShapeDtypeStruct((b, sq, ATTN_WIDTH), out_dtype),
        compiler_params=_cparams(("parallel", "parallel", "arbitrary")),
        name="attention",
    )(q, k, v)


def _delta_chunk(q, k, v, beta, gcol, grow, glast, s, reverse):
    c = q.shape[0]
    ri = lax.broadcasted_iota(jnp.int32, (c, c), 0)
    ci = lax.broadcasted_iota(jnp.int32, (c, c), 1)
    incl = (ri <= ci) if reverse else (ri >= ci)
    strict = (ri < ci) if reverse else (ri > ci)
    decay = jnp.where(incl, jnp.exp(jnp.where(incl, gcol - grow, 0.0)), 0.0)
    kb = k * beta
    low = jnp.where(strict, _bdot_nt(kb, k) * decay, 0.0)
    def same_block(size):
        sh = int(math.log2(size))
        return (ri >> sh) == (ci >> sh)

    leaf = 4
    dg = jnp.where(same_block(leaf), low, 0.0)
    imd = jnp.where(ri == ci, 1.0, 0.0) - dg
    inv = imd + _bdot(imd, _bdot(dg, dg))
    size = leaf
    while size < c:
        off = jnp.where(same_block(2 * size) & jnp.logical_not(same_block(size)), low, 0.0)
        inv = inv - _bdot(inv, _bdot(off, inv))
        size *= 2
    eg = jnp.exp(gcol)
    rhs = jnp.concatenate([v * beta, kb * eg], axis=-1)
    sol = _bdot(inv, rhs)
    dv = v.shape[-1]
    u, w = sol[:, :dv], sol[:, dv:]
    a = _bdot_nt(q, k) * decay
    v_new = u - _bdot(w, s)
    o = _bdot(q * eg, s) + _bdot(a, v_new)
    k_tail = k * jnp.exp(glast - gcol)
    s_new = s * jnp.exp(glast) + _bdot_tn(k_tail, v_new)
    return o, s_new


def _delta_kernel(q_ref, k_ref, v_ref, bf_ref, bb_ref, gfc_ref, gbc_ref, gfr_ref, gbr_ref,
                  sf0_ref, sb0_ref, o_ref, sf_ref, sb_ref, *, n_chunks):
    c = DN_CHUNK
    o_ref[...] = jnp.zeros_like(o_ref)

    def body(i, carry):
        sf, sb = carry
        for reverse in (False, True):
            ch = (n_chunks - 1 - i) if reverse else i
            rows = pl.ds(pl.multiple_of(ch * c, c), c)
            bcol = (bb_ref if reverse else bf_ref)[0, 0, rows, :]
            gcol = (gbc_ref if reverse else gfc_ref)[0, 0, rows, :]
            grow = (gbr_ref if reverse else gfr_ref)[0, 0, pl.ds(ch, 1), :]
            glast = grow[:, 0:1] if reverse else grow[:, c - 1:c]
            o, s_new = _delta_chunk(q_ref[0, rows, :], k_ref[0, rows, :], v_ref[0, rows, :],
                                    bcol, gcol, grow, glast, sb if reverse else sf, reverse)
            o_ref[0, rows, :] += o
            if reverse:
                sb = s_new
            else:
                sf = s_new
        return sf, sb

    sf, sb = lax.fori_loop(0, n_chunks, body, (sf0_ref[0, 0], sb0_ref[0, 0]))
    sf_ref[0, 0] = sf
    sb_ref[0, 0] = sb


def delta_rule(q, k, v, beta, g, s_f0, s_b0):
    b, t, _ = v.shape
    h = DN_V_HEADS
    c = DN_CHUNK
    nc = t // c
    rep = DN_V_HEADS // DN_QK_HEADS
    gch = g.reshape(b, nc, c, 2, h)
    gcf = jnp.cumsum(gch[:, :, :, 0], axis=2)
    gcb = jnp.flip(jnp.cumsum(jnp.flip(gch[:, :, :, 1], axis=2), axis=2), axis=2)

    def col(x):
        return jnp.transpose(x.reshape(b, t, h), (0, 2, 1))[..., None]

    def row(x):
        return jnp.transpose(x, (0, 3, 1, 2))

    bt = jnp.transpose(beta, (0, 2, 3, 1))[..., None]
    colspec = pl.BlockSpec((1, 1, t, 1), lambda bb, hh: (bb, hh, 0, 0))
    rowspec = pl.BlockSpec((1, 1, nc, c), lambda bb, hh: (bb, hh, 0, 0))
    stspec = pl.BlockSpec((1, 1, DN_K_DIM, DN_V_DIM), lambda bb, hh: (bb, hh, 0, 0))
    o, sf, sb = pl.pallas_call(
        functools.partial(_delta_kernel, n_chunks=nc),
        grid=(b, h),
        in_specs=[pl.BlockSpec((1, t, DN_K_DIM), lambda bb, hh: (bb, 0, hh // rep)),
                  pl.BlockSpec((1, t, DN_K_DIM), lambda bb, hh: (bb, 0, hh // rep)),
                  pl.BlockSpec((1, t, DN_V_DIM), lambda bb, hh: (bb, 0, hh)),
                  colspec, colspec, colspec, colspec, rowspec, rowspec, stspec, stspec],
        out_specs=[pl.BlockSpec((1, t, DN_V_DIM), lambda bb, hh: (bb, 0, hh)), stspec, stspec],
        out_shape=[jax.ShapeDtypeStruct((b, t, DN_V_WIDTH), F32),
                   jax.ShapeDtypeStruct((b, h, DN_K_DIM, DN_V_DIM), F32),
                   jax.ShapeDtypeStruct((b, h, DN_K_DIM, DN_V_DIM), F32)],
        compiler_params=_cparams(("parallel", "parallel")),
        name="delta_rule",
    )(q, k, v, bt[:, 0], bt[:, 1], col(gcf), col(gcb), row(gcf), row(gcb), s_f0, s_b0)
    return o, sf, sb


def _peer_route_kernel(qt_ref, keys_ref, s_ref, tl_ref, v1_ref, v2_ref):
    kk = PEER_TOPK
    neg = -jnp.inf
    for h in range(PEER_HEADS):
        for p, vref in ((0, v1_ref), (1, v2_ref)):
            r0 = (h * 2 + p) * PEER_HALF
            sc = _bdot(keys_ref[p], qt_ref[r0:r0 + PEER_HALF, :])
            s_ref[p, h] = sc
            for r in range(kk):
                m = jnp.max(sc, axis=0, keepdims=True)
                vref[r:r + 1, :] = m
                sc = jnp.where(sc == m, neg, sc)
        v2 = v2_ref[...]
        cand = [v1_ref[0:1, :] + v2] + [v1_ref[a:a + 1, :] + v2[0:8] for a in range(1, kk)]
        top = cand[0][0:1, :]
        work = list(cand)
        tau = top
        for r in range(kk):
            m = work[0]
            for wk in work[1:]:
                m = jnp.maximum(m, jnp.max(wk, axis=0, keepdims=True))
            tau = jnp.max(m, axis=0, keepdims=True)
            work = [jnp.where(wk == tau, neg, wk) for wk in work]
        z = None
        for cd in cand:
            zz = jnp.sum(jnp.where(cd >= tau, jnp.exp(cd - top), 0.0), axis=0, keepdims=True)
            z = zz if z is None else z + zz
        tl_ref[0, h:h + 1, :] = tau
        tl_ref[1, h:h + 1, :] = -(top + jnp.log(z))


def peer_route(qt, sub_keys, *, tb):
    d2, t = qt.shape
    tb = min(tb, t)
    return pl.pallas_call(
        _peer_route_kernel,
        grid=(t // tb,),
        in_specs=[pl.BlockSpec((d2, tb), lambda j: (0, j)),
                  pl.BlockSpec((2, PEER_N_KEYS, PEER_HALF), lambda j: (0, 0, 0))],
        out_specs=[pl.BlockSpec((2, PEER_HEADS, PEER_N_KEYS, tb), lambda j: (0, 0, 0, j)),
                   pl.BlockSpec((2, PEER_HEADS, tb), lambda j: (0, 0, j))],
        out_shape=[jax.ShapeDtypeStruct((2, PEER_HEADS, PEER_N_KEYS, t), F32),
                   jax.ShapeDtypeStruct((2, PEER_HEADS, t), F32)],
        scratch_shapes=[pltpu.VMEM((PEER_TOPK, tb), F32), pltpu.VMEM((PEER_TOPK, tb), F32)],
        compiler_params=_cparams(("parallel",)),
        name="peer_route",
    )(qt, sub_keys)


def _peer_dense_kernel(xt_ref, wd_ref, wut_ref, s_ref, tl_ref, o_ref, *, rows_per_block):
    i = pl.program_id(1)

    @pl.when(i == 0)
    def _():
        o_ref[...] = jnp.zeros_like(o_ref)

    hpre = jnp.dot(wd_ref[...], xt_ref[...], preferred_element_type=F32)
    gates = []
    for r in range(rows_per_block):
        e1 = i * rows_per_block + r
        acc = None
        for h in range(PEER_HEADS):
            tot = s_ref[0, h, pl.ds(e1, 1), :] + s_ref[1, h]
            wgt = jnp.where(tot >= tl_ref[0, h:h + 1, :], jnp.exp(tot + tl_ref[1, h:h + 1, :]), 0.0)
            acc = wgt if acc is None else acc + wgt
        gates.append(acc)
    gate = jnp.concatenate(gates, axis=0)
    act = 0.5 * hpre * (1.0 + lax.erf(hpre * (2.0 ** -0.5))) * gate
    o_ref[...] += jnp.dot(wut_ref[...], act.astype(BF16), preferred_element_type=F32)


def peer_dense(xt, w_down, w_up_t, scores, tl, *, tb, eb):
    d, t = xt.shape
    ne = w_down.shape[0]
    tb = min(tb, t)
    assert eb % PEER_N_KEYS == 0
    return pl.pallas_call(
        functools.partial(_peer_dense_kernel, rows_per_block=eb // PEER_N_KEYS),
        grid=(t // tb, ne // eb),
        in_specs=[pl.BlockSpec((d, tb), lambda j, i: (0, j)),
                  pl.BlockSpec((eb, d), lambda j, i: (i, 0)),
                  pl.BlockSpec((d, eb), lambda j, i: (0, i)),
                  pl.BlockSpec((2, PEER_HEADS, PEER_N_KEYS, tb), lambda j, i: (0, 0, 0, j)),
                  pl.BlockSpec((2, PEER_HEADS, tb), lambda j, i: (0, 0, j))],
        out_specs=pl.BlockSpec((d, tb), lambda j, i: (0, j)),
        out_shape=jax.ShapeDtypeStruct((d, t), F32),
        compiler_params=_cparams(("parallel", "arbitrary")),
        name="peer_dense",
    )(xt, w_down, w_up_t, scores, tl)


def peer(u, w_q, sub_keys, w_down, w_up, *, tb=512, eb=512):
    ut = u.T
    qt = proj(w_q.T, ut, tm=512, tn=1024, name="peer_q")
    scores, tl = peer_route(qt, sub_keys, tb=256)
    out_t = peer_dense(ut, w_down.astype(BF16), w_up.T.astype(BF16), scores, tl, tb=tb, eb=eb)
    return out_t.T


def _rms(x, gain):
    return x * lax.rsqrt(jnp.mean(x * x, axis=-1, keepdims=True) + EPS) * gain


def _grid_angles(n_tokens):
    n_rows = n_tokens // GRID_W
    rows = jnp.repeat(jnp.arange(n_rows), GRID_W).astype(F32)
    cols = jnp.tile(jnp.arange(GRID_W), n_rows).astype(F32)
    n_freq = ROPE_AXIS_DIM // 2
    inv = ROPE_THETA ** (-jnp.arange(n_freq, dtype=F32) / n_freq)
    return rows[:, None] * inv, cols[:, None] * inv


def _rotate(x, ang):
    half = x.shape[-1] // 2
    x1, x2 = x[..., :half], x[..., half:]
    cos = jnp.cos(ang)[None, :, None, :]
    sin = jnp.sin(ang)[None, :, None, :]
    return jnp.concatenate([x1 * cos - x2 * sin, x1 * sin + x2 * cos], axis=-1)


def _axial_rope(x, ang_row, ang_col):
    return jnp.concatenate([_rotate(x[..., :ROPE_AXIS_DIM], ang_row),
                            _rotate(x[..., ROPE_AXIS_DIM:], ang_col)], axis=-1)


def _short_conv(x, w):
    y = lax.conv_general_dilated(x, w[:, None, :], window_strides=(1,), padding=[CONV_PAD],
                                 dimension_numbers=('NWC', 'WIO', 'NWC'), feature_group_count=x.shape[-1])
    return jax.nn.silu(y)


def _l2n(x):
    return x * lax.rsqrt(jnp.sum(x * x, axis=-1, keepdims=True) + EPS)


def kernel(x_prompt, x_sample, cache_k, cache_v, state_fwd, state_bwd, c, c_ctx, ada_w, ada_b, norm1, norm2,
           af_w_in, af_q_norm, af_k_norm, af_w_out, dn_w_in, dn_conv_w, dn_a_log, dn_dt_bias, dn_o_norm,
           dn_w_out, peer_w_q, peer_sub_keys, peer_w_down, peer_w_up, final_norm):
    nb, seq, d = x_prompt.shape
    db, dseq, _ = x_sample.shape
    depth = ada_w.shape[0]
    tp = nb * seq
    ts = db * dseq
    tt = tp + ts
    tm = math.gcd(math.gcd(tp, dseq), 1024)

    def group_of_tile(i, tile):
        r = i * tile
        return jnp.where(r < tp, 0, 1 + (r - tp) // dseq)

    conds = jnp.concatenate([c_ctx[None], c, jnp.zeros((8 - 1 - db, d), F32)], axis=0)
    mods = ada_all(conds, ada_w, ada_b)
    mods = mods.reshape(depth, 8, 6, d)

    x = jnp.concatenate([x_prompt.reshape(tp, d), x_sample.reshape(ts, d)], axis=0)
    new_k, new_v, new_sf, new_sb = [], [], [], []
    for i in range(depth):
        j = i // 2
        md = [mods[i, :, n][:, None, :] for n in range(6)]
        sh1, sc1, g1, sh2, sc2, g2 = md
        u = modulate(x, norm1[i], sh1, sc1, group_of_tile, tm=tm)
        if i % 2 == 0:
            p = proj(u, af_w_in[j], tm=tm, tn=512, name="af_in")
            q = _rms(p[:, FOURIER_WIDTH:FOURIER_WIDTH + ATTN_WIDTH].reshape(tt, N_HEADS, HEAD_DIM), af_q_norm[j])
            kv0 = FOURIER_WIDTH + ATTN_WIDTH
            k = _rms(p[:, kv0:kv0 + KV_WIDTH].reshape(tt, KV_HEADS, HEAD_DIM), af_k_norm[j])
            v = p[:, kv0 + KV_WIDTH:].reshape(tt, KV_HEADS, HEAD_DIM)
            new_k.append(k[:tp].reshape(nb, seq, KV_HEADS, HEAD_DIM))
            new_v.append(v[:tp].reshape(nb, seq, KV_HEADS, HEAD_DIM))
            ang_row, ang_col = _grid_angles(dseq)
            qs = _axial_rope(q[tp:].reshape(db, dseq, N_HEADS, HEAD_DIM), ang_row, ang_col)
            ks = _axial_rope(k[tp:].reshape(db, dseq, KV_HEADS, HEAD_DIM), ang_row, ang_col)
            keys = jnp.concatenate([ks, cache_k[:, j]], axis=1).reshape(db, -1, KV_WIDTH)
            vals = jnp.concatenate([v[tp:].reshape(db, dseq, KV_HEADS, HEAD_DIM), cache_v[:, j]],
                                   axis=1).reshape(db, -1, KV_WIDTH)
            ap = attention(q[:tp].reshape(nb, seq, ATTN_WIDTH), k[:tp].reshape(nb, seq, KV_WIDTH),
                           v[:tp].reshape(nb, seq, KV_WIDTH))
            asmp = attention(qs.reshape(db, dseq, ATTN_WIDTH), keys, vals)
            fp = fourier_mix(p[:tp], nb, seq, tm=tm)
            fs = fourier_mix(p[tp:], db, dseq, tm=tm)
            mix = jnp.concatenate([jnp.concatenate([fp, fs], axis=0),
                                   jnp.concatenate([ap.reshape(tp, ATTN_WIDTH), asmp.reshape(ts, ATTN_WIDTH)], axis=0)],
                                  axis=1)
            x = proj_residual(mix, af_w_out[j], x, g1, group_of_tile, tm=tm, tn=512, name="af_out")
        else:
            w_in = dn_w_in[j]
            p = proj(u, w_in, tm=tm, tn=512, ncols=DN_CONV_CH + DN_V_WIDTH, name="dn_in")
            ba = proj(u, w_in, tm=tm, tn=128, col0=DN_CONV_CH + DN_V_WIDTH, name="dn_in_ba")
            z = p[:, DN_CONV_CH:]
            beta_all = jax.nn.sigmoid(ba.reshape(tt, 2, 2, DN_V_HEADS)[:, 0])
            g_all = -jnp.exp(dn_a_log[j]) * jax.nn.softplus(ba.reshape(tt, 2, 2, DN_V_HEADS)[:, 1] + dn_dt_bias[j])
            outs = []
            for (r0, r1, b_, t_, s_f0, s_b0) in (
                    (0, tp, nb, seq, jnp.zeros((nb, DN_V_HEADS, DN_K_DIM, DN_V_DIM), F32),
                     jnp.zeros((nb, DN_V_HEADS, DN_K_DIM, DN_V_DIM), F32)),
                    (tp, tt, db, dseq, state_fwd[:, j], state_bwd[:, j])):
                qkv = _short_conv(p[r0:r1, :DN_CONV_CH].reshape(b_, t_, DN_CONV_CH), dn_conv_w[j])
                qq = _l2n(qkv[..., :DN_QK_WIDTH].reshape(b_, t_, DN_QK_HEADS, DN_K_DIM)) * (DN_K_DIM ** -0.5)
                kk = _l2n(qkv[..., DN_QK_WIDTH:2 * DN_QK_WIDTH].reshape(b_, t_, DN_QK_HEADS, DN_K_DIM))
                vv = qkv[..., 2 * DN_QK_WIDTH:]
                o, s_f, s_b = delta_rule(qq.reshape(b_, t_, DN_QK_WIDTH), kk.reshape(b_, t_, DN_QK_WIDTH), vv,
                                         beta_all[r0:r1].reshape(b_, t_, 2, DN_V_HEADS),
                                         g_all[r0:r1].reshape(b_, t_, 2, DN_V_HEADS), s_f0, s_b0)
                outs.append(o.reshape(b_ * t_, DN_V_HEADS, DN_V_DIM))
                if r0 == 0:
                    new_sf.append(s_f)
                    new_sb.append(s_b)
            o = jnp.concatenate(outs, axis=0)
            on = (_rms(o, dn_o_norm[j]) * jax.nn.silu(z.reshape(tt, DN_V_HEADS, DN_V_DIM))).reshape(tt, DN_V_WIDTH)
            x = proj_residual(on.astype(BF16), dn_w_out[j], x, g1, group_of_tile, tm=tm, tn=256, name="dn_out")
        u2 = modulate(x, norm2[i], sh2, sc2, group_of_tile, tm=tm)
        pe = peer(u2, peer_w_q[i], peer_sub_keys[i], peer_w_down[i], peer_w_up[i])
        gate2 = jnp.concatenate([jnp.broadcast_to(g2[0], (tp, d))] +
                                [jnp.broadcast_to(g2[1 + b_], (dseq, d)) for b_ in range(db)], axis=0)
        x = x + gate2 * pe
    y = rmsnorm_rows(x, final_norm, tm=tm)
    return (y[:tp].reshape(nb, seq, d), y[tp:].reshape(db, dseq, d),
            jnp.stack(new_k, axis=1), jnp.stack(new_v, axis=1),
            jnp.stack(new_sf, axis=1), jnp.stack(new_sb, axis=1))
```

```python
import functools
import math

import jax
import jax.numpy as jnp
import numpy as np
from jax import lax
from jax.experimental import pallas as pl
from jax.experimental.pallas import tpu as pltpu

F32 = jnp.float32
BF16 = jnp.bfloat16

EPS = 1e-6
HEAD_DIM = 128
N_HEADS = 8
KV_HEADS = 2
GQA_GROUP = N_HEADS // KV_HEADS
FOURIER_GROUPS = 4
FOURIER_GROUP_DIM = 256
FOURIER_WIDTH = FOURIER_GROUPS * FOURIER_GROUP_DIM
ATTN_WIDTH = N_HEADS * HEAD_DIM
KV_WIDTH = KV_HEADS * HEAD_DIM
GRID_W = 64
ROPE_THETA = 10000.0
ROPE_AXIS_DIM = HEAD_DIM // 2

DN_QK_HEADS = 16
DN_V_HEADS = 32
DN_K_DIM = 128
DN_V_DIM = 128
DN_QK_WIDTH = DN_QK_HEADS * DN_K_DIM
DN_V_WIDTH = DN_V_HEADS * DN_V_DIM
DN_CONV_CH = 2 * DN_QK_WIDTH + DN_V_WIDTH
DN_CHUNK = 64
CONV_PAD = (2, 1)

PEER_HEADS = 8
PEER_N_KEYS = 128
PEER_HALF = 128
PEER_TOPK = 16
PEER_COL_CHUNK = 256

VMEM_LIMIT = 48 * 1024 * 1024


def _cparams(sem):
    return pltpu.CompilerParams(dimension_semantics=sem, vmem_limit_bytes=VMEM_LIMIT)


def _bdot(a, b):
    return jnp.dot(a.astype(BF16), b.astype(BF16), preferred_element_type=F32)


def _bdot_nt(a, b):
    return lax.dot_general(a.astype(BF16), b.astype(BF16), (((1,), (1,)), ((), ())),
                           preferred_element_type=F32)


def _bmm(a, b):
    return lax.dot_general(a.astype(BF16), b.astype(BF16), (((2,), (1,)), ((0,), (0,))),
                           preferred_element_type=F32)


def _bmm_nt(a, b):
    return lax.dot_general(a.astype(BF16), b.astype(BF16), (((2,), (2,)), ((0,), (0,))),
                           preferred_element_type=F32)


def _proj_kernel(x_ref, w_ref, o_ref):
    o_ref[...] = _bdot(x_ref[...], w_ref[...]).astype(o_ref.dtype)


def proj(x, w, *, tm, tn, col0=0, ncols=None, out_dtype=F32, name="proj"):
    m, k = x.shape
    ncols = w.shape[1] - col0 if ncols is None else ncols
    tm = min(tm, m)
    tn = min(tn, ncols)
    assert m % tm == 0 and ncols % tn == 0 and col0 % tn == 0
    cb = col0 // tn
    return pl.pallas_call(
        _proj_kernel,
        grid=(m // tm, ncols // tn),
        in_specs=[pl.BlockSpec((tm, k), lambda i, j: (i, 0)),
                  pl.BlockSpec((k, tn), lambda i, j: (0, j + cb))],
        out_specs=pl.BlockSpec((tm, tn), lambda i, j: (i, j)),
        out_shape=jax.ShapeDtypeStruct((m, ncols), out_dtype),
        compiler_params=_cparams(("parallel", "arbitrary")),
        name=name,
    )(x, w)


def _proj_res_kernel(x_ref, w_ref, res_ref, gate_ref, o_ref):
    o_ref[...] = res_ref[...] + gate_ref[0] * _bdot(x_ref[...], w_ref[...])


def proj_residual(x, w, res, gates, group_of_tile, *, tm, tn, name="proj_res"):
    m, k = x.shape
    n = w.shape[1]
    tm = min(tm, m)
    tn = min(tn, n)
    assert m % tm == 0 and n % tn == 0
    return pl.pallas_call(
        _proj_res_kernel,
        grid=(m // tm, n // tn),
        in_specs=[pl.BlockSpec((tm, k), lambda i, j: (i, 0)),
                  pl.BlockSpec((k, tn), lambda i, j: (0, j)),
                  pl.BlockSpec((tm, tn), lambda i, j: (i, j)),
                  pl.BlockSpec((1, 1, tn), lambda i, j: (group_of_tile(i, tm), 0, j))],
        out_specs=pl.BlockSpec((tm, tn), lambda i, j: (i, j)),
        out_shape=jax.ShapeDtypeStruct((m, n), F32),
        compiler_params=_cparams(("parallel", "arbitrary")),
        name=name,
    )(x, w, res, gates)


def _modulate_kernel(x_ref, gain_ref, shift_ref, scale_ref, o_ref):
    x = x_ref[...]
    y = x * lax.rsqrt(jnp.mean(x * x, axis=-1, keepdims=True) + EPS)
    o_ref[...] = (y * gain_ref[...] * (1.0 + scale_ref[0]) + shift_ref[0]).astype(o_ref.dtype)


def modulate(x, gain, shift, scale, group_of_tile, *, tm, out_dtype=BF16):
    m, d = x.shape
    tm = min(tm, m)
    gspec = pl.BlockSpec((1, 1, d), lambda i: (group_of_tile(i, tm), 0, 0))
    return pl.pallas_call(
        _modulate_kernel,
        grid=(m // tm,),
        in_specs=[pl.BlockSpec((tm, d), lambda i: (i, 0)),
                  pl.BlockSpec((1, d), lambda i: (0, 0)), gspec, gspec],
        out_specs=pl.BlockSpec((tm, d), lambda i: (i, 0)),
        out_shape=jax.ShapeDtypeStruct((m, d), out_dtype),
        compiler_params=_cparams(("parallel",)),
        name="modulate",
    )(x, gain.reshape(1, d), shift, scale)


def _rmsnorm_kernel(x_ref, gain_ref, o_ref):
    x = x_ref[...]
    o_ref[...] = x * lax.rsqrt(jnp.mean(x * x, axis=-1, keepdims=True) + EPS) * gain_ref[...]


def rmsnorm_rows(x, gain, *, tm):
    m, d = x.shape
    tm = min(tm, m)
    return pl.pallas_call(
        _rmsnorm_kernel,
        grid=(m // tm,),
        in_specs=[pl.BlockSpec((tm, d), lambda i: (i, 0)), pl.BlockSpec((1, d), lambda i: (0, 0))],
        out_specs=pl.BlockSpec((tm, d), lambda i: (i, 0)),
        out_shape=jax.ShapeDtypeStruct((m, d), F32),
        compiler_params=_cparams(("parallel",)),
        name="final_norm",
    )(x, gain.reshape(1, d))


def _ada_kernel(c_ref, w_ref, b_ref, o_ref):
    c = c_ref[...]
    o_ref[0] = _bdot(c * jax.nn.sigmoid(c), w_ref[0]) + b_ref[0]


def ada_all(conds, ada_w, ada_b, *, tn=1024):
    depth, d, n = ada_w.shape
    r = conds.shape[0]
    return pl.pallas_call(
        _ada_kernel,
        grid=(depth, n // tn),
        in_specs=[pl.BlockSpec((r, d), lambda l, j: (0, 0)),
                  pl.BlockSpec((1, d, tn), lambda l, j: (l, 0, j)),
                  pl.BlockSpec((1, 1, tn), lambda l, j: (l, 0, j))],
        out_specs=pl.BlockSpec((1, r, tn), lambda l, j: (l, 0, j)),
        out_shape=jax.ShapeDtypeStruct((depth, r, n), F32),
        compiler_params=_cparams(("parallel", "arbitrary")),
        name="ada",
    )(conds, ada_w, ada_b.reshape(depth, 1, n))


def _dft_tables(n, scale):
    idx = np.arange(n, dtype=np.int64)
    ang = 2.0 * np.pi * ((idx[:, None] * idx[None, :]) % n).astype(np.float64) / n
    return np.stack([np.cos(ang) * scale, np.sin(ang) * scale]).astype(np.float32)


def _chan_dft_kernel(x_ref, t_ref, o_ref):
    o_ref[0] = _bdot(x_ref[...], t_ref[0]).astype(o_ref.dtype)


def _seq_dft_kernel(t_ref, r_ref, o_ref):
    o_ref[...] = (_bdot(t_ref[0], r_ref[0]) - _bdot(t_ref[1], r_ref[1])).astype(o_ref.dtype)


def fourier_mix(p, n_seq, seq_len, *, tm, out_dtype=BF16):
    t = n_seq * seq_len
    gd = FOURIER_GROUP_DIM
    tm = min(tm, t)
    ctab = jnp.asarray(_dft_tables(gd, gd ** -0.5), BF16)
    r = pl.pallas_call(
        _chan_dft_kernel,
        grid=(t // tm, FOURIER_GROUPS, 2),
        in_specs=[pl.BlockSpec((tm, gd), lambda i, g, s: (i, g)),
                  pl.BlockSpec((1, gd, gd), lambda i, g, s: (s, 0, 0))],
        out_specs=pl.BlockSpec((1, tm, gd), lambda i, g, s: (s, i, g)),
        out_shape=jax.ShapeDtypeStruct((2, t, FOURIER_WIDTH), BF16),
        compiler_params=_cparams(("parallel", "arbitrary", "arbitrary")),
        name="chan_dft",
    )(p, ctab)
    stab = jnp.asarray(_dft_tables(seq_len, seq_len ** -0.5), BF16)
    ts = min(512, seq_len)
    tn = 512
    nrow = seq_len // ts
    return pl.pallas_call(
        _seq_dft_kernel,
        grid=(n_seq, nrow, FOURIER_WIDTH // tn),
        in_specs=[pl.BlockSpec((2, ts, seq_len), lambda b, i, j: (0, i, 0)),
                  pl.BlockSpec((2, seq_len, tn), lambda b, i, j: (0, b, j))],
        out_specs=pl.BlockSpec((ts, tn), lambda b, i, j: (b * nrow + i, j)),
        out_shape=jax.ShapeDtypeStruct((t, FOURIER_WIDTH), out_dtype),
        compiler_params=_cparams(("parallel", "arbitrary", "arbitrary")),
        name="seq_dft",
    )(stab, r)


def _attn_kernel(q_ref, k_ref, v_ref, o_ref):
    k = k_ref[0].astype(BF16)
    v = v_ref[0].astype(BF16)
    scale = HEAD_DIM ** -0.5
    for g in range(GQA_GROUP):
        q = q_ref[0, :, g * HEAD_DIM:(g + 1) * HEAD_DIM]
        s = _bdot_nt(q, k) * scale
        s = s - jnp.max(s, axis=-1, keepdims=True)
        e = jnp.exp(s)
        p = e / jnp.sum(e, axis=-1, keepdims=True)
        o_ref[0, :, g * HEAD_DIM:(g + 1) * HEAD_DIM] = _bdot(p, v).astype(o_ref.dtype)


def attention(q, k, v, *, tq=256, out_dtype=BF16):
    b, sq, _ = q.shape
    sk = k.shape[1]
    tq = min(tq, sq)
    gw = GQA_GROUP * HEAD_DIM
    return pl.pallas_call(
        _attn_kernel,
        grid=(b, KV_HEADS, sq // tq),
        in_specs=[pl.BlockSpec((1, tq, gw), lambda bb, h, i: (bb, i, h)),
                  pl.BlockSpec((1, sk, HEAD_DIM), lambda bb, h, i: (bb, 0, h)),
                  pl.BlockSpec((1, sk, HEAD_DIM), lambda bb, h, i: (bb, 0, h))],
        out_specs=pl.BlockSpec((1, tq, gw), lambda bb, h, i: (bb, i, h)),
        out_shape=jax.ShapeDtypeStruct((b, sq, ATTN_WIDTH), out_dtype),
        compiler_params=_cparams(("parallel", "parallel", "arbitrary")),
        name="attention",
    )(q, k, v)


N_PROB = 4


def _tri_masks(c):
    ri = lax.broadcasted_iota(jnp.int32, (c, c), 0)
    ci = lax.broadcasted_iota(jnp.int32, (c, c), 1)
    return ri, ci


def _unit_tri_inverse(low, ri, ci):
    c = low.shape[-1]

    def same_block(size):
        sh = int(math.log2(size))
        return (ri >> sh) == (ci >> sh)

    leaf = 4
    dg = jnp.where(same_block(leaf), low, 0.0)
    imd = jnp.where(ri == ci, 1.0, 0.0) - dg
    inv = imd + _bmm(imd, _bmm(dg, dg))
    size = leaf
    while size < c:
        off = jnp.where(same_block(2 * size) & jnp.logical_not(same_block(size)), low, 0.0)
        inv = inv - _bmm(inv, _bmm(off, inv))
        size *= 2
    return inv


def _delta_kernel(q_ref, k_ref, v_ref, cols_ref, rows_ref, sf0_ref, sb0_ref, o_ref, sf_ref, sb_ref,
                  u_s, wq_s, akt_s, gl_s, *, n_chunks, unroll_a):
    c = DN_CHUNK
    dk = DN_K_DIM
    dv = DN_V_DIM
    ri, ci = _tri_masks(c)

    ca = unroll_a

    def body_a(g0, carry):
        rows = pl.ds(pl.multiple_of(g0 * (ca * c), ca * c), ca * c)
        chs = pl.ds(g0 * ca, ca)
        k = k_ref[0, rows, :].reshape(ca, c, dk)
        q = q_ref[0, rows, :].reshape(ca, c, dk)
        kk = _bmm_nt(k, k)
        qk = _bmm_nt(q, k)
        cols = cols_ref[0, 0, rows, :].reshape(ca, c, 2 * N_PROB)
        rws = rows_ref[0, 0, chs]
        lows, rhss, decays, egs, glasts, gcols = [], [], [], [], [], []
        for p in range(N_PROB):
            head, reverse = p % 2, p >= 2
            beta = cols[:, :, p:p + 1]
            gcol = cols[:, :, N_PROB + p:N_PROB + p + 1]
            grow = rws[:, p:p + 1, :]
            glast = grow[:, :, 0:1] if reverse else grow[:, :, c - 1:c]
            incl = (ri <= ci) if reverse else (ri >= ci)
            strict = (ri < ci) if reverse else (ri > ci)
            decay = jnp.where(incl, jnp.exp(jnp.where(incl, gcol - grow, 0.0)), 0.0)
            eg = jnp.exp(gcol)
            v = v_ref[0, rows, head * dv:(head + 1) * dv].reshape(ca, c, dv)
            lows.append(jnp.where(strict, kk * beta * decay, 0.0))
            rhss.append(jnp.concatenate([v * beta, k * (beta * eg)], axis=-1))
            decays.append(decay)
            egs.append(eg)
            glasts.append(glast)
            gcols.append(gcol)
        inv = _unit_tri_inverse(jnp.concatenate(lows, axis=0), ri, ci)
        sol = _bmm(inv, jnp.concatenate(rhss, axis=0))
        for p in range(N_PROB):
            sp = sol[p * ca:(p + 1) * ca]
            u_s[p, rows, :] = sp[:, :, :dv].reshape(ca * c, dv)
            wq_s[p, chs, 0:c, :] = sp[:, :, dv:].astype(BF16)
            wq_s[p, chs, c:2 * c, :] = (q * egs[p]).astype(BF16)
            akt_s[p, chs, 0:c, :] = (qk * decays[p]).astype(BF16)
            akt_s[p, chs, c:c + dk, :] = jnp.swapaxes(k * jnp.exp(glasts[p] - gcols[p]), 1, 2).astype(BF16)
            gl_s[p, chs] = jnp.broadcast_to(jnp.exp(glasts[p]), (ca, 8, dv))
        return carry

    lax.fori_loop(0, n_chunks // ca, body_a, 0)
    o_ref[...] = jnp.zeros_like(o_ref)

    def body_b(i, states):
        chs = [(n_chunks - 1 - i) if p >= 2 else i for p in range(N_PROB)]
        rows = [pl.ds(pl.multiple_of(ch * c, c), c) for ch in chs]
        ws = [_bdot(wq_s[p, chs[p]], states[p]) for p in range(N_PROB)]
        vn = [u_s[p, rows[p], :] - ws[p][0:c] for p in range(N_PROB)]
        av = [_bdot(akt_s[p, chs[p]], vn[p]) for p in range(N_PROB)]
        new_states = []
        for p in range(N_PROB):
            head = p % 2
            o_ref[0, rows[p], head * dv:(head + 1) * dv] += ws[p][c:2 * c] + av[p][0:c]
            new_states.append(states[p] * gl_s[p, chs[p]][0:1, :] + av[p][c:c + dk])
        return tuple(new_states)

    init = tuple((sb0_ref if p >= 2 else sf0_ref)[0, p % 2] for p in range(N_PROB))
    fin = lax.fori_loop(0, n_chunks, body_b, init)
    for p in range(N_PROB):
        (sb_ref if p >= 2 else sf_ref)[0, p % 2] = fin[p]


def delta_rule(q, k, v, beta, g, s_f0, s_b0):
    b, t, _ = v.shape
    h = DN_V_HEADS
    kh = DN_QK_HEADS
    rep = h // kh
    assert rep == 2
    c = DN_CHUNK
    nc = t // c
    gch = g.reshape(b, nc, c, 2, h)
    gcf = jnp.cumsum(gch[:, :, :, 0], axis=2).reshape(b, t, h)
    gcb = jnp.flip(jnp.cumsum(jnp.flip(gch[:, :, :, 1], axis=2), axis=2), axis=2).reshape(b, t, h)
    cols = jnp.stack([beta[:, :, 0], beta[:, :, 1], gcf, gcb], axis=2).reshape(b, t, 4, kh, rep)
    cols = jnp.transpose(cols, (0, 3, 1, 2, 4)).reshape(b, kh, t, 4 * rep)
    rows = jnp.stack([gcf, gcb], axis=2).reshape(b, nc, c, 2, kh, rep)
    rows = jnp.transpose(rows, (0, 4, 1, 3, 5, 2)).reshape(b, kh, nc, N_PROB, c)
    stspec = pl.BlockSpec((1, rep, DN_K_DIM, DN_V_DIM), lambda bb, hh: (bb, hh, 0, 0))
    unroll_a = math.gcd(nc, 4)
    o, sf, sb = pl.pallas_call(
        functools.partial(_delta_kernel, n_chunks=nc, unroll_a=unroll_a),
        grid=(b, kh),
        in_specs=[pl.BlockSpec((1, t, DN_K_DIM), lambda bb, hh: (bb, 0, hh)),
                  pl.BlockSpec((1, t, DN_K_DIM), lambda bb, hh: (bb, 0, hh)),
                  pl.BlockSpec((1, t, rep * DN_V_DIM), lambda bb, hh: (bb, 0, hh)),
                  pl.BlockSpec((1, 1, t, 4 * rep), lambda bb, hh: (bb, hh, 0, 0)),
                  pl.BlockSpec((1, 1, nc, N_PROB, c), lambda bb, hh: (bb, hh, 0, 0, 0)),
                  stspec, stspec],
        out_specs=[pl.BlockSpec((1, t, rep * DN_V_DIM), lambda bb, hh: (bb, 0, hh)), stspec, stspec],
        out_shape=[jax.ShapeDtypeStruct((b, t, DN_V_WIDTH), F32),
                   jax.ShapeDtypeStruct((b, h, DN_K_DIM, DN_V_DIM), F32),
                   jax.ShapeDtypeStruct((b, h, DN_K_DIM, DN_V_DIM), F32)],
        scratch_shapes=[pltpu.VMEM((N_PROB, t, DN_V_DIM), F32),
                        pltpu.VMEM((N_PROB, nc, 2 * c, DN_K_DIM), BF16),
                        pltpu.VMEM((N_PROB, nc, c + DN_K_DIM, c), BF16),
                        pltpu.VMEM((N_PROB, nc, 8, DN_V_DIM), F32)],
        compiler_params=_cparams(("parallel", "parallel")),
        name="delta_rule",
    )(q, k, v, cols, rows, s_f0, s_b0)
    return o, sf, sb


def _peer_route_kernel(qt_ref, keys_ref, nc_ref, re_ref, v1_ref, v2_ref):
    kk = PEER_TOPK
    neg = -jnp.inf
    for h in range(PEER_HEADS):
        r0 = h * 2 * PEER_HALF
        s1 = _bdot(keys_ref[0], qt_ref[r0:r0 + PEER_HALF, :])
        s2 = _bdot(keys_ref[1], qt_ref[r0 + PEER_HALF:r0 + 2 * PEER_HALF, :])
        sc = s1
        for r in range(kk):
            m = jnp.max(sc, axis=0, keepdims=True)
            v1_ref[r:r + 1, :] = m
            sc = jnp.where(sc == m, neg, sc)
        sc = s2
        rank2 = jnp.full(s2.shape, float(kk), F32)
        for r in range(kk):
            m = jnp.max(sc, axis=0, keepdims=True)
            v2_ref[r:r + 1, :] = m
            hit = sc == m
            rank2 = jnp.where(hit, float(r), rank2)
            sc = jnp.where(hit, neg, sc)
        v2 = v2_ref[...]
        cand = jnp.concatenate([v1_ref[0:1, :] + v2] + [v1_ref[a:a + 1, :] + v2[0:8] for a in range(1, kk)],
                               axis=0)
        top = cand[0:1, :]
        work = cand
        tau = top
        for r in range(kk):
            tau = jnp.max(work, axis=0, keepdims=True)
            work = jnp.where(work == tau, neg, work)
        sel = cand >= tau
        z = jnp.sum(jnp.where(sel, jnp.exp(cand - top), 0.0), axis=0, keepdims=True)
        one = jnp.where(sel, 1.0, 0.0)
        n1 = jnp.zeros(s1.shape, F32)
        for a in range(kk):
            lo, hi = (0, kk) if a == 0 else (kk + 8 * (a - 1), kk + 8 * a)
            cnt = jnp.sum(one[lo:hi], axis=0, keepdims=True)
            n1 = jnp.where(s1 == v1_ref[a:a + 1, :], cnt, n1)
        nc_ref[0, h] = n1
        nc_ref[1, h] = jnp.exp(s1 - v1_ref[0:1, :]) / z
        re_ref[0, h] = rank2.astype(BF16)
        re_ref[1, h] = jnp.exp(s2 - v2[0:1, :]).astype(BF16)


def peer_route(qt, sub_keys, *, tb):
    d2, t = qt.shape
    tb = min(tb, t)
    tspec = pl.BlockSpec((2, PEER_HEADS, PEER_N_KEYS, tb), lambda j: (0, 0, 0, j))
    return pl.pallas_call(
        _peer_route_kernel,
        grid=(t // tb,),
        in_specs=[pl.BlockSpec((d2, tb), lambda j: (0, j)),
                  pl.BlockSpec((2, PEER_N_KEYS, PEER_HALF), lambda j: (0, 0, 0))],
        out_specs=[tspec, tspec],
        out_shape=[jax.ShapeDtypeStruct((2, PEER_HEADS, PEER_N_KEYS, t), F32),
                   jax.ShapeDtypeStruct((2, PEER_HEADS, PEER_N_KEYS, t), BF16)],
        scratch_shapes=[pltpu.VMEM((PEER_TOPK, tb), F32), pltpu.VMEM((PEER_TOPK, tb), F32)],
        compiler_params=_cparams(("parallel",)),
        name="peer_route",
    )(qt, sub_keys)


def _peer_dense_kernel(xt_ref, wd_ref, wut_ref, nc_ref, re_ref, o_ref, act_s, *, rows_per_block, n_blocks):
    i = pl.program_id(1)

    @pl.when(i == 0)
    def _():
        o_ref[...] = jnp.zeros_like(o_ref)
        act_s[...] = jnp.zeros_like(act_s)

    blk = jnp.minimum(i, n_blocks - 1)
    tb = xt_ref.shape[1]
    cw = min(PEER_COL_CHUNK, tb)
    e1s = pl.ds(pl.multiple_of(blk * rows_per_block, rows_per_block), rows_per_block)
    n1_all = [nc_ref[0, h, e1s, :] for h in range(PEER_HEADS)]
    c1_all = [nc_ref[1, h, e1s, :] for h in range(PEER_HEADS)]
    for c0 in range(0, tb, cw):
        cs = slice(c0, c0 + cw)
        hpre = jnp.dot(wd_ref[...], xt_ref[:, cs], preferred_element_type=F32)
        o_ref[:, cs] += jnp.dot(wut_ref[...], act_s[:, cs], preferred_element_type=F32)
        gates = []
        for r in range(rows_per_block):
            acc = None
            for h in range(PEER_HEADS):
                n1 = n1_all[h][r:r + 1, cs].astype(BF16)
                c1 = c1_all[h][r:r + 1, cs].astype(BF16)
                wgt = jnp.where(re_ref[0, h, :, cs] < n1, re_ref[1, h, :, cs] * c1, jnp.zeros((), BF16))
                acc = wgt if acc is None else acc + wgt
            gates.append(acc)
        gate = jnp.concatenate(gates, axis=0).astype(F32)
        act = 0.5 * hpre * (1.0 + lax.erf(hpre * (2.0 ** -0.5))) * gate
        act_s[:, cs] = act.astype(BF16)


def peer_dense(xt, w_down, w_up_t, nc, re, *, tb, eb):
    d, t = xt.shape
    ne = w_down.shape[0]
    tb = min(tb, t)
    assert eb % PEER_N_KEYS == 0
    nblk = ne // eb
    tspec = pl.BlockSpec((2, PEER_HEADS, PEER_N_KEYS, tb), lambda j, i: (0, 0, 0, j))
    return pl.pallas_call(
        functools.partial(_peer_dense_kernel, rows_per_block=eb // PEER_N_KEYS, n_blocks=nblk),
        grid=(t // tb, nblk + 1),
        in_specs=[pl.BlockSpec((d, tb), lambda j, i: (0, j)),
                  pl.BlockSpec((eb, d), lambda j, i: (jnp.minimum(i, nblk - 1), 0)),
                  pl.BlockSpec((d, eb), lambda j, i: (0, jnp.maximum(i - 1, 0))),
                  tspec, tspec],
        out_specs=pl.BlockSpec((d, tb), lambda j, i: (0, j)),
        out_shape=jax.ShapeDtypeStruct((d, t), F32),
        scratch_shapes=[pltpu.VMEM((eb, tb), BF16)],
        compiler_params=_cparams(("parallel", "arbitrary")),
        name="peer_dense",
    )(xt, w_down, w_up_t, nc, re)


def peer(u, w_q, sub_keys, w_down, w_up, *, tb=512, eb=1024):
    ut = u.T
    qt = proj(w_q.T, ut, tm=512, tn=1024, name="peer_q")
    nc, re = peer_route(qt, sub_keys, tb=256)
    out_t = peer_dense(ut, w_down.astype(BF16), w_up.T.astype(BF16), nc, re, tb=tb, eb=eb)
    return out_t.T


def _rms(x, gain):
    return x * lax.rsqrt(jnp.mean(x * x, axis=-1, keepdims=True) + EPS) * gain


def _grid_angles(n_tokens):
    n_rows = n_tokens // GRID_W
    rows = jnp.repeat(jnp.arange(n_rows), GRID_W).astype(F32)
    cols = jnp.tile(jnp.arange(GRID_W), n_rows).astype(F32)
    n_freq = ROPE_AXIS_DIM // 2
    inv = ROPE_THETA ** (-jnp.arange(n_freq, dtype=F32) / n_freq)
    return rows[:, None] * inv, cols[:, None] * inv


def _rotate(x, ang):
    half = x.shape[-1] // 2
    x1, x2 = x[..., :half], x[..., half:]
    cos = jnp.cos(ang)[None, :, None, :]
    sin = jnp.sin(ang)[None, :, None, :]
    return jnp.concatenate([x1 * cos - x2 * sin, x1 * sin + x2 * cos], axis=-1)


def _axial_rope(x, ang_row, ang_col):
    return jnp.concatenate([_rotate(x[..., :ROPE_AXIS_DIM], ang_row),
                            _rotate(x[..., ROPE_AXIS_DIM:], ang_col)], axis=-1)


def _short_conv(x, w):
    y = lax.conv_general_dilated(x, w[:, None, :], window_strides=(1,), padding=[CONV_PAD],
                                 dimension_numbers=('NWC', 'WIO', 'NWC'), feature_group_count=x.shape[-1])
    return jax.nn.silu(y)


def _l2n(x):
    return x * lax.rsqrt(jnp.sum(x * x, axis=-1, keepdims=True) + EPS)


def kernel(x_prompt, x_sample, cache_k, cache_v, state_fwd, state_bwd, c, c_ctx, ada_w, ada_b, norm1, norm2,
           af_w_in, af_q_norm, af_k_norm, af_w_out, dn_w_in, dn_conv_w, dn_a_log, dn_dt_bias, dn_o_norm,
           dn_w_out, peer_w_q, peer_sub_keys, peer_w_down, peer_w_up, final_norm):
    nb, seq, d = x_prompt.shape
    db, dseq, _ = x_sample.shape
    depth = ada_w.shape[0]
    tp = nb * seq
    ts = db * dseq
    tt = tp + ts
    tm = math.gcd(math.gcd(tp, dseq), 1024)

    def group_of_tile(i, tile):
        r = i * tile
        return jnp.where(r < tp, 0, 1 + (r - tp) // dseq)

    conds = jnp.concatenate([c_ctx[None], c, jnp.zeros((8 - 1 - db, d), F32)], axis=0)
    mods = ada_all(conds, ada_w, ada_b)
    mods = mods.reshape(depth, 8, 6, d)

    x = jnp.concatenate([x_prompt.reshape(tp, d), x_sample.reshape(ts, d)], axis=0)
    new_k, new_v, new_sf, new_sb = [], [], [], []
    for i in range(depth):
        j = i // 2
        md = [mods[i, :, n][:, None, :] for n in range(6)]
        sh1, sc1, g1, sh2, sc2, g2 = md
        u = modulate(x, norm1[i], sh1, sc1, group_of_tile, tm=tm)
        if i % 2 == 0:
            p = proj(u, af_w_in[j], tm=tm, tn=512, name="af_in")
            q = _rms(p[:, FOURIER_WIDTH:FOURIER_WIDTH + ATTN_WIDTH].reshape(tt, N_HEADS, HEAD_DIM), af_q_norm[j])
            kv0 = FOURIER_WIDTH + ATTN_WIDTH
            k = _rms(p[:, kv0:kv0 + KV_WIDTH].reshape(tt, KV_HEADS, HEAD_DIM), af_k_norm[j])
            v = p[:, kv0 + KV_WIDTH:].reshape(tt, KV_HEADS, HEAD_DIM)
            new_k.append(k[:tp].reshape(nb, seq, KV_HEADS, HEAD_DIM))
            new_v.append(v[:tp].reshape(nb, seq, KV_HEADS, HEAD_DIM))
            ang_row, ang_col = _grid_angles(dseq)
            qs = _axial_rope(q[tp:].reshape(db, dseq, N_HEADS, HEAD_DIM), ang_row, ang_col)
            ks = _axial_rope(k[tp:].reshape(db, dseq, KV_HEADS, HEAD_DIM), ang_row, ang_col)
            keys = jnp.concatenate([ks, cache_k[:, j]], axis=1).reshape(db, -1, KV_WIDTH)
            vals = jnp.concatenate([v[tp:].reshape(db, dseq, KV_HEADS, HEAD_DIM), cache_v[:, j]],
                                   axis=1).reshape(db, -1, KV_WIDTH)
            ap = attention(q[:tp].reshape(nb, seq, ATTN_WIDTH), k[:tp].reshape(nb, seq, KV_WIDTH),
                           v[:tp].reshape(nb, seq, KV_WIDTH))
            asmp = attention(qs.reshape(db, dseq, ATTN_WIDTH), keys, vals)
            fp = fourier_mix(p[:tp], nb, seq, tm=tm)
            fs = fourier_mix(p[tp:], db, dseq, tm=tm)
            mix = jnp.concatenate([jnp.concatenate([fp, fs], axis=0),
                                   jnp.concatenate([ap.reshape(tp, ATTN_WIDTH), asmp.reshape(ts, ATTN_WIDTH)], axis=0)],
                                  axis=1)
            x = proj_residual(mix, af_w_out[j], x, g1, group_of_tile, tm=tm, tn=512, name="af_out")
        else:
            w_in = dn_w_in[j]
            p = proj(u, w_in, tm=tm, tn=512, ncols=DN_CONV_CH + DN_V_WIDTH, name="dn_in")
            ba = proj(u, w_in, tm=tm, tn=128, col0=DN_CONV_CH + DN_V_WIDTH, name="dn_in_ba")
            z = p[:, DN_CONV_CH:]
            beta_all = jax.nn.sigmoid(ba.reshape(tt, 2, 2, DN_V_HEADS)[:, 0])
            g_all = -jnp.exp(dn_a_log[j]) * jax.nn.softplus(ba.reshape(tt, 2, 2, DN_V_HEADS)[:, 1] + dn_dt_bias[j])
            outs = []
            for (r0, r1, b_, t_, s_f0, s_b0) in (
                    (0, tp, nb, seq, jnp.zeros((nb, DN_V_HEADS, DN_K_DIM, DN_V_DIM), F32),
                     jnp.zeros((nb, DN_V_HEADS, DN_K_DIM, DN_V_DIM), F32)),
                    (tp, tt, db, dseq, state_fwd[:, j], state_bwd[:, j])):
                qkv = _short_conv(p[r0:r1, :DN_CONV_CH].reshape(b_, t_, DN_CONV_CH), dn_conv_w[j])
                qq = _l2n(qkv[..., :DN_QK_WIDTH].reshape(b_, t_, DN_QK_HEADS, DN_K_DIM)) * (DN_K_DIM ** -0.5)
                kk = _l2n(qkv[..., DN_QK_WIDTH:2 * DN_QK_WIDTH].reshape(b_, t_, DN_QK_HEADS, DN_K_DIM))
                vv = qkv[..., 2 * DN_QK_WIDTH:]
                o, s_f, s_b = delta_rule(qq.reshape(b_, t_, DN_QK_WIDTH), kk.reshape(b_, t_, DN_QK_WIDTH), vv,
                                         beta_all[r0:r1].reshape(b_, t_, 2, DN_V_HEADS),
                                         g_all[r0:r1].reshape(b_, t_, 2, DN_V_HEADS), s_f0, s_b0)
                outs.append(o.reshape(b_ * t_, DN_V_HEADS, DN_V_DIM))
                if r0 == 0:
                    new_sf.append(s_f)
                    new_sb.append(s_b)
            o = jnp.concatenate(outs, axis=0)
            on = (_rms(o, dn_o_norm[j]) * jax.nn.silu(z.reshape(tt, DN_V_HEADS, DN_V_DIM))).reshape(tt, DN_V_WIDTH)
            x = proj_residual(on.astype(BF16), dn_w_out[j], x, g1, group_of_tile, tm=tm, tn=256, name="dn_out")
        u2 = modulate(x, norm2[i], sh2, sc2, group_of_tile, tm=tm)
        pe = peer(u2, peer_w_q[i], peer_sub_keys[i], peer_w_down[i], peer_w_up[i])
        gate2 = jnp.concatenate([jnp.broadcast_to(g2[0], (tp, d))] +
                                [jnp.broadcast_to(g2[1 + b_], (dseq, d)) for b_ in range(db)], axis=0)
        x = x + gate2 * pe
    y = rmsnorm_rows(x, final_norm, tm=tm)
    return (y[:tp].reshape(nb, seq, d), y[tp:].reshape(db, dseq, d),
            jnp.stack(new_k, axis=1), jnp.stack(new_v, axis=1),
            jnp.stack(new_sf, axis=1), jnp.stack(new_sb, axis=1))
```

```python
import functools
import math

import jax
import jax.numpy as jnp
import numpy as np
from jax import lax
from jax.experimental import pallas as pl
from jax.experimental.pallas import tpu as pltpu

F32 = jnp.float32
BF16 = jnp.bfloat16

EPS = 1e-6
HEAD_DIM = 128
N_HEADS = 8
KV_HEADS = 2
GQA_GROUP = N_HEADS // KV_HEADS
FOURIER_GROUPS = 4
FOURIER_GROUP_DIM = 256
FOURIER_WIDTH = FOURIER_GROUPS * FOURIER_GROUP_DIM
ATTN_WIDTH = N_HEADS * HEAD_DIM
KV_WIDTH = KV_HEADS * HEAD_DIM
GRID_W = 64
ROPE_THETA = 10000.0
ROPE_AXIS_DIM = HEAD_DIM // 2

DN_QK_HEADS = 16
DN_V_HEADS = 32
DN_K_DIM = 128
DN_V_DIM = 128
DN_QK_WIDTH = DN_QK_HEADS * DN_K_DIM
DN_V_WIDTH = DN_V_HEADS * DN_V_DIM
DN_CONV_CH = 2 * DN_QK_WIDTH + DN_V_WIDTH
DN_CHUNK = 64
CONV_PAD = (2, 1)

PEER_HEADS = 8
PEER_N_KEYS = 128
PEER_HALF = 128
PEER_TOPK = 16
PEER_COL_CHUNK = 256

VMEM_LIMIT = 48 * 1024 * 1024


def _cparams(sem):
    return pltpu.CompilerParams(dimension_semantics=sem, vmem_limit_bytes=VMEM_LIMIT)


def _bdot(a, b):
    return jnp.dot(a.astype(BF16), b.astype(BF16), preferred_element_type=F32)


def _bdot_nt(a, b):
    return lax.dot_general(a.astype(BF16), b.astype(BF16), (((1,), (1,)), ((), ())),
                           preferred_element_type=F32)


def _bmm(a, b):
    return lax.dot_general(a.astype(BF16), b.astype(BF16), (((2,), (1,)), ((0,), (0,))),
                           preferred_element_type=F32)


def _bmm_nt(a, b):
    return lax.dot_general(a.astype(BF16), b.astype(BF16), (((2,), (2,)), ((0,), (0,))),
                           preferred_element_type=F32)


def _proj_kernel(x_ref, w_ref, o_ref):
    o_ref[...] = _bdot(x_ref[...], w_ref[0]).astype(o_ref.dtype)


def proj(x, w, *, layer=0, tm, tn, col0=0, ncols=None, out_dtype=F32, name="proj"):
    m, k = x.shape
    ncols = w.shape[2] - col0 if ncols is None else ncols
    tm = min(tm, m)
    tn = min(tn, ncols)
    assert m % tm == 0 and ncols % tn == 0 and col0 % tn == 0
    cb = col0 // tn
    return pl.pallas_call(
        _proj_kernel,
        grid=(m // tm, ncols // tn),
        in_specs=[pl.BlockSpec((tm, k), lambda i, j: (i, 0)),
                  pl.BlockSpec((1, k, tn), lambda i, j: (layer, 0, j + cb))],
        out_specs=pl.BlockSpec((tm, tn), lambda i, j: (i, j)),
        out_shape=jax.ShapeDtypeStruct((m, ncols), out_dtype),
        compiler_params=_cparams(("parallel", "arbitrary")),
        name=name,
    )(x, w)


def _proj_res_kernel(*refs, n_parts):
    x_refs, w_refs = refs[:n_parts], refs[n_parts:2 * n_parts]
    res_ref, gate_ref, o_ref = refs[2 * n_parts:]
    acc = _bdot(x_refs[0][...], w_refs[0][0])
    for x_ref, w_ref in zip(x_refs[1:], w_refs[1:]):
        acc = acc + _bdot(x_ref[...], w_ref[0])
    o_ref[...] = res_ref[...] + gate_ref[0] * acc


def proj_residual(xs, w, res, gates, group_of_tile, *, layer=0, tm, tn, name="proj_res"):
    m = xs[0].shape[0]
    n = w.shape[2]
    tm = min(tm, m)
    tn = min(tn, n)
    kp = xs[0].shape[1]
    assert m % tm == 0 and n % tn == 0 and all(x.shape == (m, kp) for x in xs)
    x_specs = [pl.BlockSpec((tm, kp), lambda i, j: (i, 0)) for _ in xs]
    w_specs = [pl.BlockSpec((1, kp, tn), functools.partial(lambda i, j, part: (layer, part, j), part=part))
               for part in range(len(xs))]
    return pl.pallas_call(
        functools.partial(_proj_res_kernel, n_parts=len(xs)),
        grid=(m // tm, n // tn),
        in_specs=x_specs + w_specs + [pl.BlockSpec((tm, tn), lambda i, j: (i, j)),
                                      pl.BlockSpec((1, 1, tn), lambda i, j: (group_of_tile(i, tm), 0, j))],
        out_specs=pl.BlockSpec((tm, tn), lambda i, j: (i, j)),
        out_shape=jax.ShapeDtypeStruct((m, n), F32),
        compiler_params=_cparams(("parallel", "arbitrary")),
        name=name,
    )(*xs, *([w] * len(xs)), res, gates)


def _modulate_kernel(x_ref, gain_ref, shift_ref, scale_ref, o_ref):
    x = x_ref[...]
    y = x * lax.rsqrt(jnp.mean(x * x, axis=-1, keepdims=True) + EPS)
    o_ref[...] = (y * gain_ref[...] * (1.0 + scale_ref[0]) + shift_ref[0]).astype(o_ref.dtype)


def modulate(x, gain, shift, scale, group_of_tile, *, tm, out_dtype=BF16):
    m, d = x.shape
    tm = min(tm, m)
    gspec = pl.BlockSpec((1, 1, d), lambda i: (group_of_tile(i, tm), 0, 0))
    return pl.pallas_call(
        _modulate_kernel,
        grid=(m // tm,),
        in_specs=[pl.BlockSpec((tm, d), lambda i: (i, 0)),
                  pl.BlockSpec((1, d), lambda i: (0, 0)), gspec, gspec],
        out_specs=pl.BlockSpec((tm, d), lambda i: (i, 0)),
        out_shape=jax.ShapeDtypeStruct((m, d), out_dtype),
        compiler_params=_cparams(("parallel",)),
        name="modulate",
    )(x, gain.reshape(1, d), shift, scale)


def _rmsnorm_kernel(x_ref, gain_ref, o_ref):
    x = x_ref[...]
    o_ref[...] = x * lax.rsqrt(jnp.mean(x * x, axis=-1, keepdims=True) + EPS) * gain_ref[...]


def rmsnorm_rows(x, gain, *, row0, nrows, tm):
    d = x.shape[1]
    tm = min(tm, nrows)
    assert row0 % tm == 0 and nrows % tm == 0
    rb0 = row0 // tm
    return pl.pallas_call(
        _rmsnorm_kernel,
        grid=(nrows // tm,),
        in_specs=[pl.BlockSpec((tm, d), lambda i: (rb0 + i, 0)), pl.BlockSpec((1, d), lambda i: (0, 0))],
        out_specs=pl.BlockSpec((tm, d), lambda i: (i, 0)),
        out_shape=jax.ShapeDtypeStruct((nrows, d), F32),
        compiler_params=_cparams(("parallel",)),
        name="final_norm",
    )(x, gain.reshape(1, d))


def _ada_kernel(c_ref, w_ref, b_ref, o_ref):
    c = c_ref[...]
    o_ref[0] = _bdot(c * jax.nn.sigmoid(c), w_ref[0]) + b_ref[0]


def ada_all(conds, ada_w, ada_b, *, tn=1024):
    depth, d, n = ada_w.shape
    r = conds.shape[0]
    return pl.pallas_call(
        _ada_kernel,
        grid=(depth, n // tn),
        in_specs=[pl.BlockSpec((r, d), lambda l, j: (0, 0)),
                  pl.BlockSpec((1, d, tn), lambda l, j: (l, 0, j)),
                  pl.BlockSpec((1, 1, tn), lambda l, j: (l, 0, j))],
        out_specs=pl.BlockSpec((1, r, tn), lambda l, j: (l, 0, j)),
        out_shape=jax.ShapeDtypeStruct((depth, r, n), F32),
        compiler_params=_cparams(("parallel", "arbitrary")),
        name="ada",
    )(conds, ada_w, ada_b.reshape(depth, 1, n))


def _dft_tables(n, scale):
    idx = np.arange(n, dtype=np.int64)
    ang = 2.0 * np.pi * ((idx[:, None] * idx[None, :]) % n).astype(np.float64) / n
    return np.stack([np.cos(ang) * scale, np.sin(ang) * scale]).astype(np.float32)


def _chan_dft_kernel(x_ref, t_ref, o_ref):
    o_ref[0] = _bdot(x_ref[...], t_ref[0]).astype(o_ref.dtype)


def _seq_dft_kernel(t_ref, r_ref, *rest):
    o_ref = rest[-1]
    o_ref[...] = (_bdot(t_ref[0], r_ref[0]) - _bdot(t_ref[1], r_ref[1])).astype(o_ref.dtype)


def chan_dft(p, *, tm):
    t = p.shape[0]
    gd = FOURIER_GROUP_DIM
    tm = min(tm, t)
    ctab = jnp.asarray(_dft_tables(gd, gd ** -0.5), BF16)
    return pl.pallas_call(
        _chan_dft_kernel,
        grid=(t // tm, FOURIER_GROUPS, 2),
        in_specs=[pl.BlockSpec((tm, gd), lambda i, g, s: (i, g)),
                  pl.BlockSpec((1, gd, gd), lambda i, g, s: (s, 0, 0))],
        out_specs=pl.BlockSpec((1, tm, gd), lambda i, g, s: (s, i, g)),
        out_shape=jax.ShapeDtypeStruct((2, t, FOURIER_WIDTH), BF16),
        compiler_params=_cparams(("parallel", "arbitrary", "arbitrary")),
        name="chan_dft",
    )(p, ctab)


def seq_dft(r, *, n_seq, seq_len, row0, prev_out=None, out_dtype=BF16):
    t = r.shape[1]
    assert row0 % seq_len == 0
    stab = jnp.asarray(_dft_tables(seq_len, seq_len ** -0.5), BF16)
    ts = min(512, seq_len)
    tn = 512
    nrow = seq_len // ts
    sb0 = row0 // seq_len
    rb0 = row0 // ts
    in_specs = [pl.BlockSpec((2, ts, seq_len), lambda b, i, j: (0, i, 0)),
                pl.BlockSpec((2, seq_len, tn), lambda b, i, j: (0, sb0 + b, j))]
    args = [stab, r]
    aliases = {}
    if prev_out is not None:
        in_specs.append(pl.BlockSpec(memory_space=pl.ANY))
        args.append(prev_out)
        aliases = {2: 0}
    return pl.pallas_call(
        _seq_dft_kernel,
        grid=(n_seq, nrow, FOURIER_WIDTH // tn),
        in_specs=in_specs,
        out_specs=pl.BlockSpec((ts, tn), lambda b, i, j: (rb0 + b * nrow + i, j)),
        out_shape=jax.ShapeDtypeStruct((t, FOURIER_WIDTH), out_dtype),
        input_output_aliases=aliases,
        compiler_params=_cparams(("parallel", "arbitrary", "arbitrary")),
        name="seq_dft",
    )(*args)


def _attn_kernel(q_ref, k_ref, v_ref, *rest):
    o_ref = rest[-1]
    k = k_ref[0].astype(BF16)
    v = v_ref[0].astype(BF16)
    scale = HEAD_DIM ** -0.5
    for g in range(GQA_GROUP):
        q = q_ref[0, :, g * HEAD_DIM:(g + 1) * HEAD_DIM]
        s = _bdot_nt(q, k) * scale
        s = s - jnp.max(s, axis=-1, keepdims=True)
        e = jnp.exp(s)
        p = e / jnp.sum(e, axis=-1, keepdims=True)
        o_ref[:, g * HEAD_DIM:(g + 1) * HEAD_DIM] = _bdot(p, v).astype(o_ref.dtype)


def attention(q, k, v, *, total_rows, row0, prev_out=None, tq=256, out_dtype=BF16):
    b, sq, _ = q.shape
    sk = k.shape[1]
    tq = min(tq, sq)
    assert row0 % tq == 0
    gw = GQA_GROUP * HEAD_DIM
    nq = sq // tq
    rb0 = row0 // tq
    in_specs = [pl.BlockSpec((1, tq, gw), lambda bb, h, i: (bb, i, h)),
                pl.BlockSpec((1, sk, HEAD_DIM), lambda bb, h, i: (bb, 0, h)),
                pl.BlockSpec((1, sk, HEAD_DIM), lambda bb, h, i: (bb, 0, h))]
    args = [q, k, v]
    aliases = {}
    if prev_out is not None:
        in_specs.append(pl.BlockSpec(memory_space=pl.ANY))
        args.append(prev_out)
        aliases = {3: 0}
    return pl.pallas_call(
        _attn_kernel,
        grid=(b, KV_HEADS, nq),
        in_specs=in_specs,
        out_specs=pl.BlockSpec((tq, gw), lambda bb, h, i: (rb0 + bb * nq + i, h)),
        out_shape=jax.ShapeDtypeStruct((total_rows, ATTN_WIDTH), out_dtype),
        input_output_aliases=aliases,
        compiler_params=_cparams(("parallel", "parallel", "arbitrary")),
        name="attention",
    )(*args)


N_PROB = 4


def _tri_masks(c):
    ri = lax.broadcasted_iota(jnp.int32, (c, c), 0)
    ci = lax.broadcasted_iota(jnp.int32, (c, c), 1)
    return ri, ci


def _unit_tri_inverse(low, ri, ci):
    c = low.shape[-1]

    def same_block(size):
        sh = int(math.log2(size))
        return (ri >> sh) == (ci >> sh)

    leaf = 4
    dg = jnp.where(same_block(leaf), low, 0.0)
    imd = jnp.where(ri == ci, 1.0, 0.0) - dg
    inv = imd + _bmm(imd, _bmm(dg, dg))
    size = leaf
    while size < c:
        off = jnp.where(same_block(2 * size) & jnp.logical_not(same_block(size)), low, 0.0)
        inv = inv - _bmm(inv, _bmm(off, inv))
        size *= 2
    return inv


def _delta_kernel(*refs, n_chunks, unroll_a, has_prev):
    n_in = 9 + int(has_prev)
    q_ref, k_ref, v_ref, z_ref, gain_ref, cols_ref, rows_ref, sf0_ref, sb0_ref = refs[:9]
    on_ref, sf_ref, sb_ref, u_s, wq_s, akt_s, gl_s, o_s = refs[n_in:]
    c = DN_CHUNK
    dk = DN_K_DIM
    dv = DN_V_DIM
    ri, ci = _tri_masks(c)

    ca = unroll_a

    def body_a(g0, carry):
        rows = pl.ds(pl.multiple_of(g0 * (ca * c), ca * c), ca * c)
        chs = pl.ds(g0 * ca, ca)
        k = k_ref[rows, :].astype(F32).reshape(ca, c, dk)
        q = q_ref[rows, :].astype(F32).reshape(ca, c, dk)
        kk = _bmm_nt(k, k)
        qk = _bmm_nt(q, k)
        cols = cols_ref[0, 0, rows, :].reshape(ca, c, 2 * N_PROB)
        rws = rows_ref[0, 0, chs]
        lows, rhss, decays, egs, glasts, gcols = [], [], [], [], [], []
        for p in range(N_PROB):
            head, reverse = p % 2, p >= 2
            beta = cols[:, :, p:p + 1]
            gcol = cols[:, :, N_PROB + p:N_PROB + p + 1]
            grow = rws[:, p:p + 1, :]
            glast = grow[:, :, 0:1] if reverse else grow[:, :, c - 1:c]
            incl = (ri <= ci) if reverse else (ri >= ci)
            strict = (ri < ci) if reverse else (ri > ci)
            decay = jnp.where(incl, jnp.exp(jnp.where(incl, gcol - grow, 0.0)), 0.0)
            eg = jnp.exp(gcol)
            v = v_ref[rows, head * dv:(head + 1) * dv].astype(F32).reshape(ca, c, dv)
            lows.append(jnp.where(strict, kk * beta * decay, 0.0))
            rhss.append(jnp.concatenate([v * beta, k * (beta * eg)], axis=-1))
            decays.append(decay)
            egs.append(eg)
            glasts.append(glast)
            gcols.append(gcol)
        inv = _unit_tri_inverse(jnp.concatenate(lows, axis=0), ri, ci)
        sol = _bmm(inv, jnp.concatenate(rhss, axis=0))
        for p in range(N_PROB):
            sp = sol[p * ca:(p + 1) * ca]
            u_s[p, rows, :] = sp[:, :, :dv].reshape(ca * c, dv)
            wq_s[p, chs, 0:c, :] = sp[:, :, dv:].astype(BF16)
            wq_s[p, chs, c:2 * c, :] = (q * egs[p]).astype(BF16)
            akt_s[p, chs, 0:c, :] = (qk * decays[p]).astype(BF16)
            akt_s[p, chs, c:c + dk, :] = jnp.swapaxes(k * jnp.exp(glasts[p] - gcols[p]), 1, 2).astype(BF16)
            gl_s[p, chs] = jnp.broadcast_to(jnp.exp(glasts[p]), (ca, 8, dv))
        return carry

    lax.fori_loop(0, n_chunks // ca, body_a, 0)
    o_s[...] = jnp.zeros_like(o_s)

    def body_b(i, states):
        chs = [(n_chunks - 1 - i) if p >= 2 else i for p in range(N_PROB)]
        rows = [pl.ds(pl.multiple_of(ch * c, c), c) for ch in chs]
        ws = [_bdot(wq_s[p, chs[p]], states[p]) for p in range(N_PROB)]
        vn = [u_s[p, rows[p], :] - ws[p][0:c] for p in range(N_PROB)]
        av = [_bdot(akt_s[p, chs[p]], vn[p]) for p in range(N_PROB)]
        new_states = []
        for p in range(N_PROB):
            head = p % 2
            o_s[rows[p], head * dv:(head + 1) * dv] += ws[p][c:2 * c] + av[p][0:c]
            new_states.append(states[p] * gl_s[p, chs[p]][0:1, :] + av[p][c:c + dk])
        return tuple(new_states)

    init = tuple((sb0_ref if p >= 2 else sf0_ref)[0, p % 2] for p in range(N_PROB))
    fin = lax.fori_loop(0, n_chunks, body_b, init)
    for p in range(N_PROB):
        (sb_ref if p >= 2 else sf_ref)[0, p % 2] = fin[p]
    for head in range(N_PROB // 2):
        hs = slice(head * dv, (head + 1) * dv)
        o = o_s[:, hs]
        z = z_ref[:, hs]
        on = o * lax.rsqrt(jnp.mean(o * o, axis=-1, keepdims=True) + EPS) * gain_ref[...]
        on_ref[:, hs] = (on * (z * jax.nn.sigmoid(z))).astype(on_ref.dtype)


def delta_mixer(qkv, p, beta, g, gain, s_f0, s_b0, *, n_seq, seq_len, row0, prev_out=None):
    b, t = n_seq, seq_len
    tt = qkv.shape[0]
    h = DN_V_HEADS
    kh = DN_QK_HEADS
    rep = h // kh
    assert rep == 2 and row0 % t == 0
    rb0 = row0 // t
    c = DN_CHUNK
    nc = t // c
    gch = g.reshape(b, nc, c, 2, h)
    gcf = jnp.cumsum(gch[:, :, :, 0], axis=2).reshape(b, t, h)
    gcb = jnp.flip(jnp.cumsum(jnp.flip(gch[:, :, :, 1], axis=2), axis=2), axis=2).reshape(b, t, h)
    cols = jnp.stack([beta[:, :, 0], beta[:, :, 1], gcf, gcb], axis=2).reshape(b, t, 4, kh, rep)
    cols = jnp.transpose(cols, (0, 3, 1, 2, 4)).reshape(b, kh, t, 4 * rep)
    rows = jnp.stack([gcf, gcb], axis=2).reshape(b, nc, c, 2, kh, rep)
    rows = jnp.transpose(rows, (0, 4, 1, 3, 5, 2)).reshape(b, kh, nc, N_PROB, c)
    stspec = pl.BlockSpec((1, rep, DN_K_DIM, DN_V_DIM), lambda bb, hh: (bb, hh, 0, 0))
    unroll_a = math.gcd(nc, 4)
    vw = rep * DN_V_DIM
    k_blk0 = DN_QK_WIDTH // DN_K_DIM
    v_blk0 = 2 * DN_QK_WIDTH // vw
    z_blk0 = DN_CONV_CH // vw
    in_specs = [pl.BlockSpec((t, DN_K_DIM), lambda bb, hh: (rb0 + bb, hh)),
                pl.BlockSpec((t, DN_K_DIM), lambda bb, hh: (rb0 + bb, k_blk0 + hh)),
                pl.BlockSpec((t, vw), lambda bb, hh: (rb0 + bb, v_blk0 + hh)),
                pl.BlockSpec((t, vw), lambda bb, hh: (rb0 + bb, z_blk0 + hh)),
                pl.BlockSpec((1, DN_V_DIM), lambda bb, hh: (0, 0)),
                pl.BlockSpec((1, 1, t, 4 * rep), lambda bb, hh: (bb, hh, 0, 0)),
                pl.BlockSpec((1, 1, nc, N_PROB, c), lambda bb, hh: (bb, hh, 0, 0, 0)),
                stspec, stspec]
    args = [qkv, qkv, qkv, p, gain.reshape(1, DN_V_DIM), cols, rows, s_f0, s_b0]
    aliases = {}
    if prev_out is not None:
        in_specs.append(pl.BlockSpec(memory_space=pl.ANY))
        args.append(prev_out)
        aliases = {len(args) - 1: 0}
    on, sf, sb = pl.pallas_call(
        functools.partial(_delta_kernel, n_chunks=nc, unroll_a=unroll_a, has_prev=prev_out is not None),
        grid=(b, kh),
        in_specs=in_specs,
        out_specs=[pl.BlockSpec((t, vw), lambda bb, hh: (rb0 + bb, hh)), stspec, stspec],
        out_shape=[jax.ShapeDtypeStruct((tt, DN_V_WIDTH), BF16),
                   jax.ShapeDtypeStruct((b, h, DN_K_DIM, DN_V_DIM), F32),
                   jax.ShapeDtypeStruct((b, h, DN_K_DIM, DN_V_DIM), F32)],
        scratch_shapes=[pltpu.VMEM((N_PROB, t, DN_V_DIM), F32),
                        pltpu.VMEM((N_PROB, nc, 2 * c, DN_K_DIM), BF16),
                        pltpu.VMEM((N_PROB, nc, c + DN_K_DIM, c), BF16),
                        pltpu.VMEM((N_PROB, nc, 8, DN_V_DIM), F32),
                        pltpu.VMEM((t, vw), F32)],
        input_output_aliases=aliases,
        compiler_params=_cparams(("parallel", "parallel")),
        name="delta_rule",
    )(*args)
    return on, sf, sb


def _dn_prep_kernel(*refs, has_prev):
    p_ref, w_ref = refs[:2]
    o_ref = refs[2 + int(has_prev)]
    j = pl.program_id(1)
    x = p_ref[...]
    s, cb = x.shape
    t = lax.broadcasted_iota(jnp.int32, (s, cb), 0)
    w = w_ref[...]
    y = (w[0:1] * jnp.where(t >= 2, pltpu.roll(x, 2, 0), 0.0) + w[1:2] * jnp.where(t >= 1, pltpu.roll(x, 1, 0), 0.0)
         + w[2:3] * x + w[3:4] * jnp.where(t < s - 1, pltpu.roll(x, s - 1, 0), 0.0))
    y = y * jax.nn.sigmoid(y)
    qk_blocks = DN_QK_WIDTH // cb
    for hh in range(cb // DN_K_DIM):
        seg = y[:, hh * DN_K_DIM:(hh + 1) * DN_K_DIM]
        inv = lax.rsqrt(jnp.sum(seg * seg, axis=-1, keepdims=True) + EPS)
        mult = jnp.where(j < qk_blocks, inv * (DN_K_DIM ** -0.5), jnp.where(j < 2 * qk_blocks, inv, 1.0))
        o_ref[:, hh * DN_K_DIM:(hh + 1) * DN_K_DIM] = (seg * mult).astype(o_ref.dtype)


def dn_prep(p, conv_w, *, n_seq, seq_len, row0, prev_out=None, cb=512):
    tt = p.shape[0]
    assert row0 % seq_len == 0 and DN_QK_WIDTH % cb == 0
    rb0 = row0 // seq_len
    in_specs = [pl.BlockSpec((seq_len, cb), lambda b, j: (rb0 + b, j)),
                pl.BlockSpec((conv_w.shape[0], cb), lambda b, j: (0, j))]
    args = [p, conv_w]
    aliases = {}
    if prev_out is not None:
        in_specs.append(pl.BlockSpec(memory_space=pl.ANY))
        args.append(prev_out)
        aliases = {2: 0}
    return pl.pallas_call(
        functools.partial(_dn_prep_kernel, has_prev=prev_out is not None),
        grid=(n_seq, DN_CONV_CH // cb),
        in_specs=in_specs,
        out_specs=pl.BlockSpec((seq_len, cb), lambda b, j: (rb0 + b, j)),
        out_shape=jax.ShapeDtypeStruct((tt, DN_CONV_CH), BF16),
        input_output_aliases=aliases,
        compiler_params=_cparams(("parallel", "arbitrary")),
        name="dn_prep",
    )(*args)


def _peer_route_kernel(q_ref, keys_ref, nc_ref, re_ref, v1_ref, v2_ref):
    kk = PEER_TOPK
    neg = -jnp.inf
    for h in range(PEER_HEADS):
        r0 = h * 2 * PEER_HALF
        s1 = _bdot_nt(keys_ref[0], q_ref[:, r0:r0 + PEER_HALF])
        s2 = _bdot_nt(keys_ref[1], q_ref[:, r0 + PEER_HALF:r0 + 2 * PEER_HALF])
        sc = s1
        for r in range(kk):
            m = jnp.max(sc, axis=0, keepdims=True)
            v1_ref[r:r + 1, :] = m
            sc = jnp.where(sc == m, neg, sc)
        sc = s2
        rank2 = jnp.full(s2.shape, float(kk), F32)
        for r in range(kk):
            m = jnp.max(sc, axis=0, keepdims=True)
            v2_ref[r:r + 1, :] = m
            hit = sc == m
            rank2 = jnp.where(hit, float(r), rank2)
            sc = jnp.where(hit, neg, sc)
        v2 = v2_ref[...]
        cand = jnp.concatenate([v1_ref[0:1, :] + v2] + [v1_ref[a:a + 1, :] + v2[0:8] for a in range(1, kk)],
                               axis=0)
        top = cand[0:1, :]
        work = cand
        tau = top
        for r in range(kk):
            tau = jnp.max(work, axis=0, keepdims=True)
            work = jnp.where(work == tau, neg, work)
        sel = cand >= tau
        z = jnp.sum(jnp.where(sel, jnp.exp(cand - top), 0.0), axis=0, keepdims=True)
        one = jnp.where(sel, 1.0, 0.0)
        n1 = jnp.zeros(s1.shape, F32)
        for a in range(kk):
            lo, hi = (0, kk) if a == 0 else (kk + 8 * (a - 1), kk + 8 * a)
            cnt = jnp.sum(one[lo:hi], axis=0, keepdims=True)
            n1 = jnp.where(s1 == v1_ref[a:a + 1, :], cnt, n1)
        nc_ref[0, h] = n1
        nc_ref[1, h] = jnp.exp(s1 - v1_ref[0:1, :]) / z
        re_ref[0, h] = rank2.astype(BF16)
        re_ref[1, h] = jnp.exp(s2 - v2[0:1, :]).astype(BF16)


def peer_route(q, sub_keys, *, tb):
    t, d2 = q.shape
    tb = min(tb, t)
    tspec = pl.BlockSpec((2, PEER_HEADS, PEER_N_KEYS, tb), lambda j: (0, 0, 0, j))
    return pl.pallas_call(
        _peer_route_kernel,
        grid=(t // tb,),
        in_specs=[pl.BlockSpec((tb, d2), lambda j: (j, 0)),
                  pl.BlockSpec((2, PEER_N_KEYS, PEER_HALF), lambda j: (0, 0, 0))],
        out_specs=[tspec, tspec],
        out_shape=[jax.ShapeDtypeStruct((2, PEER_HEADS, PEER_N_KEYS, t), F32),
                   jax.ShapeDtypeStruct((2, PEER_HEADS, PEER_N_KEYS, t), BF16)],
        scratch_shapes=[pltpu.VMEM((PEER_TOPK, tb), F32), pltpu.VMEM((PEER_TOPK, tb), F32)],
        compiler_params=_cparams(("parallel",)),
        name="peer_route",
    )(q, sub_keys)


def _cast_kernel(x_ref, o_ref):
    o_ref[...] = x_ref[0].astype(o_ref.dtype)


def cast_layer_bf16(w, layer, *, tr=1024):
    _, r, c = w.shape
    return pl.pallas_call(
        _cast_kernel,
        grid=(r // tr,),
        in_specs=[pl.BlockSpec((1, tr, c), lambda i: (layer, i, 0))],
        out_specs=pl.BlockSpec((tr, c), lambda i: (i, 0)),
        out_shape=jax.ShapeDtypeStruct((r, c), BF16),
        compiler_params=_cparams(("parallel",)),
        name="cast_bf16",
    )(w)


def _transpose_cast_kernel(x_ref, o_ref):
    o_ref[...] = jnp.transpose(x_ref[0]).astype(o_ref.dtype)


def transpose_layer_bf16(w, layer, *, tr=512):
    _, r, c = w.shape
    return pl.pallas_call(
        _transpose_cast_kernel,
        grid=(r // tr,),
        in_specs=[pl.BlockSpec((1, tr, c), lambda i: (layer, i, 0))],
        out_specs=pl.BlockSpec((c, tr), lambda i: (0, i)),
        out_shape=jax.ShapeDtypeStruct((c, r), BF16),
        compiler_params=_cparams(("parallel",)),
        name="transpose_bf16",
    )(w)


def _peer_dense_kernel(u_ref, wd_ref, wut_ref, nc_ref, re_ref, x_ref, g_ref, o_ref, acc_s, act_s, *,
                       rows_per_block, n_blocks):
    i = pl.program_id(1)

    @pl.when(i == 0)
    def _():
        acc_s[...] = jnp.zeros_like(acc_s)
        act_s[...] = jnp.zeros_like(act_s)

    blk = jnp.minimum(i, n_blocks - 1)
    tb = u_ref.shape[0]
    cw = min(PEER_COL_CHUNK, tb)
    e1s = pl.ds(pl.multiple_of(blk * rows_per_block, rows_per_block), rows_per_block)
    n1_all = [nc_ref[0, h, e1s, :] for h in range(PEER_HEADS)]
    c1_all = [nc_ref[1, h, e1s, :] for h in range(PEER_HEADS)]
    for c0 in range(0, tb, cw):
        cs = slice(c0, c0 + cw)
        hpre = _bdot_nt(wd_ref[...], u_ref[cs, :])
        acc_s[:, cs] += jnp.dot(wut_ref[...], act_s[:, cs], preferred_element_type=F32)
        gates = []
        for r in range(rows_per_block):
            acc = None
            for h in range(PEER_HEADS):
                n1 = n1_all[h][r:r + 1, cs].astype(BF16)
                c1 = c1_all[h][r:r + 1, cs].astype(BF16)
                wgt = jnp.where(re_ref[0, h, :, cs] < n1, re_ref[1, h, :, cs] * c1, jnp.zeros((), BF16))
                acc = wgt if acc is None else acc + wgt
            gates.append(acc)
        gate = jnp.concatenate(gates, axis=0).astype(F32)
        act = 0.5 * hpre * (1.0 + lax.erf(hpre * (2.0 ** -0.5))) * gate
        act_s[:, cs] = act.astype(BF16)

    @pl.when(i == n_blocks)
    def _():
        o_ref[...] = x_ref[...] + g_ref[0] * jnp.transpose(acc_s[...])


def peer_dense(u, w_down, w_up_t, nc, re, x, gates, group_of_tile, *, tb, eb):
    t, d = u.shape
    ne = w_down.shape[0]
    tb = min(tb, t)
    assert eb % PEER_N_KEYS == 0 and t % tb == 0
    nblk = ne // eb
    once = pl.Buffered(1)
    tspec = pl.BlockSpec((2, PEER_HEADS, PEER_N_KEYS, tb), lambda j, i: (0, 0, 0, j), pipeline_mode=once)
    return pl.pallas_call(
        functools.partial(_peer_dense_kernel, rows_per_block=eb // PEER_N_KEYS, n_blocks=nblk),
        grid=(t // tb, nblk + 1),
        in_specs=[pl.BlockSpec((tb, d), lambda j, i: (j, 0), pipeline_mode=once),
                  pl.BlockSpec((eb, d), lambda j, i: (jnp.minimum(i, nblk - 1), 0)),
                  pl.BlockSpec((d, eb), lambda j, i: (0, jnp.maximum(i - 1, 0))),
                  tspec, tspec,
                  pl.BlockSpec((tb, d), lambda j, i: (j, 0), pipeline_mode=once),
                  pl.BlockSpec((1, 1, d), lambda j, i: (group_of_tile(j, tb), 0, 0))],
        out_specs=pl.BlockSpec((tb, d), lambda j, i: (j, 0)),
        out_shape=jax.ShapeDtypeStruct((t, d), F32),
        scratch_shapes=[pltpu.VMEM((d, tb), F32), pltpu.VMEM((eb, tb), BF16)],
        compiler_params=_cparams(("parallel", "arbitrary")),
        name="peer_dense",
    )(u, w_down, w_up_t, nc, re, x, gates)


def peer_residual(x, u, layer, w_q, sub_keys, w_down, w_up, gates, group_of_tile, *, tb=512, eb=1024):
    q = proj(u, w_q, layer=layer, tm=1024, tn=512, name="peer_q")
    nc, re = peer_route(q, sub_keys[layer], tb=256)
    return peer_dense(u, cast_layer_bf16(w_down, layer), transpose_layer_bf16(w_up, layer), nc, re,
                      x, gates, group_of_tile, tb=tb, eb=eb)


def _rms(x, gain):
    return x * lax.rsqrt(jnp.mean(x * x, axis=-1, keepdims=True) + EPS) * gain


def _grid_angles(n_tokens):
    n_rows = n_tokens // GRID_W
    rows = jnp.repeat(jnp.arange(n_rows), GRID_W).astype(F32)
    cols = jnp.tile(jnp.arange(GRID_W), n_rows).astype(F32)
    n_freq = ROPE_AXIS_DIM // 2
    inv = ROPE_THETA ** (-jnp.arange(n_freq, dtype=F32) / n_freq)
    return rows[:, None] * inv, cols[:, None] * inv


def _rotate(x, ang):
    half = x.shape[-1] // 2
    x1, x2 = x[..., :half], x[..., half:]
    cos = jnp.cos(ang)[None, :, None, :]
    sin = jnp.sin(ang)[None, :, None, :]
    return jnp.concatenate([x1 * cos - x2 * sin, x1 * sin + x2 * cos], axis=-1)


def _axial_rope(x, ang_row, ang_col):
    return jnp.concatenate([_rotate(x[..., :ROPE_AXIS_DIM], ang_row),
                            _rotate(x[..., ROPE_AXIS_DIM:], ang_col)], axis=-1)


def _short_conv(x, w):
    y = lax.conv_general_dilated(x, w[:, None, :], window_strides=(1,), padding=[CONV_PAD],
                                 dimension_numbers=('NWC', 'WIO', 'NWC'), feature_group_count=x.shape[-1])
    return jax.nn.silu(y)


def _l2n(x):
    return x * lax.rsqrt(jnp.sum(x * x, axis=-1, keepdims=True) + EPS)


def kernel(x_prompt, x_sample, cache_k, cache_v, state_fwd, state_bwd, c, c_ctx, ada_w, ada_b, norm1, norm2,
           af_w_in, af_q_norm, af_k_norm, af_w_out, dn_w_in, dn_conv_w, dn_a_log, dn_dt_bias, dn_o_norm,
           dn_w_out, peer_w_q, peer_sub_keys, peer_w_down, peer_w_up, final_norm):
    nb, seq, d = x_prompt.shape
    db, dseq, _ = x_sample.shape
    depth = ada_w.shape[0]
    tp = nb * seq
    ts = db * dseq
    tt = tp + ts
    tm = math.gcd(math.gcd(tp, dseq), 1024)

    def group_of_tile(i, tile):
        r = i * tile
        return jnp.where(r < tp, 0, 1 + (r - tp) // dseq)

    conds = jnp.concatenate([c_ctx[None], c, jnp.zeros((8 - 1 - db, d), F32)], axis=0)
    mods = ada_all(conds, ada_w, ada_b)
    mods = mods.reshape(depth, 8, 6, d)

    x = jnp.concatenate([x_prompt.reshape(tp, d), x_sample.reshape(ts, d)], axis=0)
    new_k, new_v, new_sf, new_sb = [], [], [], []
    for i in range(depth):
        j = i // 2
        md = [mods[i, :, n][:, None, :] for n in range(6)]
        sh1, sc1, g1, sh2, sc2, g2 = md
        u = modulate(x, norm1[i], sh1, sc1, group_of_tile, tm=tm)
        if i % 2 == 0:
            p = proj(u, af_w_in, layer=j, tm=tm, tn=512, name="af_in")
            q = _rms(p[:, FOURIER_WIDTH:FOURIER_WIDTH + ATTN_WIDTH].reshape(tt, N_HEADS, HEAD_DIM), af_q_norm[j])
            kv0 = FOURIER_WIDTH + ATTN_WIDTH
            k = _rms(p[:, kv0:kv0 + KV_WIDTH].reshape(tt, KV_HEADS, HEAD_DIM), af_k_norm[j])
            v = p[:, kv0 + KV_WIDTH:].reshape(tt, KV_HEADS, HEAD_DIM)
            new_k.append(k[:tp].reshape(nb, seq, KV_HEADS, HEAD_DIM))
            new_v.append(v[:tp].reshape(nb, seq, KV_HEADS, HEAD_DIM))
            ang_row, ang_col = _grid_angles(dseq)
            qs = _axial_rope(q[tp:].reshape(db, dseq, N_HEADS, HEAD_DIM), ang_row, ang_col)
            ks = _axial_rope(k[tp:].reshape(db, dseq, KV_HEADS, HEAD_DIM), ang_row, ang_col)
            keys = jnp.concatenate([ks, cache_k[:, j]], axis=1).reshape(db, -1, KV_WIDTH)
            vals = jnp.concatenate([v[tp:].reshape(db, dseq, KV_HEADS, HEAD_DIM), cache_v[:, j]],
                                   axis=1).reshape(db, -1, KV_WIDTH)
            att = attention(q[:tp].reshape(nb, seq, ATTN_WIDTH), k[:tp].reshape(nb, seq, KV_WIDTH),
                            v[:tp].reshape(nb, seq, KV_WIDTH), total_rows=tt, row0=0)
            att = attention(qs.reshape(db, dseq, ATTN_WIDTH), keys, vals, total_rows=tt, row0=tp, prev_out=att)
            r = chan_dft(p, tm=tm)
            fou = seq_dft(r, n_seq=nb, seq_len=seq, row0=0)
            fou = seq_dft(r, n_seq=db, seq_len=dseq, row0=tp, prev_out=fou)
            x = proj_residual([fou, att], af_w_out, x, g1, group_of_tile, layer=j, tm=tm, tn=512, name="af_out")
        else:
            p = proj(u, dn_w_in, layer=j, tm=tm, tn=512, ncols=DN_CONV_CH + DN_V_WIDTH, name="dn_in")
            ba = proj(u, dn_w_in, layer=j, tm=tm, tn=128, col0=DN_CONV_CH + DN_V_WIDTH, name="dn_in_ba")
            beta_all = jax.nn.sigmoid(ba.reshape(tt, 2, 2, DN_V_HEADS)[:, 0])
            g_all = -jnp.exp(dn_a_log[j]) * jax.nn.softplus(ba.reshape(tt, 2, 2, DN_V_HEADS)[:, 1] + dn_dt_bias[j])
            qkv = dn_prep(p, dn_conv_w[j], n_seq=nb, seq_len=seq, row0=0)
            qkv = dn_prep(p, dn_conv_w[j], n_seq=db, seq_len=dseq, row0=tp, prev_out=qkv)
            zeros = jnp.zeros((nb, DN_V_HEADS, DN_K_DIM, DN_V_DIM), F32)
            on, s_f, s_b = delta_mixer(qkv, p, beta_all[:tp].reshape(nb, seq, 2, DN_V_HEADS),
                                       g_all[:tp].reshape(nb, seq, 2, DN_V_HEADS), dn_o_norm[j], zeros, zeros,
                                       n_seq=nb, seq_len=seq, row0=0)
            new_sf.append(s_f)
            new_sb.append(s_b)
            on, _, _ = delta_mixer(qkv, p, beta_all[tp:].reshape(db, dseq, 2, DN_V_HEADS),
                                   g_all[tp:].reshape(db, dseq, 2, DN_V_HEADS), dn_o_norm[j],
                                   state_fwd[:, j], state_bwd[:, j], n_seq=db, seq_len=dseq, row0=tp, prev_out=on)
            x = proj_residual([on], dn_w_out, x, g1, group_of_tile, layer=j, tm=tm, tn=256, name="dn_out")
        u2 = modulate(x, norm2[i], sh2, sc2, group_of_tile, tm=tm)
        x = peer_residual(x, u2, i, peer_w_q, peer_sub_keys, peer_w_down, peer_w_up, g2, group_of_tile)
    return (rmsnorm_rows(x, final_norm, row0=0, nrows=tp, tm=tm).reshape(nb, seq, d),
            rmsnorm_rows(x, final_norm, row0=tp, nrows=ts, tm=tm).reshape(db, dseq, d),
            jnp.stack(new_k, axis=1), jnp.stack(new_v, axis=1),
            jnp.stack(new_sf, axis=1), jnp.stack(new_sb, axis=1))
```

```python
import functools
import math

import jax
import jax.numpy as jnp
import numpy as np
from jax import lax
from jax.experimental import pallas as pl
from jax.experimental.pallas import tpu as pltpu

F32 = jnp.float32
BF16 = jnp.bfloat16

EPS = 1e-6
HEAD_DIM = 128
N_HEADS = 8
KV_HEADS = 2
GQA_GROUP = N_HEADS // KV_HEADS
FOURIER_GROUPS = 4
FOURIER_GROUP_DIM = 256
FOURIER_WIDTH = FOURIER_GROUPS * FOURIER_GROUP_DIM
ATTN_WIDTH = N_HEADS * HEAD_DIM
KV_WIDTH = KV_HEADS * HEAD_DIM
GRID_W = 64
ROPE_THETA = 10000.0
ROPE_AXIS_DIM = HEAD_DIM // 2

DN_QK_HEADS = 16
DN_V_HEADS = 32
DN_K_DIM = 128
DN_V_DIM = 128
DN_QK_WIDTH = DN_QK_HEADS * DN_K_DIM
DN_V_WIDTH = DN_V_HEADS * DN_V_DIM
DN_CONV_CH = 2 * DN_QK_WIDTH + DN_V_WIDTH
DN_CHUNK = 64

PEER_HEADS = 8
PEER_N_KEYS = 128
PEER_HALF = 128
PEER_TOPK = 16
PEER_COL_CHUNK = 256
PACK = 16

VMEM_LIMIT = 56 * 1024 * 1024


def _cparams(sem):
    return pltpu.CompilerParams(dimension_semantics=sem, vmem_limit_bytes=VMEM_LIMIT)


def _bdot(a, b):
    return jnp.dot(a.astype(BF16), b.astype(BF16), preferred_element_type=F32)


def _bdot_nt(a, b):
    return lax.dot_general(a.astype(BF16), b.astype(BF16), (((1,), (1,)), ((), ())),
                           preferred_element_type=F32)


def _bmm(a, b):
    return lax.dot_general(a.astype(BF16), b.astype(BF16), (((2,), (1,)), ((0,), (0,))),
                           preferred_element_type=F32)


def _bmm_nt(a, b):
    return lax.dot_general(a.astype(BF16), b.astype(BF16), (((2,), (2,)), ((0,), (0,))),
                           preferred_element_type=F32)


def _proj_kernel(x_ref, w_ref, o_ref):
    o_ref[...] = _bdot(x_ref[...], w_ref[0]).astype(o_ref.dtype)


def proj(x, w, *, layer=0, tm, tn, col0=0, ncols=None, out_dtype=F32, name="proj"):
    m, k = x.shape
    ncols = w.shape[2] - col0 if ncols is None else ncols
    tm = min(tm, m)
    tn = min(tn, ncols)
    assert m % tm == 0 and ncols % tn == 0 and col0 % tn == 0
    cb = col0 // tn
    return pl.pallas_call(
        _proj_kernel,
        grid=(m // tm, ncols // tn),
        in_specs=[pl.BlockSpec((tm, k), lambda i, j: (i, 0)),
                  pl.BlockSpec((1, k, tn), lambda i, j: (layer, 0, j + cb))],
        out_specs=pl.BlockSpec((tm, tn), lambda i, j: (i, j)),
        out_shape=jax.ShapeDtypeStruct((m, ncols), out_dtype),
        compiler_params=_cparams(("parallel", "arbitrary")),
        name=name,
    )(x, w)


def _proj_res_kernel(*refs, n_parts):
    x_refs, w_refs = refs[:n_parts], refs[n_parts:2 * n_parts]
    res_ref, gate_ref, o_ref = refs[2 * n_parts:]
    acc = _bdot(x_refs[0][...], w_refs[0][0])
    for x_ref, w_ref in zip(x_refs[1:], w_refs[1:]):
        acc = acc + _bdot(x_ref[...], w_ref[0])
    o_ref[...] = res_ref[...] + gate_ref[0] * acc


def proj_residual(xs, w, res, gates, group_of_tile, *, layer=0, tm, tn, name="proj_res"):
    m = xs[0].shape[0]
    n = w.shape[2]
    tm = min(tm, m)
    tn = min(tn, n)
    kp = xs[0].shape[1]
    assert m % tm == 0 and n % tn == 0 and all(x.shape == (m, kp) for x in xs)
    x_specs = [pl.BlockSpec((tm, kp), lambda i, j: (i, 0)) for _ in xs]
    w_specs = [pl.BlockSpec((1, kp, tn), functools.partial(lambda i, j, part: (layer, part, j), part=part))
               for part in range(len(xs))]
    return pl.pallas_call(
        functools.partial(_proj_res_kernel, n_parts=len(xs)),
        grid=(m // tm, n // tn),
        in_specs=x_specs + w_specs + [pl.BlockSpec((tm, tn), lambda i, j: (i, j)),
                                      pl.BlockSpec((1, 1, tn), lambda i, j: (group_of_tile(i, tm), 0, j))],
        out_specs=pl.BlockSpec((tm, tn), lambda i, j: (i, j)),
        out_shape=jax.ShapeDtypeStruct((m, n), F32),
        compiler_params=_cparams(("parallel", "arbitrary")),
        name=name,
    )(*xs, *([w] * len(xs)), res, gates)


def _modulate_kernel(x_ref, gain_ref, shift_ref, scale_ref, o_ref):
    x = x_ref[...]
    y = x * lax.rsqrt(jnp.mean(x * x, axis=-1, keepdims=True) + EPS)
    o_ref[...] = (y * gain_ref[...] * (1.0 + scale_ref[0]) + shift_ref[0]).astype(o_ref.dtype)


def modulate(x, gain, shift, scale, group_of_tile, *, tm, out_dtype=BF16):
    m, d = x.shape
    tm = min(tm, m)
    gspec = pl.BlockSpec((1, 1, d), lambda i: (group_of_tile(i, tm), 0, 0))
    return pl.pallas_call(
        _modulate_kernel,
        grid=(m // tm,),
        in_specs=[pl.BlockSpec((tm, d), lambda i: (i, 0)),
                  pl.BlockSpec((1, d), lambda i: (0, 0)), gspec, gspec],
        out_specs=pl.BlockSpec((tm, d), lambda i: (i, 0)),
        out_shape=jax.ShapeDtypeStruct((m, d), out_dtype),
        compiler_params=_cparams(("parallel",)),
        name="modulate",
    )(x, gain.reshape(1, d), shift, scale)


def _rmsnorm_kernel(x_ref, gain_ref, o_ref):
    x = x_ref[...]
    o_ref[...] = x * lax.rsqrt(jnp.mean(x * x, axis=-1, keepdims=True) + EPS) * gain_ref[...]


def rmsnorm_rows(x, gain, *, row0, nrows, tm):
    d = x.shape[1]
    tm = min(tm, nrows)
    assert row0 % tm == 0 and nrows % tm == 0
    rb0 = row0 // tm
    return pl.pallas_call(
        _rmsnorm_kernel,
        grid=(nrows // tm,),
        in_specs=[pl.BlockSpec((tm, d), lambda i: (rb0 + i, 0)), pl.BlockSpec((1, d), lambda i: (0, 0))],
        out_specs=pl.BlockSpec((tm, d), lambda i: (i, 0)),
        out_shape=jax.ShapeDtypeStruct((nrows, d), F32),
        compiler_params=_cparams(("parallel",)),
        name="final_norm",
    )(x, gain.reshape(1, d))


def _ada_kernel(c_ref, w_ref, b_ref, o_ref):
    c = c_ref[...]
    o_ref[0] = _bdot(c * jax.nn.sigmoid(c), w_ref[0]) + b_ref[0]


def ada_all(conds, ada_w, ada_b, *, tn=1024):
    depth, d, n = ada_w.shape
    r = conds.shape[0]
    return pl.pallas_call(
        _ada_kernel,
        grid=(depth, n // tn),
        in_specs=[pl.BlockSpec((r, d), lambda l, j: (0, 0)),
                  pl.BlockSpec((1, d, tn), lambda l, j: (l, 0, j)),
                  pl.BlockSpec((1, 1, tn), lambda l, j: (l, 0, j))],
        out_specs=pl.BlockSpec((1, r, tn), lambda l, j: (l, 0, j)),
        out_shape=jax.ShapeDtypeStruct((depth, r, n), F32),
        compiler_params=_cparams(("parallel", "arbitrary")),
        name="ada",
    )(conds, ada_w, ada_b.reshape(depth, 1, n))


def _dft_tables(n, scale):
    idx = np.arange(n, dtype=np.int64)
    ang = 2.0 * np.pi * ((idx[:, None] * idx[None, :]) % n).astype(np.float64) / n
    return np.stack([np.cos(ang) * scale, np.sin(ang) * scale]).astype(np.float32)


def _chan_dft_kernel(x_ref, t_ref, o_ref):
    o_ref[0] = _bdot(x_ref[...], t_ref[0]).astype(o_ref.dtype)


def _seq_dft_kernel(t_ref, r_ref, *rest):
    o_ref = rest[-1]
    o_ref[...] = (_bdot(t_ref[0], r_ref[0]) - _bdot(t_ref[1], r_ref[1])).astype(o_ref.dtype)


def chan_dft(p, *, tm):
    t = p.shape[0]
    gd = FOURIER_GROUP_DIM
    tm = min(tm, t)
    ctab = jnp.asarray(_dft_tables(gd, gd ** -0.5), BF16)
    return pl.pallas_call(
        _chan_dft_kernel,
        grid=(t // tm, FOURIER_GROUPS, 2),
        in_specs=[pl.BlockSpec((tm, gd), lambda i, g, s: (i, g)),
                  pl.BlockSpec((1, gd, gd), lambda i, g, s: (s, 0, 0))],
        out_specs=pl.BlockSpec((1, tm, gd), lambda i, g, s: (s, i, g)),
        out_shape=jax.ShapeDtypeStruct((2, t, FOURIER_WIDTH), BF16),
        compiler_params=_cparams(("parallel", "arbitrary", "arbitrary")),
        name="chan_dft",
    )(p, ctab)


def seq_dft(r, *, n_seq, seq_len, row0, prev_out=None, out_dtype=BF16):
    t = r.shape[1]
    assert row0 % seq_len == 0
    stab = jnp.asarray(_dft_tables(seq_len, seq_len ** -0.5), BF16)
    ts = min(512, seq_len)
    tn = 512
    nrow = seq_len // ts
    sb0 = row0 // seq_len
    rb0 = row0 // ts
    in_specs = [pl.BlockSpec((2, ts, seq_len), lambda b, i, j: (0, i, 0)),
                pl.BlockSpec((2, seq_len, tn), lambda b, i, j: (0, sb0 + b, j))]
    args = [stab, r]
    aliases = {}
    if prev_out is not None:
        in_specs.append(pl.BlockSpec(memory_space=pl.ANY))
        args.append(prev_out)
        aliases = {2: 0}
    return pl.pallas_call(
        _seq_dft_kernel,
        grid=(n_seq, nrow, FOURIER_WIDTH // tn),
        in_specs=in_specs,
        out_specs=pl.BlockSpec((ts, tn), lambda b, i, j: (rb0 + b * nrow + i, j)),
        out_shape=jax.ShapeDtypeStruct((t, FOURIER_WIDTH), out_dtype),
        input_output_aliases=aliases,
        compiler_params=_cparams(("parallel", "arbitrary", "arbitrary")),
        name="seq_dft",
    )(*args)


def _attn_kernel(q_ref, k_ref, v_ref, *rest):
    o_ref = rest[-1]
    k = k_ref[0].astype(BF16)
    v = v_ref[0].astype(BF16)
    scale = HEAD_DIM ** -0.5
    for g in range(GQA_GROUP):
        q = q_ref[0, :, g * HEAD_DIM:(g + 1) * HEAD_DIM]
        s = _bdot_nt(q, k) * scale
        s = s - jnp.max(s, axis=-1, keepdims=True)
        e = jnp.exp(s)
        p = e / jnp.sum(e, axis=-1, keepdims=True)
        o_ref[:, g * HEAD_DIM:(g + 1) * HEAD_DIM] = _bdot(p, v).astype(o_ref.dtype)


def attention(q, k, v, *, total_rows, row0, prev_out=None, tq=256, out_dtype=BF16):
    b, sq, _ = q.shape
    sk = k.shape[1]
    tq = min(tq, sq)
    assert row0 % tq == 0
    gw = GQA_GROUP * HEAD_DIM
    nq = sq // tq
    rb0 = row0 // tq
    in_specs = [pl.BlockSpec((1, tq, gw), lambda bb, h, i: (bb, i, h)),
                pl.BlockSpec((1, sk, HEAD_DIM), lambda bb, h, i: (bb, 0, h)),
                pl.BlockSpec((1, sk, HEAD_DIM), lambda bb, h, i: (bb, 0, h))]
    args = [q, k, v]
    aliases = {}
    if prev_out is not None:
        in_specs.append(pl.BlockSpec(memory_space=pl.ANY))
        args.append(prev_out)
        aliases = {3: 0}
    return pl.pallas_call(
        _attn_kernel,
        grid=(b, KV_HEADS, nq),
        in_specs=in_specs,
        out_specs=pl.BlockSpec((tq, gw), lambda bb, h, i: (rb0 + bb * nq + i, h)),
        out_shape=jax.ShapeDtypeStruct((total_rows, ATTN_WIDTH), out_dtype),
        input_output_aliases=aliases,
        compiler_params=_cparams(("parallel", "parallel", "arbitrary")),
        name="attention",
    )(*args)


N_PROB = 4


def _tri_masks(c):
    ri = lax.broadcasted_iota(jnp.int32, (c, c), 0)
    ci = lax.broadcasted_iota(jnp.int32, (c, c), 1)
    return ri, ci


def _unit_tri_inverse(low, ri, ci):
    c = low.shape[-1]

    def same_block(size):
        sh = int(math.log2(size))
        return (ri >> sh) == (ci >> sh)

    leaf = 4
    dg = jnp.where(same_block(leaf), low, 0.0)
    imd = jnp.where(ri == ci, 1.0, 0.0) - dg
    inv = imd + _bmm(imd, _bmm(dg, dg))
    size = leaf
    while size < c:
        off = jnp.where(same_block(2 * size) & jnp.logical_not(same_block(size)), low, 0.0)
        inv = inv - _bmm(inv, _bmm(off, inv))
        size *= 2
    return inv


def _delta_kernel(*refs, n_chunks, unroll_a, has_prev):
    n_in = 9 + int(has_prev)
    q_ref, k_ref, v_ref, z_ref, gain_ref, cols_ref, rows_ref, sf0_ref, sb0_ref = refs[:9]
    on_ref, sf_ref, sb_ref, u_s, wq_s, akt_s, gl_s, o_s = refs[n_in:]
    c = DN_CHUNK
    dk = DN_K_DIM
    dv = DN_V_DIM
    ri, ci = _tri_masks(c)

    ca = unroll_a

    def body_a(g0, carry):
        rows = pl.ds(pl.multiple_of(g0 * (ca * c), ca * c), ca * c)
        chs = pl.ds(g0 * ca, ca)
        k = k_ref[rows, :].astype(F32).reshape(ca, c, dk)
        q = q_ref[rows, :].astype(F32).reshape(ca, c, dk)
        kk = _bmm_nt(k, k)
        qk = _bmm_nt(q, k)
        cols = cols_ref[0, 0, rows, :].reshape(ca, c, 2 * N_PROB)
        rws = rows_ref[0, 0, chs]
        lows, rhss, decays, egs, glasts, gcols = [], [], [], [], [], []
        for p in range(N_PROB):
            head, reverse = p % 2, p >= 2
            beta = cols[:, :, p:p + 1]
            gcol = cols[:, :, N_PROB + p:N_PROB + p + 1]
            grow = rws[:, p:p + 1, :]
            glast = grow[:, :, 0:1] if reverse else grow[:, :, c - 1:c]
            incl = (ri <= ci) if reverse else (ri >= ci)
            strict = (ri < ci) if reverse else (ri > ci)
            decay = jnp.where(incl, jnp.exp(jnp.where(incl, gcol - grow, 0.0)), 0.0)
            eg = jnp.exp(gcol)
            v = v_ref[rows, head * dv:(head + 1) * dv].astype(F32).reshape(ca, c, dv)
            lows.append(jnp.where(strict, kk * beta * decay, 0.0))
            rhss.append(jnp.concatenate([v * beta, k * (beta * eg)], axis=-1))
            decays.append(decay)
            egs.append(eg)
            glasts.append(glast)
            gcols.append(gcol)
        inv = _unit_tri_inverse(jnp.concatenate(lows, axis=0), ri, ci)
        sol = _bmm(inv, jnp.concatenate(rhss, axis=0))
        for p in range(N_PROB):
            sp = sol[p * ca:(p + 1) * ca]
            u_s[p, rows, :] = sp[:, :, :dv].reshape(ca * c, dv)
            wq_s[p, chs, 0:c, :] = sp[:, :, dv:].astype(BF16)
            wq_s[p, chs, c:2 * c, :] = (q * egs[p]).astype(BF16)
            akt_s[p, chs, 0:c, :] = (qk * decays[p]).astype(BF16)
            akt_s[p, chs, c:c + dk, :] = jnp.swapaxes(k * jnp.exp(glasts[p] - gcols[p]), 1, 2).astype(BF16)
            gl_s[p, chs] = jnp.broadcast_to(jnp.exp(glasts[p]), (ca, 8, dv))
        return carry

    lax.fori_loop(0, n_chunks // ca, body_a, 0)
    o_s[...] = jnp.zeros_like(o_s)

    def body_b(i, states):
        chs = [(n_chunks - 1 - i) if p >= 2 else i for p in range(N_PROB)]
        rows = [pl.ds(pl.multiple_of(ch * c, c), c) for ch in chs]
        ws = [_bdot(wq_s[p, chs[p]], states[p]) for p in range(N_PROB)]
        vn = [u_s[p, rows[p], :] - ws[p][0:c] for p in range(N_PROB)]
        av = [_bdot(akt_s[p, chs[p]], vn[p]) for p in range(N_PROB)]
        new_states = []
        for p in range(N_PROB):
            head = p % 2
            o_s[rows[p], head * dv:(head + 1) * dv] += ws[p][c:2 * c] + av[p][0:c]
            new_states.append(states[p] * gl_s[p, chs[p]][0:1, :] + av[p][c:c + dk])
        return tuple(new_states)

    init = tuple((sb0_ref if p >= 2 else sf0_ref)[0, p % 2] for p in range(N_PROB))
    fin = lax.fori_loop(0, n_chunks, body_b, init)
    for p in range(N_PROB):
        (sb_ref if p >= 2 else sf_ref)[0, p % 2] = fin[p]
    for head in range(N_PROB // 2):
        hs = slice(head * dv, (head + 1) * dv)
        o = o_s[:, hs]
        z = z_ref[:, hs]
        on = o * lax.rsqrt(jnp.mean(o * o, axis=-1, keepdims=True) + EPS) * gain_ref[...]
        on_ref[:, hs] = (on * (z * jax.nn.sigmoid(z))).astype(on_ref.dtype)


def delta_mixer(qkv, p, beta, g, gain, s_f0, s_b0, *, n_seq, seq_len, row0, prev_out=None):
    b, t = n_seq, seq_len
    tt = qkv.shape[0]
    h = DN_V_HEADS
    kh = DN_QK_HEADS
    rep = h // kh
    assert rep == 2 and row0 % t == 0
    rb0 = row0 // t
    c = DN_CHUNK
    nc = t // c
    gch = g.reshape(b, nc, c, 2, h)
    gcf = jnp.cumsum(gch[:, :, :, 0], axis=2).reshape(b, t, h)
    gcb = jnp.flip(jnp.cumsum(jnp.flip(gch[:, :, :, 1], axis=2), axis=2), axis=2).reshape(b, t, h)
    cols = jnp.stack([beta[:, :, 0], beta[:, :, 1], gcf, gcb], axis=2).reshape(b, t, 4, kh, rep)
    cols = jnp.transpose(cols, (0, 3, 1, 2, 4)).reshape(b, kh, t, 4 * rep)
    rows = jnp.stack([gcf, gcb], axis=2).reshape(b, nc, c, 2, kh, rep)
    rows = jnp.transpose(rows, (0, 4, 1, 3, 5, 2)).reshape(b, kh, nc, N_PROB, c)
    stspec = pl.BlockSpec((1, rep, DN_K_DIM, DN_V_DIM), lambda bb, hh: (bb, hh, 0, 0))
    unroll_a = math.gcd(nc, 4)
    vw = rep * DN_V_DIM
    k_blk0 = DN_QK_WIDTH // DN_K_DIM
    v_blk0 = 2 * DN_QK_WIDTH // vw
    z_blk0 = DN_CONV_CH // vw
    in_specs = [pl.BlockSpec((t, DN_K_DIM), lambda bb, hh: (rb0 + bb, hh)),
                pl.BlockSpec((t, DN_K_DIM), lambda bb, hh: (rb0 + bb, k_blk0 + hh)),
                pl.BlockSpec((t, vw), lambda bb, hh: (rb0 + bb, v_blk0 + hh)),
                pl.BlockSpec((t, vw), lambda bb, hh: (rb0 + bb, z_blk0 + hh)),
                pl.BlockSpec((1, DN_V_DIM), lambda bb, hh: (0, 0)),
                pl.BlockSpec((1, 1, t, 4 * rep), lambda bb, hh: (bb, hh, 0, 0)),
                pl.BlockSpec((1, 1, nc, N_PROB, c), lambda bb, hh: (bb, hh, 0, 0, 0)),
                stspec, stspec]
    args = [qkv, qkv, qkv, p, gain.reshape(1, DN_V_DIM), cols, rows, s_f0, s_b0]
    aliases = {}
    if prev_out is not None:
        in_specs.append(pl.BlockSpec(memory_space=pl.ANY))
        args.append(prev_out)
        aliases = {len(args) - 1: 0}
    on, sf, sb = pl.pallas_call(
        functools.partial(_delta_kernel, n_chunks=nc, unroll_a=unroll_a, has_prev=prev_out is not None),
        grid=(b, kh),
        in_specs=in_specs,
        out_specs=[pl.BlockSpec((t, vw), lambda bb, hh: (rb0 + bb, hh)), stspec, stspec],
        out_shape=[jax.ShapeDtypeStruct((tt, DN_V_WIDTH), BF16),
                   jax.ShapeDtypeStruct((b, h, DN_K_DIM, DN_V_DIM), F32),
                   jax.ShapeDtypeStruct((b, h, DN_K_DIM, DN_V_DIM), F32)],
        scratch_shapes=[pltpu.VMEM((N_PROB, t, DN_V_DIM), F32),
                        pltpu.VMEM((N_PROB, nc, 2 * c, DN_K_DIM), BF16),
                        pltpu.VMEM((N_PROB, nc, c + DN_K_DIM, c), BF16),
                        pltpu.VMEM((N_PROB, nc, 8, DN_V_DIM), F32),
                        pltpu.VMEM((t, vw), F32)],
        input_output_aliases=aliases,
        compiler_params=_cparams(("parallel", "parallel")),
        name="delta_rule",
    )(*args)
    return on, sf, sb


def _dn_prep_kernel(*refs, has_prev):
    p_ref, w_ref = refs[:2]
    o_ref = refs[2 + int(has_prev)]
    j = pl.program_id(1)
    x = p_ref[...]
    s, cb = x.shape
    t = lax.broadcasted_iota(jnp.int32, (s, cb), 0)
    w = w_ref[...]
    y = (w[0:1] * jnp.where(t >= 2, pltpu.roll(x, 2, 0), 0.0) + w[1:2] * jnp.where(t >= 1, pltpu.roll(x, 1, 0), 0.0)
         + w[2:3] * x + w[3:4] * jnp.where(t < s - 1, pltpu.roll(x, s - 1, 0), 0.0))
    y = y * jax.nn.sigmoid(y)
    qk_blocks = DN_QK_WIDTH // cb
    for hh in range(cb // DN_K_DIM):
        seg = y[:, hh * DN_K_DIM:(hh + 1) * DN_K_DIM]
        inv = lax.rsqrt(jnp.sum(seg * seg, axis=-1, keepdims=True) + EPS)
        mult = jnp.where(j < qk_blocks, inv * (DN_K_DIM ** -0.5), jnp.where(j < 2 * qk_blocks, inv, 1.0))
        o_ref[:, hh * DN_K_DIM:(hh + 1) * DN_K_DIM] = (seg * mult).astype(o_ref.dtype)


def dn_prep(p, conv_w, *, n_seq, seq_len, row0, prev_out=None, cb=512):
    tt = p.shape[0]
    assert row0 % seq_len == 0 and DN_QK_WIDTH % cb == 0
    rb0 = row0 // seq_len
    in_specs = [pl.BlockSpec((seq_len, cb), lambda b, j: (rb0 + b, j)),
                pl.BlockSpec((conv_w.shape[0], cb), lambda b, j: (0, j))]
    args = [p, conv_w]
    aliases = {}
    if prev_out is not None:
        in_specs.append(pl.BlockSpec(memory_space=pl.ANY))
        args.append(prev_out)
        aliases = {2: 0}
    return pl.pallas_call(
        functools.partial(_dn_prep_kernel, has_prev=prev_out is not None),
        grid=(n_seq, DN_CONV_CH // cb),
        in_specs=in_specs,
        out_specs=pl.BlockSpec((seq_len, cb), lambda b, j: (rb0 + b, j)),
        out_shape=jax.ShapeDtypeStruct((tt, DN_CONV_CH), BF16),
        input_output_aliases=aliases,
        compiler_params=_cparams(("parallel", "arbitrary")),
        name="dn_prep",
    )(*args)


def _peer_route_kernel(q_ref, keys_ref, nc_ref, re_ref, v1_ref, v2_ref):
    kk = PEER_TOPK
    neg = -jnp.inf
    for h in range(PEER_HEADS):
        r0 = h * 2 * PEER_HALF
        s1 = _bdot_nt(keys_ref[0], q_ref[:, r0:r0 + PEER_HALF])
        s2 = _bdot_nt(keys_ref[1], q_ref[:, r0 + PEER_HALF:r0 + 2 * PEER_HALF])
        sc = s1
        for r in range(kk):
            m = jnp.max(sc, axis=0, keepdims=True)
            v1_ref[r:r + 1, :] = m
            sc = jnp.where(sc == m, neg, sc)
        sc = s2
        rank2 = jnp.full(s2.shape, float(kk), F32)
        for r in range(kk):
            m = jnp.max(sc, axis=0, keepdims=True)
            v2_ref[r:r + 1, :] = m
            hit = sc == m
            rank2 = jnp.where(hit, float(r), rank2)
            sc = jnp.where(hit, neg, sc)
        v2 = v2_ref[...]
        cand = jnp.concatenate([v1_ref[0:1, :] + v2] + [v1_ref[a:a + 1, :] + v2[0:8] for a in range(1, kk)],
                               axis=0)
        top = cand[0:1, :]
        work = cand
        tau = top
        for r in range(kk):
            tau = jnp.max(work, axis=0, keepdims=True)
            work = jnp.where(work == tau, neg, work)
        sel = cand >= tau
        z = jnp.sum(jnp.where(sel, jnp.exp(cand - top), 0.0), axis=0, keepdims=True)
        one = jnp.where(sel, 1.0, 0.0)
        n1 = jnp.zeros(s1.shape, F32)
        for a in range(kk):
            lo, hi = (0, kk) if a == 0 else (kk + 8 * (a - 1), kk + 8 * a)
            cnt = jnp.sum(one[lo:hi], axis=0, keepdims=True)
            n1 = jnp.where(s1 == v1_ref[a:a + 1, :], cnt, n1)
        nc_ref[0, h] = n1
        nc_ref[1, h] = jnp.exp(s1 - v1_ref[0:1, :]) / z
        re_ref[0, h] = rank2.astype(BF16)
        re_ref[1, h] = jnp.exp(s2 - v2[0:1, :]).astype(BF16)


def peer_route(q, sub_keys, *, tb):
    t, d2 = q.shape
    tb = min(tb, t)
    tspec = pl.BlockSpec((2, PEER_HEADS, PEER_N_KEYS, tb), lambda j: (0, 0, 0, j))
    return pl.pallas_call(
        _peer_route_kernel,
        grid=(t // tb,),
        in_specs=[pl.BlockSpec((tb, d2), lambda j: (j, 0)),
                  pl.BlockSpec((2, PEER_N_KEYS, PEER_HALF), lambda j: (0, 0, 0))],
        out_specs=[tspec, tspec],
        out_shape=[jax.ShapeDtypeStruct((2, PEER_HEADS, PEER_N_KEYS, t), F32),
                   jax.ShapeDtypeStruct((2, PEER_HEADS, PEER_N_KEYS, t), BF16)],
        scratch_shapes=[pltpu.VMEM((PEER_TOPK, tb), F32), pltpu.VMEM((PEER_TOPK, tb), F32)],
        compiler_params=_cparams(("parallel",)),
        name="peer_route",
    )(q, sub_keys)


def _cast_kernel(x_ref, o_ref):
    o_ref[...] = x_ref[0].astype(o_ref.dtype)


def cast_layer_bf16(w, layer, *, tr=1024):
    _, r, c = w.shape
    return pl.pallas_call(
        _cast_kernel,
        grid=(r // tr,),
        in_specs=[pl.BlockSpec((1, tr, c), lambda i: (layer, i, 0))],
        out_specs=pl.BlockSpec((tr, c), lambda i: (i, 0)),
        out_shape=jax.ShapeDtypeStruct((r, c), BF16),
        compiler_params=_cparams(("parallel",)),
        name="cast_bf16",
    )(w)


def _transpose_cast_kernel(x_ref, o_ref):
    o_ref[...] = jnp.transpose(x_ref[0]).astype(o_ref.dtype)


def transpose_layer_bf16(w, layer, *, tr=512):
    _, r, c = w.shape
    return pl.pallas_call(
        _transpose_cast_kernel,
        grid=(r // tr,),
        in_specs=[pl.BlockSpec((1, tr, c), lambda i: (layer, i, 0))],
        out_specs=pl.BlockSpec((c, tr), lambda i: (0, i)),
        out_shape=jax.ShapeDtypeStruct((c, r), BF16),
        compiler_params=_cparams(("parallel",)),
        name="transpose_bf16",
    )(w)


def _peer_dense_kernel(u_ref, wd_ref, wut_ref, nc_ref, re_ref, x_ref, g_ref, o_ref, acc_s, ut_s, *,
                       rows_per_block, n_blocks):
    i = pl.program_id(1)

    @pl.when(i == 0)
    def _():
        acc_s[...] = jnp.zeros_like(acc_s)
        ut_s[...] = jnp.transpose(u_ref[...].astype(F32)).astype(BF16)

    tb = u_ref.shape[0]
    cw = min(PEER_COL_CHUNK, tb)
    e1s = pl.ds(pl.multiple_of(i * rows_per_block, rows_per_block), rows_per_block)
    n1_all = [nc_ref[0, h, e1s, :] for h in range(PEER_HEADS)]
    c1_all = [nc_ref[1, h, e1s, :] for h in range(PEER_HEADS)]
    chunks = [slice(c0, c0 + cw) for c0 in range(0, tb, cw)]
    hpres = [jnp.dot(wd_ref[...], ut_s[:, cs], preferred_element_type=F32) for cs in chunks]
    for cs, hpre in zip(chunks, hpres):
        gates = []
        for r in range(rows_per_block):
            acc = None
            for h in range(PEER_HEADS):
                n1 = jnp.broadcast_to(n1_all[h][r:r + 1, cs], (PACK, cw)).astype(BF16)[None]
                c1 = jnp.broadcast_to(c1_all[h][r:r + 1, cs], (PACK, cw)).astype(BF16)[None]
                rank2 = re_ref[0, h, :, cs].reshape(PEER_N_KEYS // PACK, PACK, cw)
                e2w = re_ref[1, h, :, cs].reshape(PEER_N_KEYS // PACK, PACK, cw)
                wgt = jnp.where(rank2 < n1, e2w * c1, jnp.zeros((), BF16))
                acc = wgt if acc is None else acc + wgt
            gates.append(acc.reshape(PEER_N_KEYS, cw))
        gate = jnp.concatenate(gates, axis=0).astype(F32)
        act = 0.5 * hpre * (1.0 + lax.erf(hpre * (2.0 ** -0.5))) * gate
        acc_s[:, cs] += jnp.dot(wut_ref[...], act.astype(BF16), preferred_element_type=F32)

    @pl.when(i == n_blocks - 1)
    def _():
        o_ref[...] = x_ref[...] + g_ref[0] * jnp.transpose(acc_s[...])


def peer_dense(u, w_down, w_up_t, nc, re, x, gates, group_of_tile, *, tb, eb):
    t, d = u.shape
    ne = w_down.shape[0]
    tb = min(tb, t)
    assert eb % PEER_N_KEYS == 0 and t % tb == 0
    nblk = ne // eb
    once = pl.Buffered(1)
    tspec = pl.BlockSpec((2, PEER_HEADS, PEER_N_KEYS, tb), lambda j, i: (0, 0, 0, j), pipeline_mode=once)
    return pl.pallas_call(
        functools.partial(_peer_dense_kernel, rows_per_block=eb // PEER_N_KEYS, n_blocks=nblk),
        grid=(t // tb, nblk),
        in_specs=[pl.BlockSpec((tb, d), lambda j, i: (j, 0), pipeline_mode=once),
                  pl.BlockSpec((eb, d), lambda j, i: (i, 0)),
                  pl.BlockSpec((d, eb), lambda j, i: (0, i)),
                  tspec, tspec,
                  pl.BlockSpec((tb, d), lambda j, i: (j, 0), pipeline_mode=once),
                  pl.BlockSpec((1, 1, d), lambda j, i: (group_of_tile(j, tb), 0, 0))],
        out_specs=pl.BlockSpec((tb, d), lambda j, i: (j, 0)),
        out_shape=jax.ShapeDtypeStruct((t, d), F32),
        scratch_shapes=[pltpu.VMEM((d, tb), F32), pltpu.VMEM((d, tb), BF16)],
        compiler_params=_cparams(("parallel", "arbitrary")),
        name="peer_dense",
    )(u, w_down, w_up_t, nc, re, x, gates)


def peer_residual(x, u, layer, w_q, sub_keys, w_down, w_up, gates, group_of_tile, *, tb=512, eb=1024):
    q = proj(u, w_q, layer=layer, tm=1024, tn=512, name="peer_q")
    nc, re = peer_route(q, sub_keys[layer], tb=256)
    return peer_dense(u, cast_layer_bf16(w_down, layer), transpose_layer_bf16(w_up, layer), nc, re,
                      x, gates, group_of_tile, tb=tb, eb=eb)


def _rope_tables(n_prompt_rows, n_seq, seq_len):
    n_rows = seq_len // GRID_W
    rows = jnp.repeat(jnp.arange(n_rows), GRID_W).astype(F32)
    cols = jnp.tile(jnp.arange(GRID_W), n_rows).astype(F32)
    n_freq = ROPE_AXIS_DIM // 2
    inv = ROPE_THETA ** (-jnp.arange(n_freq, dtype=F32) / n_freq)
    ar, ac = rows[:, None] * inv, cols[:, None] * inv
    cos_t = jnp.concatenate([jnp.cos(ar), jnp.cos(ar), jnp.cos(ac), jnp.cos(ac)], axis=-1)
    sin_t = jnp.concatenate([-jnp.sin(ar), jnp.sin(ar), -jnp.sin(ac), jnp.sin(ac)], axis=-1)
    cos_t = jnp.concatenate([jnp.ones((n_prompt_rows, HEAD_DIM), F32)] + [cos_t] * n_seq, axis=0)
    sin_t = jnp.concatenate([jnp.zeros((n_prompt_rows, HEAD_DIM), F32)] + [sin_t] * n_seq, axis=0)
    return cos_t, sin_t


def _qk_prep_kernel(pq_ref, pk_ref, qg_ref, kg_ref, cos_ref, sin_ref, q_ref, k_ref):
    cos_t = cos_ref[...]
    sin_t = sin_ref[...]
    lane = lax.broadcasted_iota(jnp.int32, cos_t.shape, 1)
    first = (lane % (ROPE_AXIS_DIM)) < (ROPE_AXIS_DIM // 2)
    quarter = ROPE_AXIS_DIM // 2

    def norm_rope(x, gain):
        y = x * lax.rsqrt(jnp.mean(x * x, axis=-1, keepdims=True) + EPS) * gain
        swapped = jnp.where(first, pltpu.roll(y, HEAD_DIM - quarter, 1), pltpu.roll(y, quarter, 1))
        return y * cos_t + swapped * sin_t

    for h in range(N_HEADS):
        hs = slice(h * HEAD_DIM, (h + 1) * HEAD_DIM)
        q_ref[:, hs] = norm_rope(pq_ref[:, hs], qg_ref[...]).astype(q_ref.dtype)
    for h in range(KV_HEADS):
        hs = slice(h * HEAD_DIM, (h + 1) * HEAD_DIM)
        k_ref[:, hs] = norm_rope(pk_ref[:, hs], kg_ref[...])


def qk_prep(p, q_gain, k_gain, cos_t, sin_t, *, tm):
    t = p.shape[0]
    tm = min(tm, t)
    assert FOURIER_WIDTH % ATTN_WIDTH == 0 and (FOURIER_WIDTH + ATTN_WIDTH) % KV_WIDTH == 0
    q_blk, k_blk = FOURIER_WIDTH // ATTN_WIDTH, (FOURIER_WIDTH + ATTN_WIDTH) // KV_WIDTH
    rspec = pl.BlockSpec((tm, HEAD_DIM), lambda i: (i, 0))
    gspec = pl.BlockSpec((1, HEAD_DIM), lambda i: (0, 0))
    return pl.pallas_call(
        _qk_prep_kernel,
        grid=(t // tm,),
        in_specs=[pl.BlockSpec((tm, ATTN_WIDTH), lambda i: (i, q_blk)),
                  pl.BlockSpec((tm, KV_WIDTH), lambda i: (i, k_blk)), gspec, gspec, rspec, rspec],
        out_specs=[pl.BlockSpec((tm, ATTN_WIDTH), lambda i: (i, 0)), pl.BlockSpec((tm, KV_WIDTH), lambda i: (i, 0))],
        out_shape=[jax.ShapeDtypeStruct((t, ATTN_WIDTH), BF16), jax.ShapeDtypeStruct((t, KV_WIDTH), F32)],
        compiler_params=_cparams(("parallel",)),
        name="qk_prep",
    )(p, p, q_gain.reshape(1, HEAD_DIM), k_gain.reshape(1, HEAD_DIM), cos_t, sin_t)


def kernel(x_prompt, x_sample, cache_k, cache_v, state_fwd, state_bwd, c, c_ctx, ada_w, ada_b, norm1, norm2,
           af_w_in, af_q_norm, af_k_norm, af_w_out, dn_w_in, dn_conv_w, dn_a_log, dn_dt_bias, dn_o_norm,
           dn_w_out, peer_w_q, peer_sub_keys, peer_w_down, peer_w_up, final_norm):
    nb, seq, d = x_prompt.shape
    db, dseq, _ = x_sample.shape
    depth = ada_w.shape[0]
    tp = nb * seq
    ts = db * dseq
    tt = tp + ts
    tm = math.gcd(math.gcd(tp, dseq), 1024)

    def group_of_tile(i, tile):
        r = i * tile
        return jnp.where(r < tp, 0, 1 + (r - tp) // dseq)

    conds = jnp.concatenate([c_ctx[None], c, jnp.zeros((8 - 1 - db, d), F32)], axis=0)
    mods = ada_all(conds, ada_w, ada_b)
    mods = mods.reshape(depth, 8, 6, d)

    x = jnp.concatenate([x_prompt.reshape(tp, d), x_sample.reshape(ts, d)], axis=0)
    new_k, new_v, new_sf, new_sb = [], [], [], []
    for i in range(depth):
        j = i // 2
        md = [mods[i, :, n][:, None, :] for n in range(6)]
        sh1, sc1, g1, sh2, sc2, g2 = md
        u = modulate(x, norm1[i], sh1, sc1, group_of_tile, tm=tm)
        if i % 2 == 0:
            p = proj(u, af_w_in, layer=j, tm=tm, tn=512, name="af_in")
            cos_t, sin_t = _rope_tables(tp, db, dseq)
            q, k = qk_prep(p, af_q_norm[j], af_k_norm[j], cos_t, sin_t, tm=tm)
            v = p[:, FOURIER_WIDTH + ATTN_WIDTH + KV_WIDTH:]
            new_k.append(k[:tp].reshape(nb, seq, KV_HEADS, HEAD_DIM))
            new_v.append(v[:tp].reshape(nb, seq, KV_HEADS, HEAD_DIM))
            keys = jnp.concatenate([k[tp:].reshape(db, dseq, KV_WIDTH),
                                    cache_k[:, j].reshape(db, -1, KV_WIDTH)], axis=1)
            vals = jnp.concatenate([v[tp:].reshape(db, dseq, KV_WIDTH),
                                    cache_v[:, j].reshape(db, -1, KV_WIDTH)], axis=1)
            att = attention(q[:tp].reshape(nb, seq, ATTN_WIDTH), k[:tp].reshape(nb, seq, KV_WIDTH),
                            v[:tp].reshape(nb, seq, KV_WIDTH), total_rows=tt, row0=0)
            att = attention(q[tp:].reshape(db, dseq, ATTN_WIDTH), keys, vals, total_rows=tt, row0=tp, prev_out=att)
            r = chan_dft(p, tm=tm)
            fou = seq_dft(r, n_seq=nb, seq_len=seq, row0=0)
            fou = seq_dft(r, n_seq=db, seq_len=dseq, row0=tp, prev_out=fou)
            x = proj_residual([fou, att], af_w_out, x, g1, group_of_tile, layer=j, tm=tm, tn=512, name="af_out")
        else:
            p = proj(u, dn_w_in, layer=j, tm=tm, tn=512, ncols=DN_CONV_CH + DN_V_WIDTH, name="dn_in")
            ba = proj(u, dn_w_in, layer=j, tm=tm, tn=128, col0=DN_CONV_CH + DN_V_WIDTH, name="dn_in_ba")
            beta_all = jax.nn.sigmoid(ba.reshape(tt, 2, 2, DN_V_HEADS)[:, 0])
            g_all = -jnp.exp(dn_a_log[j]) * jax.nn.softplus(ba.reshape(tt, 2, 2, DN_V_HEADS)[:, 1] + dn_dt_bias[j])
            qkv = dn_prep(p, dn_conv_w[j], n_seq=nb, seq_len=seq, row0=0)
            qkv = dn_prep(p, dn_conv_w[j], n_seq=db, seq_len=dseq, row0=tp, prev_out=qkv)
            zeros = jnp.zeros((nb, DN_V_HEADS, DN_K_DIM, DN_V_DIM), F32)
            on, s_f, s_b = delta_mixer(qkv, p, beta_all[:tp].reshape(nb, seq, 2, DN_V_HEADS),
                                       g_all[:tp].reshape(nb, seq, 2, DN_V_HEADS), dn_o_norm[j], zeros, zeros,
                                       n_seq=nb, seq_len=seq, row0=0)
            new_sf.append(s_f)
            new_sb.append(s_b)
            on, _, _ = delta_mixer(qkv, p, beta_all[tp:].reshape(db, dseq, 2, DN_V_HEADS),
                                   g_all[tp:].reshape(db, dseq, 2, DN_V_HEADS), dn_o_norm[j],
                                   state_fwd[:, j], state_bwd[:, j], n_seq=db, seq_len=dseq, row0=tp, prev_out=on)
            x = proj_residual([on], dn_w_out, x, g1, group_of_tile, layer=j, tm=tm, tn=256, name="dn_out")
        u2 = modulate(x, norm2[i], sh2, sc2, group_of_tile, tm=tm)
        x = peer_residual(x, u2, i, peer_w_q, peer_sub_keys, peer_w_down, peer_w_up, g2, group_of_tile)
    return (rmsnorm_rows(x, final_norm, row0=0, nrows=tp, tm=tm).reshape(nb, seq, d),
            rmsnorm_rows(x, final_norm, row0=tp, nrows=ts, tm=tm).reshape(db, dseq, d),
            jnp.stack(new_k, axis=1), jnp.stack(new_v, axis=1),
            jnp.stack(new_sf, axis=1), jnp.stack(new_sb, axis=1))
```

```python
import functools
import math

import jax
import jax.numpy as jnp
import numpy as np
from jax import lax
from jax.experimental import pallas as pl
from jax.experimental.pallas import tpu as pltpu

F32 = jnp.float32
BF16 = jnp.bfloat16

EPS = 1e-6
HEAD_DIM = 128
N_HEADS = 8
KV_HEADS = 2
GQA_GROUP = N_HEADS // KV_HEADS
FOURIER_GROUPS = 4
FOURIER_GROUP_DIM = 256
FOURIER_WIDTH = FOURIER_GROUPS * FOURIER_GROUP_DIM
ATTN_WIDTH = N_HEADS * HEAD_DIM
KV_WIDTH = KV_HEADS * HEAD_DIM
GRID_W = 64
ROPE_THETA = 10000.0
ROPE_AXIS_DIM = HEAD_DIM // 2

DN_QK_HEADS = 16
DN_V_HEADS = 32
DN_K_DIM = 128
DN_V_DIM = 128
DN_QK_WIDTH = DN_QK_HEADS * DN_K_DIM
DN_V_WIDTH = DN_V_HEADS * DN_V_DIM
DN_CONV_CH = 2 * DN_QK_WIDTH + DN_V_WIDTH
DN_CHUNK = 64

PEER_HEADS = 8
PEER_N_KEYS = 128
PEER_HALF = 128
PEER_TOPK = 16
PEER_COL_CHUNK = 256
PACK = 16

VMEM_LIMIT = 56 * 1024 * 1024


def _cparams(sem):
    return pltpu.CompilerParams(dimension_semantics=sem, vmem_limit_bytes=VMEM_LIMIT)


def _bdot(a, b):
    return jnp.dot(a.astype(BF16), b.astype(BF16), preferred_element_type=F32)


def _bdot_nt(a, b):
    return lax.dot_general(a.astype(BF16), b.astype(BF16), (((1,), (1,)), ((), ())),
                           preferred_element_type=F32)


def _bmm(a, b):
    return lax.dot_general(a.astype(BF16), b.astype(BF16), (((2,), (1,)), ((0,), (0,))),
                           preferred_element_type=F32)


def _bmm_nt(a, b):
    return lax.dot_general(a.astype(BF16), b.astype(BF16), (((2,), (2,)), ((0,), (0,))),
                           preferred_element_type=F32)


def _proj_kernel(x_ref, w_ref, o_ref):
    o_ref[...] = _bdot(x_ref[...], w_ref[0]).astype(o_ref.dtype)


def proj(x, w, *, layer=0, tm, tn, col0=0, ncols=None, out_dtype=F32, name="proj"):
    m, k = x.shape
    ncols = w.shape[2] - col0 if ncols is None else ncols
    tm = min(tm, m)
    tn = min(tn, ncols)
    assert m % tm == 0 and ncols % tn == 0 and col0 % tn == 0
    cb = col0 // tn
    return pl.pallas_call(
        _proj_kernel,
        grid=(m // tm, ncols // tn),
        in_specs=[pl.BlockSpec((tm, k), lambda i, j: (i, 0)),
                  pl.BlockSpec((1, k, tn), lambda i, j: (layer, 0, j + cb))],
        out_specs=pl.BlockSpec((tm, tn), lambda i, j: (i, j)),
        out_shape=jax.ShapeDtypeStruct((m, ncols), out_dtype),
        compiler_params=_cparams(("parallel", "arbitrary")),
        name=name,
    )(x, w)


def _proj_res_kernel(*refs, n_parts):
    x_refs, w_refs = refs[:n_parts], refs[n_parts:2 * n_parts]
    res_ref, gate_ref, o_ref = refs[2 * n_parts:]
    acc = _bdot(x_refs[0][...], w_refs[0][0])
    for x_ref, w_ref in zip(x_refs[1:], w_refs[1:]):
        acc = acc + _bdot(x_ref[...], w_ref[0])
    o_ref[...] = res_ref[...] + gate_ref[0] * acc


def proj_residual(xs, w, res, gates, group_of_tile, *, layer=0, tm, tn, name="proj_res"):
    m = xs[0].shape[0]
    n = w.shape[2]
    tm = min(tm, m)
    tn = min(tn, n)
    kp = xs[0].shape[1]
    assert m % tm == 0 and n % tn == 0 and all(x.shape == (m, kp) for x in xs)
    x_specs = [pl.BlockSpec((tm, kp), lambda i, j: (i, 0)) for _ in xs]
    w_specs = [pl.BlockSpec((1, kp, tn), functools.partial(lambda i, j, part: (layer, part, j), part=part))
               for part in range(len(xs))]
    return pl.pallas_call(
        functools.partial(_proj_res_kernel, n_parts=len(xs)),
        grid=(m // tm, n // tn),
        in_specs=x_specs + w_specs + [pl.BlockSpec((tm, tn), lambda i, j: (i, j)),
                                      pl.BlockSpec((1, 1, tn), lambda i, j: (group_of_tile(i, tm), 0, j))],
        out_specs=pl.BlockSpec((tm, tn), lambda i, j: (i, j)),
        out_shape=jax.ShapeDtypeStruct((m, n), F32),
        compiler_params=_cparams(("parallel", "arbitrary")),
        name=name,
    )(*xs, *([w] * len(xs)), res, gates)


def _modulate_kernel(x_ref, gain_ref, shift_ref, scale_ref, o_ref):
    x = x_ref[...]
    y = x * lax.rsqrt(jnp.mean(x * x, axis=-1, keepdims=True) + EPS)
    o_ref[...] = (y * gain_ref[...] * (1.0 + scale_ref[0]) + shift_ref[0]).astype(o_ref.dtype)


def modulate(x, gain, shift, scale, group_of_tile, *, tm, out_dtype=BF16):
    m, d = x.shape
    tm = min(tm, m)
    gspec = pl.BlockSpec((1, 1, d), lambda i: (group_of_tile(i, tm), 0, 0))
    return pl.pallas_call(
        _modulate_kernel,
        grid=(m // tm,),
        in_specs=[pl.BlockSpec((tm, d), lambda i: (i, 0)),
                  pl.BlockSpec((1, d), lambda i: (0, 0)), gspec, gspec],
        out_specs=pl.BlockSpec((tm, d), lambda i: (i, 0)),
        out_shape=jax.ShapeDtypeStruct((m, d), out_dtype),
        compiler_params=_cparams(("parallel",)),
        name="modulate",
    )(x, gain.reshape(1, d), shift, scale)


def _rmsnorm_kernel(x_ref, gain_ref, o_ref):
    x = x_ref[...]
    o_ref[...] = x * lax.rsqrt(jnp.mean(x * x, axis=-1, keepdims=True) + EPS) * gain_ref[...]


def rmsnorm_rows(x, gain, *, row0, nrows, tm):
    d = x.shape[1]
    tm = min(tm, nrows)
    assert row0 % tm == 0 and nrows % tm == 0
    rb0 = row0 // tm
    return pl.pallas_call(
        _rmsnorm_kernel,
        grid=(nrows // tm,),
        in_specs=[pl.BlockSpec((tm, d), lambda i: (rb0 + i, 0)), pl.BlockSpec((1, d), lambda i: (0, 0))],
        out_specs=pl.BlockSpec((tm, d), lambda i: (i, 0)),
        out_shape=jax.ShapeDtypeStruct((nrows, d), F32),
        compiler_params=_cparams(("parallel",)),
        name="final_norm",
    )(x, gain.reshape(1, d))


def _ada_kernel(c_ref, w_ref, b_ref, o_ref):
    c = c_ref[...]
    o_ref[0] = _bdot(c * jax.nn.sigmoid(c), w_ref[0]) + b_ref[0]


def ada_all(conds, ada_w, ada_b, *, tn=1024):
    depth, d, n = ada_w.shape
    r = conds.shape[0]
    return pl.pallas_call(
        _ada_kernel,
        grid=(depth, n // tn),
        in_specs=[pl.BlockSpec((r, d), lambda l, j: (0, 0)),
                  pl.BlockSpec((1, d, tn), lambda l, j: (l, 0, j)),
                  pl.BlockSpec((1, 1, tn), lambda l, j: (l, 0, j))],
        out_specs=pl.BlockSpec((1, r, tn), lambda l, j: (l, 0, j)),
        out_shape=jax.ShapeDtypeStruct((depth, r, n), F32),
        compiler_params=_cparams(("parallel", "arbitrary")),
        name="ada",
    )(conds, ada_w, ada_b.reshape(depth, 1, n))


def _dft_tables(n, scale):
    idx = np.arange(n, dtype=np.int64)
    ang = 2.0 * np.pi * ((idx[:, None] * idx[None, :]) % n).astype(np.float64) / n
    return np.stack([np.cos(ang) * scale, np.sin(ang) * scale]).astype(np.float32)


def _chan_dft_kernel(x_ref, t_ref, o_ref):
    o_ref[0] = _bdot(x_ref[...], t_ref[0]).astype(o_ref.dtype)


def _seq_dft_kernel(t_ref, r_ref, *rest):
    o_ref = rest[-1]
    o_ref[...] = (_bdot(t_ref[0], r_ref[0]) - _bdot(t_ref[1], r_ref[1])).astype(o_ref.dtype)


def chan_dft(p, *, tm):
    t = p.shape[0]
    gd = FOURIER_GROUP_DIM
    tm = min(tm, t)
    ctab = jnp.asarray(_dft_tables(gd, gd ** -0.5), BF16)
    return pl.pallas_call(
        _chan_dft_kernel,
        grid=(t // tm, FOURIER_GROUPS, 2),
        in_specs=[pl.BlockSpec((tm, gd), lambda i, g, s: (i, g)),
                  pl.BlockSpec((1, gd, gd), lambda i, g, s: (s, 0, 0))],
        out_specs=pl.BlockSpec((1, tm, gd), lambda i, g, s: (s, i, g)),
        out_shape=jax.ShapeDtypeStruct((2, t, FOURIER_WIDTH), BF16),
        compiler_params=_cparams(("parallel", "arbitrary", "arbitrary")),
        name="chan_dft",
    )(p, ctab)


def seq_dft(r, *, n_seq, seq_len, row0, prev_out=None, out_dtype=BF16):
    t = r.shape[1]
    assert row0 % seq_len == 0
    stab = jnp.asarray(_dft_tables(seq_len, seq_len ** -0.5), BF16)
    ts = min(512, seq_len)
    tn = 512
    nrow = seq_len // ts
    sb0 = row0 // seq_len
    rb0 = row0 // ts
    in_specs = [pl.BlockSpec((2, ts, seq_len), lambda b, i, j: (0, i, 0)),
                pl.BlockSpec((2, seq_len, tn), lambda b, i, j: (0, sb0 + b, j))]
    args = [stab, r]
    aliases = {}
    if prev_out is not None:
        in_specs.append(pl.BlockSpec(memory_space=pl.ANY))
        args.append(prev_out)
        aliases = {2: 0}
    return pl.pallas_call(
        _seq_dft_kernel,
        grid=(n_seq, nrow, FOURIER_WIDTH // tn),
        in_specs=in_specs,
        out_specs=pl.BlockSpec((ts, tn), lambda b, i, j: (rb0 + b * nrow + i, j)),
        out_shape=jax.ShapeDtypeStruct((t, FOURIER_WIDTH), out_dtype),
        input_output_aliases=aliases,
        compiler_params=_cparams(("parallel", "arbitrary", "arbitrary")),
        name="seq_dft",
    )(*args)


def _attn_kernel(q_ref, k_ref, v_ref, *rest):
    o_ref = rest[-1]
    k = k_ref[0].astype(BF16)
    v = v_ref[0].astype(BF16)
    scale = HEAD_DIM ** -0.5
    for g in range(GQA_GROUP):
        q = q_ref[0, :, g * HEAD_DIM:(g + 1) * HEAD_DIM]
        s = _bdot_nt(q, k)
        e = jnp.exp((s - jnp.max(s, axis=-1, keepdims=True)) * scale)
        o = _bdot(e, v) / jnp.sum(e, axis=-1, keepdims=True)
        o_ref[:, g * HEAD_DIM:(g + 1) * HEAD_DIM] = o.astype(o_ref.dtype)


def attention(q, k, v, *, total_rows, row0, prev_out=None, tq=256, out_dtype=BF16):
    b, sq, _ = q.shape
    sk = k.shape[1]
    tq = min(tq, sq)
    assert row0 % tq == 0
    gw = GQA_GROUP * HEAD_DIM
    nq = sq // tq
    rb0 = row0 // tq
    in_specs = [pl.BlockSpec((1, tq, gw), lambda bb, h, i: (bb, i, h)),
                pl.BlockSpec((1, sk, HEAD_DIM), lambda bb, h, i: (bb, 0, h)),
                pl.BlockSpec((1, sk, HEAD_DIM), lambda bb, h, i: (bb, 0, h))]
    args = [q, k, v]
    aliases = {}
    if prev_out is not None:
        in_specs.append(pl.BlockSpec(memory_space=pl.ANY))
        args.append(prev_out)
        aliases = {3: 0}
    return pl.pallas_call(
        _attn_kernel,
        grid=(b, KV_HEADS, nq),
        in_specs=in_specs,
        out_specs=pl.BlockSpec((tq, gw), lambda bb, h, i: (rb0 + bb * nq + i, h)),
        out_shape=jax.ShapeDtypeStruct((total_rows, ATTN_WIDTH), out_dtype),
        input_output_aliases=aliases,
        compiler_params=_cparams(("parallel", "parallel", "arbitrary")),
        name="attention",
    )(*args)


N_PROB = 4
DELTA_VMEM_BYTES_PER_ROW_HEAD = 14 * 1024


def _tri_masks(c):
    ri = lax.broadcasted_iota(jnp.int32, (c, c), 0)
    ci = lax.broadcasted_iota(jnp.int32, (c, c), 1)
    return ri, ci


def _unit_tri_inverse(low, ri, ci):
    c = low.shape[-1]

    def same_block(size):
        sh = int(math.log2(size))
        return (ri >> sh) == (ci >> sh)

    leaf = 4
    dg = jnp.where(same_block(leaf), low, 0.0)
    imd = jnp.where(ri == ci, 1.0, 0.0) - dg
    inv = imd + _bmm(imd, _bmm(dg, dg))
    size = leaf
    while size < c:
        off = jnp.where(same_block(2 * size) & jnp.logical_not(same_block(size)), low, 0.0)
        inv = inv - _bmm(inv, _bmm(off, inv))
        size *= 2
    return inv


def _delta_kernel(*refs, n_chunks, unroll_a, kh_step, has_prev):
    n_in = 9 + int(has_prev)
    q_ref, k_ref, v_ref, z_ref, gain_ref, cols_ref, rows_ref, sf0_ref, sb0_ref = refs[:9]
    on_ref, sf_ref, sb_ref, u_s, wq_s, akt_s, gl_s, o_s = refs[n_in:]
    c = DN_CHUNK
    dk = DN_K_DIM
    dv = DN_V_DIM
    ri, ci = _tri_masks(c)

    ca = unroll_a
    n_chain = kh_step * N_PROB

    def vhead(pp):
        return 2 * (pp // N_PROB) + pp % 2

    def body_a(hq, g0, carry):
        rows = pl.ds(pl.multiple_of(g0 * (ca * c), ca * c), ca * c)
        chs = pl.ds(g0 * ca, ca)
        k = k_ref[rows, hq * dk:(hq + 1) * dk].astype(F32).reshape(ca, c, dk)
        q = q_ref[rows, hq * dk:(hq + 1) * dk].astype(F32).reshape(ca, c, dk)
        kk = _bmm_nt(k, k)
        qk = _bmm_nt(q, k)
        cols = cols_ref[0, hq, rows, :].reshape(ca, c, 2 * N_PROB)
        rws = rows_ref[0, hq, chs]
        lows, rhss, decays, egs, glasts, gcols = [], [], [], [], [], []
        for p in range(N_PROB):
            head, reverse = p % 2, p >= 2
            beta = cols[:, :, p:p + 1]
            gcol = cols[:, :, N_PROB + p:N_PROB + p + 1]
            grow = rws[:, p:p + 1, :]
            glast = grow[:, :, 0:1] if reverse else grow[:, :, c - 1:c]
            incl = (ri <= ci) if reverse else (ri >= ci)
            strict = (ri < ci) if reverse else (ri > ci)
            decay = jnp.where(incl, jnp.exp(jnp.where(incl, gcol - grow, 0.0)), 0.0)
            eg = jnp.exp(gcol)
            vh = 2 * hq + head
            v = v_ref[rows, vh * dv:(vh + 1) * dv].astype(F32).reshape(ca, c, dv)
            lows.append(jnp.where(strict, kk * beta * decay, 0.0))
            rhss.append(jnp.concatenate([v * beta, k * (beta * eg)], axis=-1))
            decays.append(decay)
            egs.append(eg)
            glasts.append(glast)
            gcols.append(gcol)
        inv = _unit_tri_inverse(jnp.concatenate(lows, axis=0), ri, ci)
        sol = _bmm(inv, jnp.concatenate(rhss, axis=0))
        for p in range(N_PROB):
            pp = hq * N_PROB + p
            sp = sol[p * ca:(p + 1) * ca]
            u_s[pp, rows, :] = sp[:, :, :dv].reshape(ca * c, dv)
            wq_s[pp, chs, 0:c, :] = sp[:, :, dv:].astype(BF16)
            wq_s[pp, chs, c:2 * c, :] = (q * egs[p]).astype(BF16)
            akt_s[pp, chs, 0:c, :] = (qk * decays[p]).astype(BF16)
            akt_s[pp, chs, c:c + dk, :] = jnp.swapaxes(k * jnp.exp(glasts[p] - gcols[p]), 1, 2).astype(BF16)
            gl_s[pp, chs] = jnp.broadcast_to(jnp.exp(glasts[p]), (ca, 8, dv))
        return carry

    for hq in range(kh_step):
        lax.fori_loop(0, n_chunks // ca, functools.partial(body_a, hq), 0)
    o_s[...] = jnp.zeros_like(o_s)

    def body_b(i, states):
        chs = [(n_chunks - 1 - i) if pp % N_PROB >= 2 else i for pp in range(n_chain)]
        rows = [pl.ds(pl.multiple_of(ch * c, c), c) for ch in chs]
        ws = [_bdot(wq_s[pp, chs[pp]], states[pp]) for pp in range(n_chain)]
        vn = [u_s[pp, rows[pp], :] - ws[pp][0:c] for pp in range(n_chain)]
        av = [_bdot(akt_s[pp, chs[pp]], vn[pp]) for pp in range(n_chain)]
        new_states = []
        for pp in range(n_chain):
            head = vhead(pp)
            o_s[rows[pp], head * dv:(head + 1) * dv] += ws[pp][c:2 * c] + av[pp][0:c]
            new_states.append(states[pp] * gl_s[pp, chs[pp]][0:1, :] + av[pp][c:c + dk])
        return tuple(new_states)

    init = tuple((sb0_ref if pp % N_PROB >= 2 else sf0_ref)[0, vhead(pp)] for pp in range(n_chain))
    fin = lax.fori_loop(0, n_chunks, body_b, init)
    for pp in range(n_chain):
        (sb_ref if pp % N_PROB >= 2 else sf_ref)[0, vhead(pp)] = fin[pp]
    for head in range(2 * kh_step):
        hs = slice(head * dv, (head + 1) * dv)
        o = o_s[:, hs]
        z = z_ref[:, hs].astype(F32)
        on = o * lax.rsqrt(jnp.mean(o * o, axis=-1, keepdims=True) + EPS) * gain_ref[...]
        on_ref[:, hs] = (on * (z * jax.nn.sigmoid(z))).astype(on_ref.dtype)


def delta_mixer(qkv, p, beta, g, gain, s_f0, s_b0, *, n_seq, seq_len, row0, prev_out=None):
    b, t = n_seq, seq_len
    tt = qkv.shape[0]
    h = DN_V_HEADS
    kh = DN_QK_HEADS
    rep = h // kh
    assert rep == 2 and row0 % t == 0
    rb0 = row0 // t
    c = DN_CHUNK
    nc = t // c
    gch = g.reshape(b, nc, c, 2, h)
    gcf = jnp.cumsum(gch[:, :, :, 0], axis=2).reshape(b, t, h)
    gcb = jnp.flip(jnp.cumsum(jnp.flip(gch[:, :, :, 1], axis=2), axis=2), axis=2).reshape(b, t, h)
    cols = jnp.stack([beta[:, :, 0], beta[:, :, 1], gcf, gcb], axis=2).reshape(b, t, 4, kh, rep)
    cols = jnp.transpose(cols, (0, 3, 1, 2, 4)).reshape(b, kh, t, 4 * rep)
    rows = jnp.stack([gcf, gcb], axis=2).reshape(b, nc, c, 2, kh, rep)
    rows = jnp.transpose(rows, (0, 4, 1, 3, 5, 2)).reshape(b, kh, nc, N_PROB, c)
    kh_step = 2 if t * 2 * DELTA_VMEM_BYTES_PER_ROW_HEAD <= VMEM_LIMIT // 2 else 1
    stspec = pl.BlockSpec((1, rep * kh_step, DN_K_DIM, DN_V_DIM), lambda bb, hh: (bb, hh, 0, 0))
    unroll_a = math.gcd(nc, 8)
    kw = kh_step * DN_K_DIM
    vw = kh_step * rep * DN_V_DIM
    k_blk0 = DN_QK_WIDTH // kw
    v_blk0 = 2 * DN_QK_WIDTH // vw
    z_blk0 = DN_CONV_CH // vw
    in_specs = [pl.BlockSpec((t, kw), lambda bb, hh: (rb0 + bb, hh)),
                pl.BlockSpec((t, kw), lambda bb, hh: (rb0 + bb, k_blk0 + hh)),
                pl.BlockSpec((t, vw), lambda bb, hh: (rb0 + bb, v_blk0 + hh)),
                pl.BlockSpec((t, vw), lambda bb, hh: (rb0 + bb, z_blk0 + hh)),
                pl.BlockSpec((1, DN_V_DIM), lambda bb, hh: (0, 0)),
                pl.BlockSpec((1, kh_step, t, 4 * rep), lambda bb, hh: (bb, hh, 0, 0)),
                pl.BlockSpec((1, kh_step, nc, N_PROB, c), lambda bb, hh: (bb, hh, 0, 0, 0)),
                stspec, stspec]
    args = [qkv, qkv, qkv, p, gain.reshape(1, DN_V_DIM), cols, rows, s_f0, s_b0]
    aliases = {}
    if prev_out is not None:
        in_specs.append(pl.BlockSpec(memory_space=pl.ANY))
        args.append(prev_out)
        aliases = {len(args) - 1: 0}
    on, sf, sb = pl.pallas_call(
        functools.partial(_delta_kernel, n_chunks=nc, unroll_a=unroll_a, kh_step=kh_step,
                          has_prev=prev_out is not None),
        grid=(b, kh // kh_step),
        in_specs=in_specs,
        out_specs=[pl.BlockSpec((t, vw), lambda bb, hh: (rb0 + bb, hh)), stspec, stspec],
        out_shape=[jax.ShapeDtypeStruct((tt, DN_V_WIDTH), BF16),
                   jax.ShapeDtypeStruct((b, h, DN_K_DIM, DN_V_DIM), F32),
                   jax.ShapeDtypeStruct((b, h, DN_K_DIM, DN_V_DIM), F32)],
        scratch_shapes=[pltpu.VMEM((kh_step * N_PROB, t, DN_V_DIM), F32),
                        pltpu.VMEM((kh_step * N_PROB, nc, 2 * c, DN_K_DIM), BF16),
                        pltpu.VMEM((kh_step * N_PROB, nc, c + DN_K_DIM, c), BF16),
                        pltpu.VMEM((kh_step * N_PROB, nc, 8, DN_V_DIM), F32),
                        pltpu.VMEM((t, vw), F32)],
        input_output_aliases=aliases,
        compiler_params=_cparams(("parallel", "parallel")),
        name="delta_rule",
    )(*args)
    return on, sf, sb


def _dn_prep_kernel(*refs, has_prev):
    p_ref, w_ref = refs[:2]
    o_ref = refs[2 + int(has_prev)]
    j = pl.program_id(1)
    x = p_ref[...].astype(F32)
    s, cb = x.shape
    t = lax.broadcasted_iota(jnp.int32, (s, cb), 0)
    w = w_ref[...]
    y = (w[0:1] * jnp.where(t >= 2, pltpu.roll(x, 2, 0), 0.0) + w[1:2] * jnp.where(t >= 1, pltpu.roll(x, 1, 0), 0.0)
         + w[2:3] * x + w[3:4] * jnp.where(t < s - 1, pltpu.roll(x, s - 1, 0), 0.0))
    y = y * jax.nn.sigmoid(y)
    qk_blocks = DN_QK_WIDTH // cb
    for hh in range(cb // DN_K_DIM):
        seg = y[:, hh * DN_K_DIM:(hh + 1) * DN_K_DIM]
        inv = lax.rsqrt(jnp.sum(seg * seg, axis=-1, keepdims=True) + EPS)
        mult = jnp.where(j < qk_blocks, inv * (DN_K_DIM ** -0.5), jnp.where(j < 2 * qk_blocks, inv, 1.0))
        o_ref[:, hh * DN_K_DIM:(hh + 1) * DN_K_DIM] = (seg * mult).astype(o_ref.dtype)


def dn_prep(p, conv_w, *, n_seq, seq_len, row0, prev_out=None, cb=512):
    tt = p.shape[0]
    assert row0 % seq_len == 0 and DN_QK_WIDTH % cb == 0
    rb0 = row0 // seq_len
    in_specs = [pl.BlockSpec((seq_len, cb), lambda b, j: (rb0 + b, j)),
                pl.BlockSpec((conv_w.shape[0], cb), lambda b, j: (0, j))]
    args = [p, conv_w]
    aliases = {}
    if prev_out is not None:
        in_specs.append(pl.BlockSpec(memory_space=pl.ANY))
        args.append(prev_out)
        aliases = {2: 0}
    return pl.pallas_call(
        functools.partial(_dn_prep_kernel, has_prev=prev_out is not None),
        grid=(n_seq, DN_CONV_CH // cb),
        in_specs=in_specs,
        out_specs=pl.BlockSpec((seq_len, cb), lambda b, j: (rb0 + b, j)),
        out_shape=jax.ShapeDtypeStruct((tt, DN_CONV_CH), BF16),
        input_output_aliases=aliases,
        compiler_params=_cparams(("parallel", "arbitrary")),
        name="dn_prep",
    )(*args)


def _peer_route_kernel(q_ref, keys_ref, nc_ref, re_ref, v1_ref, v2_ref):
    kk = PEER_TOPK
    neg = -jnp.inf
    for h in range(PEER_HEADS):
        r0 = h * 2 * PEER_HALF
        s1 = _bdot_nt(keys_ref[0], q_ref[:, r0:r0 + PEER_HALF])
        s2 = _bdot_nt(keys_ref[1], q_ref[:, r0 + PEER_HALF:r0 + 2 * PEER_HALF])
        sc = s1
        for r in range(kk):
            m = jnp.max(sc, axis=0, keepdims=True)
            v1_ref[r:r + 1, :] = m
            sc = jnp.where(sc == m, neg, sc)
        sc = s2
        rank2 = jnp.full(s2.shape, float(kk), F32)
        for r in range(kk):
            m = jnp.max(sc, axis=0, keepdims=True)
            v2_ref[r:r + 1, :] = m
            hit = sc == m
            rank2 = jnp.where(hit, float(r), rank2)
            sc = jnp.where(hit, neg, sc)
        v2 = v2_ref[...]
        cand = jnp.concatenate([v1_ref[0:1, :] + v2] + [v1_ref[a:a + 1, :] + v2[0:8] for a in range(1, kk)],
                               axis=0)
        top = cand[0:1, :]
        work = cand
        tau = top
        for r in range(kk):
            tau = jnp.max(work, axis=0, keepdims=True)
            work = jnp.where(work == tau, neg, work)
        sel = cand >= tau
        z = jnp.sum(jnp.where(sel, jnp.exp(cand - top), 0.0), axis=0, keepdims=True)
        one = jnp.where(sel, 1.0, 0.0)
        n1 = jnp.zeros(s1.shape, F32)
        for a in range(kk):
            lo, hi = (0, kk) if a == 0 else (kk + 8 * (a - 1), kk + 8 * a)
            cnt = jnp.sum(one[lo:hi], axis=0, keepdims=True)
            n1 = jnp.where(s1 == v1_ref[a:a + 1, :], cnt, n1)
        nc_ref[0, h] = n1
        nc_ref[1, h] = jnp.exp(s1 - v1_ref[0:1, :]) / z
        re_ref[0, h] = rank2.astype(BF16)
        re_ref[1, h] = jnp.exp(s2 - v2[0:1, :]).astype(BF16)


def peer_route(q, sub_keys, *, tb):
    t, d2 = q.shape
    tb = min(tb, t)
    tspec = pl.BlockSpec((2, PEER_HEADS, PEER_N_KEYS, tb), lambda j: (0, 0, 0, j))
    return pl.pallas_call(
        _peer_route_kernel,
        grid=(t // tb,),
        in_specs=[pl.BlockSpec((tb, d2), lambda j: (j, 0)),
                  pl.BlockSpec((2, PEER_N_KEYS, PEER_HALF), lambda j: (0, 0, 0))],
        out_specs=[tspec, tspec],
        out_shape=[jax.ShapeDtypeStruct((2, PEER_HEADS, PEER_N_KEYS, t), F32),
                   jax.ShapeDtypeStruct((2, PEER_HEADS, PEER_N_KEYS, t), BF16)],
        scratch_shapes=[pltpu.VMEM((PEER_TOPK, tb), F32), pltpu.VMEM((PEER_TOPK, tb), F32)],
        compiler_params=_cparams(("parallel",)),
        name="peer_route",
    )(q, sub_keys)


def _cast_kernel(x_ref, o_ref):
    o_ref[...] = x_ref[0].astype(o_ref.dtype)


def cast_layer_bf16(w, layer, *, tr=1024):
    _, r, c = w.shape
    return pl.pallas_call(
        _cast_kernel,
        grid=(r // tr,),
        in_specs=[pl.BlockSpec((1, tr, c), lambda i: (layer, i, 0))],
        out_specs=pl.BlockSpec((tr, c), lambda i: (i, 0)),
        out_shape=jax.ShapeDtypeStruct((r, c), BF16),
        compiler_params=_cparams(("parallel",)),
        name="cast_bf16",
    )(w)


def _transpose_cast_kernel(x_ref, o_ref):
    o_ref[...] = jnp.transpose(x_ref[0]).astype(o_ref.dtype)


def transpose_layer_bf16(w, layer, *, tr=512):
    _, r, c = w.shape
    return pl.pallas_call(
        _transpose_cast_kernel,
        grid=(r // tr,),
        in_specs=[pl.BlockSpec((1, tr, c), lambda i: (layer, i, 0))],
        out_specs=pl.BlockSpec((c, tr), lambda i: (0, i)),
        out_shape=jax.ShapeDtypeStruct((c, r), BF16),
        compiler_params=_cparams(("parallel",)),
        name="transpose_bf16",
    )(w)


def _peer_dense_kernel(u_ref, wd_ref, wut_ref, nc_ref, re_ref, x_ref, g_ref, o_ref, acc_s, ut_s, *,
                       rows_per_block, n_blocks):
    i = pl.program_id(1)

    @pl.when(i == 0)
    def _():
        acc_s[...] = jnp.zeros_like(acc_s)
        ut_s[...] = jnp.transpose(u_ref[...].astype(F32)).astype(BF16)

    tb = u_ref.shape[0]
    cw = min(PEER_COL_CHUNK, tb)
    e1s = pl.ds(pl.multiple_of(i * rows_per_block, rows_per_block), rows_per_block)
    n1_all = [nc_ref[0, h, e1s, :] for h in range(PEER_HEADS)]
    c1_all = [nc_ref[1, h, e1s, :] for h in range(PEER_HEADS)]
    chunks = [slice(c0, c0 + cw) for c0 in range(0, tb, cw)]
    hpres = [jnp.dot(wd_ref[...], ut_s[:, cs], preferred_element_type=F32) for cs in chunks]
    for cs, hpre in zip(chunks, hpres):
        gates = []
        for r in range(rows_per_block):
            acc = None
            for h in range(PEER_HEADS):
                n1 = jnp.broadcast_to(n1_all[h][r:r + 1, cs], (PACK, cw)).astype(BF16)[None]
                c1 = jnp.broadcast_to(c1_all[h][r:r + 1, cs], (PACK, cw)).astype(BF16)[None]
                rank2 = re_ref[0, h, :, cs].reshape(PEER_N_KEYS // PACK, PACK, cw)
                e2w = re_ref[1, h, :, cs].reshape(PEER_N_KEYS // PACK, PACK, cw)
                wgt = jnp.where(rank2 < n1, e2w * c1, jnp.zeros((), BF16))
                acc = wgt if acc is None else acc + wgt
            gates.append(acc.reshape(PEER_N_KEYS, cw))
        gate = jnp.concatenate(gates, axis=0).astype(F32)
        act = 0.5 * hpre * (1.0 + lax.erf(hpre * (2.0 ** -0.5))) * gate
        acc_s[:, cs] += jnp.dot(wut_ref[...], act.astype(BF16), preferred_element_type=F32)

    @pl.when(i == n_blocks - 1)
    def _():
        o_ref[...] = x_ref[...] + g_ref[0] * jnp.transpose(acc_s[...])


def peer_dense(u, w_down, w_up_t, nc, re, x, gates, group_of_tile, *, tb, eb):
    t, d = u.shape
    ne = w_down.shape[0]
    tb = min(tb, t)
    assert eb % PEER_N_KEYS == 0 and t % tb == 0
    nblk = ne // eb
    once = pl.Buffered(1)
    tspec = pl.BlockSpec((2, PEER_HEADS, PEER_N_KEYS, tb), lambda j, i: (0, 0, 0, j), pipeline_mode=once)
    return pl.pallas_call(
        functools.partial(_peer_dense_kernel, rows_per_block=eb // PEER_N_KEYS, n_blocks=nblk),
        grid=(t // tb, nblk),
        in_specs=[pl.BlockSpec((tb, d), lambda j, i: (j, 0), pipeline_mode=once),
                  pl.BlockSpec((eb, d), lambda j, i: (i, 0)),
                  pl.BlockSpec((d, eb), lambda j, i: (0, i)),
                  tspec, tspec,
                  pl.BlockSpec((tb, d), lambda j, i: (j, 0), pipeline_mode=once),
                  pl.BlockSpec((1, 1, d), lambda j, i: (group_of_tile(j, tb), 0, 0))],
        out_specs=pl.BlockSpec((tb, d), lambda j, i: (j, 0)),
        out_shape=jax.ShapeDtypeStruct((t, d), F32),
        scratch_shapes=[pltpu.VMEM((d, tb), F32), pltpu.VMEM((d, tb), BF16)],
        compiler_params=_cparams(("parallel", "arbitrary")),
        name="peer_dense",
    )(u, w_down, w_up_t, nc, re, x, gates)


def peer_residual(x, u, layer, w_q, sub_keys, w_down, w_up, gates, group_of_tile, *, tb=512, eb=1024):
    q = proj(u, w_q, layer=layer, tm=1024, tn=512, name="peer_q")
    nc, re = peer_route(q, sub_keys[layer], tb=256)
    return peer_dense(u, cast_layer_bf16(w_down, layer), transpose_layer_bf16(w_up, layer), nc, re,
                      x, gates, group_of_tile, tb=tb, eb=eb)


def _rope_tables(n_prompt_rows, n_seq, seq_len):
    n_rows = seq_len // GRID_W
    rows = jnp.repeat(jnp.arange(n_rows), GRID_W).astype(F32)
    cols = jnp.tile(jnp.arange(GRID_W), n_rows).astype(F32)
    n_freq = ROPE_AXIS_DIM // 2
    inv = ROPE_THETA ** (-jnp.arange(n_freq, dtype=F32) / n_freq)
    ar, ac = rows[:, None] * inv, cols[:, None] * inv
    cos_t = jnp.concatenate([jnp.cos(ar), jnp.cos(ar), jnp.cos(ac), jnp.cos(ac)], axis=-1)
    sin_t = jnp.concatenate([-jnp.sin(ar), jnp.sin(ar), -jnp.sin(ac), jnp.sin(ac)], axis=-1)
    cos_t = jnp.concatenate([jnp.ones((n_prompt_rows, HEAD_DIM), F32)] + [cos_t] * n_seq, axis=0)
    sin_t = jnp.concatenate([jnp.zeros((n_prompt_rows, HEAD_DIM), F32)] + [sin_t] * n_seq, axis=0)
    return cos_t, sin_t


def _qk_prep_kernel(pq_ref, pk_ref, qg_ref, kg_ref, cos_ref, sin_ref, q_ref, k_ref):
    cos_t = cos_ref[...]
    sin_t = sin_ref[...]
    lane = lax.broadcasted_iota(jnp.int32, cos_t.shape, 1)
    first = (lane % (ROPE_AXIS_DIM)) < (ROPE_AXIS_DIM // 2)
    quarter = ROPE_AXIS_DIM // 2

    def norm_rope(x, gain):
        y = x * lax.rsqrt(jnp.mean(x * x, axis=-1, keepdims=True) + EPS) * gain
        swapped = jnp.where(first, pltpu.roll(y, HEAD_DIM - quarter, 1), pltpu.roll(y, quarter, 1))
        return y * cos_t + swapped * sin_t

    for h in range(N_HEADS):
        hs = slice(h * HEAD_DIM, (h + 1) * HEAD_DIM)
        q_ref[:, hs] = norm_rope(pq_ref[:, hs], qg_ref[...]).astype(q_ref.dtype)
    for h in range(KV_HEADS):
        hs = slice(h * HEAD_DIM, (h + 1) * HEAD_DIM)
        k_ref[:, hs] = norm_rope(pk_ref[:, hs], kg_ref[...])


def qk_prep(p, q_gain, k_gain, cos_t, sin_t, *, tm):
    t = p.shape[0]
    tm = min(tm, t)
    assert FOURIER_WIDTH % ATTN_WIDTH == 0 and (FOURIER_WIDTH + ATTN_WIDTH) % KV_WIDTH == 0
    q_blk, k_blk = FOURIER_WIDTH // ATTN_WIDTH, (FOURIER_WIDTH + ATTN_WIDTH) // KV_WIDTH
    rspec = pl.BlockSpec((tm, HEAD_DIM), lambda i: (i, 0))
    gspec = pl.BlockSpec((1, HEAD_DIM), lambda i: (0, 0))
    return pl.pallas_call(
        _qk_prep_kernel,
        grid=(t // tm,),
        in_specs=[pl.BlockSpec((tm, ATTN_WIDTH), lambda i: (i, q_blk)),
                  pl.BlockSpec((tm, KV_WIDTH), lambda i: (i, k_blk)), gspec, gspec, rspec, rspec],
        out_specs=[pl.BlockSpec((tm, ATTN_WIDTH), lambda i: (i, 0)), pl.BlockSpec((tm, KV_WIDTH), lambda i: (i, 0))],
        out_shape=[jax.ShapeDtypeStruct((t, ATTN_WIDTH), BF16), jax.ShapeDtypeStruct((t, KV_WIDTH), F32)],
        compiler_params=_cparams(("parallel",)),
        name="qk_prep",
    )(p, p, q_gain.reshape(1, HEAD_DIM), k_gain.reshape(1, HEAD_DIM), cos_t, sin_t)


def kernel(x_prompt, x_sample, cache_k, cache_v, state_fwd, state_bwd, c, c_ctx, ada_w, ada_b, norm1, norm2,
           af_w_in, af_q_norm, af_k_norm, af_w_out, dn_w_in, dn_conv_w, dn_a_log, dn_dt_bias, dn_o_norm,
           dn_w_out, peer_w_q, peer_sub_keys, peer_w_down, peer_w_up, final_norm):
    nb, seq, d = x_prompt.shape
    db, dseq, _ = x_sample.shape
    depth = ada_w.shape[0]
    tp = nb * seq
    ts = db * dseq
    tt = tp + ts
    tm = math.gcd(math.gcd(tp, dseq), 1024)

    def group_of_tile(i, tile):
        r = i * tile
        return jnp.where(r < tp, 0, 1 + (r - tp) // dseq)

    conds = jnp.concatenate([c_ctx[None], c, jnp.zeros((8 - 1 - db, d), F32)], axis=0)
    mods = ada_all(conds, ada_w, ada_b)
    mods = mods.reshape(depth, 8, 6, d)

    x = jnp.concatenate([x_prompt.reshape(tp, d), x_sample.reshape(ts, d)], axis=0)
    new_k, new_v, new_sf, new_sb = [], [], [], []
    for i in range(depth):
        j = i // 2
        md = [mods[i, :, n][:, None, :] for n in range(6)]
        sh1, sc1, g1, sh2, sc2, g2 = md
        u = modulate(x, norm1[i], sh1, sc1, group_of_tile, tm=tm)
        if i % 2 == 0:
            p = proj(u, af_w_in, layer=j, tm=tm, tn=512, name="af_in")
            cos_t, sin_t = _rope_tables(tp, db, dseq)
            q, k = qk_prep(p, af_q_norm[j], af_k_norm[j], cos_t, sin_t, tm=tm)
            v = p[:, FOURIER_WIDTH + ATTN_WIDTH + KV_WIDTH:]
            new_k.append(k[:tp].reshape(nb, seq, KV_HEADS, HEAD_DIM))
            new_v.append(v[:tp].reshape(nb, seq, KV_HEADS, HEAD_DIM))
            keys = jnp.concatenate([k[tp:].reshape(db, dseq, KV_WIDTH),
                                    cache_k[:, j].reshape(db, -1, KV_WIDTH)], axis=1)
            vals = jnp.concatenate([v[tp:].reshape(db, dseq, KV_WIDTH),
                                    cache_v[:, j].reshape(db, -1, KV_WIDTH)], axis=1)
            att = attention(q[:tp].reshape(nb, seq, ATTN_WIDTH), k[:tp].reshape(nb, seq, KV_WIDTH),
                            v[:tp].reshape(nb, seq, KV_WIDTH), total_rows=tt, row0=0)
            att = attention(q[tp:].reshape(db, dseq, ATTN_WIDTH), keys, vals, total_rows=tt, row0=tp, prev_out=att)
            r = chan_dft(p, tm=tm)
            fou = seq_dft(r, n_seq=nb, seq_len=seq, row0=0)
            fou = seq_dft(r, n_seq=db, seq_len=dseq, row0=tp, prev_out=fou)
            x = proj_residual([fou, att], af_w_out, x, g1, group_of_tile, layer=j, tm=tm, tn=512, name="af_out")
        else:
            p = proj(u, dn_w_in, layer=j, tm=2 * tm, tn=512, ncols=DN_CONV_CH + DN_V_WIDTH, out_dtype=BF16,
                     name="dn_in")
            ba = proj(u, dn_w_in, layer=j, tm=tm, tn=128, col0=DN_CONV_CH + DN_V_WIDTH, name="dn_in_ba")
            beta_all = jax.nn.sigmoid(ba.reshape(tt, 2, 2, DN_V_HEADS)[:, 0])
            g_all = -jnp.exp(dn_a_log[j]) * jax.nn.softplus(ba.reshape(tt, 2, 2, DN_V_HEADS)[:, 1] + dn_dt_bias[j])
            qkv = dn_prep(p, dn_conv_w[j], n_seq=nb, seq_len=seq, row0=0)
            qkv = dn_prep(p, dn_conv_w[j], n_seq=db, seq_len=dseq, row0=tp, prev_out=qkv)
            zeros = jnp.zeros((nb, DN_V_HEADS, DN_K_DIM, DN_V_DIM), F32)
            on, s_f, s_b = delta_mixer(qkv, p, beta_all[:tp].reshape(nb, seq, 2, DN_V_HEADS),
                                       g_all[:tp].reshape(nb, seq, 2, DN_V_HEADS), dn_o_norm[j], zeros, zeros,
                                       n_seq=nb, seq_len=seq, row0=0)
            new_sf.append(s_f)
            new_sb.append(s_b)
            on, _, _ = delta_mixer(qkv, p, beta_all[tp:].reshape(db, dseq, 2, DN_V_HEADS),
                                   g_all[tp:].reshape(db, dseq, 2, DN_V_HEADS), dn_o_norm[j],
                                   state_fwd[:, j], state_bwd[:, j], n_seq=db, seq_len=dseq, row0=tp, prev_out=on)
            x = proj_residual([on], dn_w_out, x, g1, group_of_tile, layer=j, tm=tm, tn=256, name="dn_out")
        u2 = modulate(x, norm2[i], sh2, sc2, group_of_tile, tm=tm)
        x = peer_residual(x, u2, i, peer_w_q, peer_sub_keys, peer_w_down, peer_w_up, g2, group_of_tile)
    return (rmsnorm_rows(x, final_norm, row0=0, nrows=tp, tm=tm).reshape(nb, seq, d),
            rmsnorm_rows(x, final_norm, row0=tp, nrows=ts, tm=tm).reshape(db, dseq, d),
            jnp.stack(new_k, axis=1), jnp.stack(new_v, axis=1),
            jnp.stack(new_sf, axis=1), jnp.stack(new_sb, axis=1))
```

```python
import functools
import math

import jax
import jax.numpy as jnp
import numpy as np
from jax import lax
from jax.experimental import pallas as pl
from jax.experimental.pallas import tpu as pltpu

F32 = jnp.float32
BF16 = jnp.bfloat16

EPS = 1e-6
HEAD_DIM = 128
N_HEADS = 8
KV_HEADS = 2
GQA_GROUP = N_HEADS // KV_HEADS
FOURIER_GROUPS = 4
FOURIER_GROUP_DIM = 256
FOURIER_WIDTH = FOURIER_GROUPS * FOURIER_GROUP_DIM
ATTN_WIDTH = N_HEADS * HEAD_DIM
KV_WIDTH = KV_HEADS * HEAD_DIM
GRID_W = 64
ROPE_THETA = 10000.0
ROPE_AXIS_DIM = HEAD_DIM // 2

DN_QK_HEADS = 16
DN_V_HEADS = 32
DN_K_DIM = 128
DN_V_DIM = 128
DN_QK_WIDTH = DN_QK_HEADS * DN_K_DIM
DN_V_WIDTH = DN_V_HEADS * DN_V_DIM
DN_CONV_CH = 2 * DN_QK_WIDTH + DN_V_WIDTH
DN_CHUNK = 64

PEER_HEADS = 8
PEER_N_KEYS = 128
PEER_HALF = 128
PEER_TOPK = 16
PEER_COL_CHUNK = 256
PACK = 16

VMEM_LIMIT = 56 * 1024 * 1024


def _cparams(sem):
    return pltpu.CompilerParams(dimension_semantics=sem, vmem_limit_bytes=VMEM_LIMIT)


def _bdot(a, b):
    return jnp.dot(a.astype(BF16), b.astype(BF16), preferred_element_type=F32)


def _bdot_nt(a, b):
    return lax.dot_general(a.astype(BF16), b.astype(BF16), (((1,), (1,)), ((), ())),
                           preferred_element_type=F32)


def _bmm(a, b):
    return lax.dot_general(a.astype(BF16), b.astype(BF16), (((2,), (1,)), ((0,), (0,))),
                           preferred_element_type=F32)


def _bmm_nt(a, b):
    return lax.dot_general(a.astype(BF16), b.astype(BF16), (((2,), (2,)), ((0,), (0,))),
                           preferred_element_type=F32)


def _proj_kernel(x_ref, w_ref, o_ref):
    o_ref[...] = _bdot(x_ref[...], w_ref[0]).astype(o_ref.dtype)


def proj(x, w, *, layer=0, tm, tn, col0=0, ncols=None, out_dtype=F32, name="proj"):
    m, k = x.shape
    ncols = w.shape[2] - col0 if ncols is None else ncols
    tm = min(tm, m)
    tn = min(tn, ncols)
    assert m % tm == 0 and ncols % tn == 0 and col0 % tn == 0
    cb = col0 // tn
    return pl.pallas_call(
        _proj_kernel,
        grid=(m // tm, ncols // tn),
        in_specs=[pl.BlockSpec((tm, k), lambda i, j: (i, 0)),
                  pl.BlockSpec((1, k, tn), lambda i, j: (layer, 0, j + cb))],
        out_specs=pl.BlockSpec((tm, tn), lambda i, j: (i, j)),
        out_shape=jax.ShapeDtypeStruct((m, ncols), out_dtype),
        compiler_params=_cparams(("parallel", "arbitrary")),
        name=name,
    )(x, w)


def _proj_res_kernel(*refs, n_parts):
    x_refs, w_refs = refs[:n_parts], refs[n_parts:2 * n_parts]
    res_ref, gate_ref, o_ref = refs[2 * n_parts:]
    acc = _bdot(x_refs[0][...], w_refs[0][0])
    for x_ref, w_ref in zip(x_refs[1:], w_refs[1:]):
        acc = acc + _bdot(x_ref[...], w_ref[0])
    o_ref[...] = res_ref[...] + gate_ref[0] * acc


def proj_residual(xs, w, res, gates, group_of_tile, *, layer=0, tm, tn, name="proj_res"):
    m = xs[0].shape[0]
    n = w.shape[2]
    tm = min(tm, m)
    tn = min(tn, n)
    kp = xs[0].shape[1]
    assert m % tm == 0 and n % tn == 0 and all(x.shape == (m, kp) for x in xs)
    x_specs = [pl.BlockSpec((tm, kp), lambda i, j: (i, 0)) for _ in xs]
    w_specs = [pl.BlockSpec((1, kp, tn), functools.partial(lambda i, j, part: (layer, part, j), part=part))
               for part in range(len(xs))]
    return pl.pallas_call(
        functools.partial(_proj_res_kernel, n_parts=len(xs)),
        grid=(m // tm, n // tn),
        in_specs=x_specs + w_specs + [pl.BlockSpec((tm, tn), lambda i, j: (i, j)),
                                      pl.BlockSpec((1, 1, tn), lambda i, j: (group_of_tile(i, tm), 0, j))],
        out_specs=pl.BlockSpec((tm, tn), lambda i, j: (i, j)),
        out_shape=jax.ShapeDtypeStruct((m, n), F32),
        compiler_params=_cparams(("parallel", "arbitrary")),
        name=name,
    )(*xs, *([w] * len(xs)), res, gates)


def _modulate_kernel(x_ref, gain_ref, shift_ref, scale_ref, o_ref):
    x = x_ref[...]
    y = x * lax.rsqrt(jnp.mean(x * x, axis=-1, keepdims=True) + EPS)
    o_ref[...] = (y * gain_ref[...] * (1.0 + scale_ref[0]) + shift_ref[0]).astype(o_ref.dtype)


def modulate(x, gain, shift, scale, group_of_tile, *, tm, out_dtype=BF16):
    m, d = x.shape
    tm = min(tm, m)
    gspec = pl.BlockSpec((1, 1, d), lambda i: (group_of_tile(i, tm), 0, 0))
    return pl.pallas_call(
        _modulate_kernel,
        grid=(m // tm,),
        in_specs=[pl.BlockSpec((tm, d), lambda i: (i, 0)),
                  pl.BlockSpec((1, d), lambda i: (0, 0)), gspec, gspec],
        out_specs=pl.BlockSpec((tm, d), lambda i: (i, 0)),
        out_shape=jax.ShapeDtypeStruct((m, d), out_dtype),
        compiler_params=_cparams(("parallel",)),
        name="modulate",
    )(x, gain.reshape(1, d), shift, scale)


def _rmsnorm_kernel(x_ref, gain_ref, o_ref):
    x = x_ref[...]
    o_ref[...] = x * lax.rsqrt(jnp.mean(x * x, axis=-1, keepdims=True) + EPS) * gain_ref[...]


def rmsnorm_rows(x, gain, *, row0, nrows, tm):
    d = x.shape[1]
    tm = min(tm, nrows)
    assert row0 % tm == 0 and nrows % tm == 0
    rb0 = row0 // tm
    return pl.pallas_call(
        _rmsnorm_kernel,
        grid=(nrows // tm,),
        in_specs=[pl.BlockSpec((tm, d), lambda i: (rb0 + i, 0)), pl.BlockSpec((1, d), lambda i: (0, 0))],
        out_specs=pl.BlockSpec((tm, d), lambda i: (i, 0)),
        out_shape=jax.ShapeDtypeStruct((nrows, d), F32),
        compiler_params=_cparams(("parallel",)),
        name="final_norm",
    )(x, gain.reshape(1, d))


def _ada_kernel(c_ref, w_ref, b_ref, o_ref):
    c = c_ref[...]
    o_ref[0] = _bdot(c * jax.nn.sigmoid(c), w_ref[0]) + b_ref[0]


def ada_all(conds, ada_w, ada_b, *, tn=1024):
    depth, d, n = ada_w.shape
    r = conds.shape[0]
    return pl.pallas_call(
        _ada_kernel,
        grid=(depth, n // tn),
        in_specs=[pl.BlockSpec((r, d), lambda l, j: (0, 0)),
                  pl.BlockSpec((1, d, tn), lambda l, j: (l, 0, j)),
                  pl.BlockSpec((1, 1, tn), lambda l, j: (l, 0, j))],
        out_specs=pl.BlockSpec((1, r, tn), lambda l, j: (l, 0, j)),
        out_shape=jax.ShapeDtypeStruct((depth, r, n), F32),
        compiler_params=_cparams(("parallel", "arbitrary")),
        name="ada",
    )(conds, ada_w, ada_b.reshape(depth, 1, n))


def _dft_tables(n, scale):
    idx = np.arange(n, dtype=np.int64)
    ang = 2.0 * np.pi * ((idx[:, None] * idx[None, :]) % n).astype(np.float64) / n
    return np.stack([np.cos(ang) * scale, np.sin(ang) * scale]).astype(np.float32)


def _chan_dft_kernel(x_ref, t_ref, o_ref):
    o_ref[0] = _bdot(x_ref[...], t_ref[0]).astype(o_ref.dtype)


def _seq_dft_kernel(t_ref, r_ref, *rest):
    o_ref = rest[-1]
    o_ref[...] = (_bdot(t_ref[0], r_ref[0]) - _bdot(t_ref[1], r_ref[1])).astype(o_ref.dtype)


def chan_dft(p, *, tm):
    t = p.shape[0]
    gd = FOURIER_GROUP_DIM
    tm = min(tm, t)
    ctab = jnp.asarray(_dft_tables(gd, gd ** -0.5), BF16)
    return pl.pallas_call(
        _chan_dft_kernel,
        grid=(t // tm, FOURIER_GROUPS, 2),
        in_specs=[pl.BlockSpec((tm, gd), lambda i, g, s: (i, g)),
                  pl.BlockSpec((1, gd, gd), lambda i, g, s: (s, 0, 0))],
        out_specs=pl.BlockSpec((1, tm, gd), lambda i, g, s: (s, i, g)),
        out_shape=jax.ShapeDtypeStruct((2, t, FOURIER_WIDTH), BF16),
        compiler_params=_cparams(("parallel", "arbitrary", "arbitrary")),
        name="chan_dft",
    )(p, ctab)


def seq_dft(r, *, n_seq, seq_len, row0, prev_out=None, out_dtype=BF16):
    t = r.shape[1]
    assert row0 % seq_len == 0
    stab = jnp.asarray(_dft_tables(seq_len, seq_len ** -0.5), BF16)
    ts = min(512, seq_len)
    tn = 512
    nrow = seq_len // ts
    sb0 = row0 // seq_len
    rb0 = row0 // ts
    in_specs = [pl.BlockSpec((2, ts, seq_len), lambda b, i, j: (0, i, 0)),
                pl.BlockSpec((2, seq_len, tn), lambda b, i, j: (0, sb0 + b, j))]
    args = [stab, r]
    aliases = {}
    if prev_out is not None:
        in_specs.append(pl.BlockSpec(memory_space=pl.ANY))
        args.append(prev_out)
        aliases = {2: 0}
    return pl.pallas_call(
        _seq_dft_kernel,
        grid=(n_seq, nrow, FOURIER_WIDTH // tn),
        in_specs=in_specs,
        out_specs=pl.BlockSpec((ts, tn), lambda b, i, j: (rb0 + b * nrow + i, j)),
        out_shape=jax.ShapeDtypeStruct((t, FOURIER_WIDTH), out_dtype),
        input_output_aliases=aliases,
        compiler_params=_cparams(("parallel", "arbitrary", "arbitrary")),
        name="seq_dft",
    )(*args)


def _attn_kernel(q_ref, k_ref, v_ref, *rest):
    o_ref = rest[-1]
    k = k_ref[0].astype(BF16)
    v = v_ref[0].astype(BF16)
    scale = HEAD_DIM ** -0.5
    for g in range(GQA_GROUP):
        q = q_ref[0, :, g * HEAD_DIM:(g + 1) * HEAD_DIM]
        s = _bdot_nt(q, k)
        e = jnp.exp((s - jnp.max(s, axis=-1, keepdims=True)) * scale)
        o = _bdot(e, v) / jnp.sum(e, axis=-1, keepdims=True)
        o_ref[:, g * HEAD_DIM:(g + 1) * HEAD_DIM] = o.astype(o_ref.dtype)


def attention(q, k, v, *, total_rows, row0, prev_out=None, tq=256, out_dtype=BF16):
    b, sq, _ = q.shape
    sk = k.shape[1]
    tq = min(tq, sq)
    assert row0 % tq == 0
    gw = GQA_GROUP * HEAD_DIM
    nq = sq // tq
    rb0 = row0 // tq
    in_specs = [pl.BlockSpec((1, tq, gw), lambda bb, h, i: (bb, i, h)),
                pl.BlockSpec((1, sk, HEAD_DIM), lambda bb, h, i: (bb, 0, h)),
                pl.BlockSpec((1, sk, HEAD_DIM), lambda bb, h, i: (bb, 0, h))]
    args = [q, k, v]
    aliases = {}
    if prev_out is not None:
        in_specs.append(pl.BlockSpec(memory_space=pl.ANY))
        args.append(prev_out)
        aliases = {3: 0}
    return pl.pallas_call(
        _attn_kernel,
        grid=(b, KV_HEADS, nq),
        in_specs=in_specs,
        out_specs=pl.BlockSpec((tq, gw), lambda bb, h, i: (rb0 + bb * nq + i, h)),
        out_shape=jax.ShapeDtypeStruct((total_rows, ATTN_WIDTH), out_dtype),
        input_output_aliases=aliases,
        compiler_params=_cparams(("parallel", "parallel", "arbitrary")),
        name="attention",
    )(*args)


N_PROB = 4
DELTA_VMEM_BYTES_PER_ROW_HEAD = 14 * 1024


def _tri_masks(c):
    ri = lax.broadcasted_iota(jnp.int32, (c, c), 0)
    ci = lax.broadcasted_iota(jnp.int32, (c, c), 1)
    return ri, ci


def _unit_tri_inverse(low, ri, ci):
    c = low.shape[-1]

    def same_block(size):
        sh = int(math.log2(size))
        return (ri >> sh) == (ci >> sh)

    leaf = 4
    dg = jnp.where(same_block(leaf), low, 0.0)
    imd = jnp.where(ri == ci, 1.0, 0.0) - dg
    inv = imd + _bmm(imd, _bmm(dg, dg))
    size = leaf
    while size < c:
        off = jnp.where(same_block(2 * size) & jnp.logical_not(same_block(size)), low, 0.0)
        inv = inv - _bmm(inv, _bmm(off, inv))
        size *= 2
    return inv


def _delta_kernel(*refs, n_chunks, unroll_a, kh_step, has_prev):
    n_in = 9 + int(has_prev)
    q_ref, k_ref, v_ref, z_ref, gain_ref, cols_ref, rows_ref, sf0_ref, sb0_ref = refs[:9]
    on_ref, sf_ref, sb_ref, u_s, wq_s, akt_s, gl_s, o_s = refs[n_in:]
    c = DN_CHUNK
    dk = DN_K_DIM
    dv = DN_V_DIM
    ri, ci = _tri_masks(c)

    ca = unroll_a
    n_chain = kh_step * N_PROB

    def vhead(pp):
        return 2 * (pp // N_PROB) + pp % 2

    def body_a(hq, g0, carry):
        rows = pl.ds(pl.multiple_of(g0 * (ca * c), ca * c), ca * c)
        chs = pl.ds(g0 * ca, ca)
        k = k_ref[rows, hq * dk:(hq + 1) * dk].astype(F32).reshape(ca, c, dk)
        q = q_ref[rows, hq * dk:(hq + 1) * dk].astype(F32).reshape(ca, c, dk)
        kk = _bmm_nt(k, k)
        qk = _bmm_nt(q, k)
        cols = cols_ref[0, hq, rows, :].reshape(ca, c, 2 * N_PROB)
        rws = rows_ref[0, hq, chs]
        lows, rhss, decays, egs, glasts, gcols = [], [], [], [], [], []
        for p in range(N_PROB):
            head, reverse = p % 2, p >= 2
            beta = cols[:, :, p:p + 1]
            gcol = cols[:, :, N_PROB + p:N_PROB + p + 1]
            grow = rws[:, p:p + 1, :]
            glast = grow[:, :, 0:1] if reverse else grow[:, :, c - 1:c]
            incl = (ri <= ci) if reverse else (ri >= ci)
            strict = (ri < ci) if reverse else (ri > ci)
            decay = jnp.where(incl, jnp.exp(jnp.where(incl, gcol - grow, 0.0)), 0.0)
            eg = jnp.exp(gcol)
            vh = 2 * hq + head
            v = v_ref[rows, vh * dv:(vh + 1) * dv].astype(F32).reshape(ca, c, dv)
            lows.append(jnp.where(strict, kk * beta * decay, 0.0))
            rhss.append(jnp.concatenate([v * beta, k * (beta * eg)], axis=-1))
            decays.append(decay)
            egs.append(eg)
            glasts.append(glast)
            gcols.append(gcol)
        inv = _unit_tri_inverse(jnp.concatenate(lows, axis=0), ri, ci)
        sol = _bmm(inv, jnp.concatenate(rhss, axis=0))
        for p in range(N_PROB):
            pp = hq * N_PROB + p
            sp = sol[p * ca:(p + 1) * ca]
            u_s[pp, rows, :] = sp[:, :, :dv].reshape(ca * c, dv)
            wq_s[pp, chs, 0:c, :] = sp[:, :, dv:].astype(BF16)
            wq_s[pp, chs, c:2 * c, :] = (q * egs[p]).astype(BF16)
            akt_s[pp, chs, 0:c, :] = (qk * decays[p]).astype(BF16)
            akt_s[pp, chs, c:c + dk, :] = jnp.swapaxes(k * jnp.exp(glasts[p] - gcols[p]), 1, 2).astype(BF16)
            gl_s[pp, chs] = jnp.broadcast_to(jnp.exp(glasts[p]), (ca, 8, dv))
        return carry

    for hq in range(kh_step):
        lax.fori_loop(0, n_chunks // ca, functools.partial(body_a, hq), 0)
    o_s[...] = jnp.zeros_like(o_s)

    def body_b(i, states):
        chs = [(n_chunks - 1 - i) if pp % N_PROB >= 2 else i for pp in range(n_chain)]
        rows = [pl.ds(pl.multiple_of(ch * c, c), c) for ch in chs]
        ws = [_bdot(wq_s[pp, chs[pp]], states[pp]) for pp in range(n_chain)]
        vn = [u_s[pp, rows[pp], :] - ws[pp][0:c] for pp in range(n_chain)]
        av = [_bdot(akt_s[pp, chs[pp]], vn[pp]) for pp in range(n_chain)]
        new_states = []
        for pp in range(n_chain):
            head = vhead(pp)
            o_s[rows[pp], head * dv:(head + 1) * dv] += ws[pp][c:2 * c] + av[pp][0:c]
            new_states.append(states[pp] * gl_s[pp, chs[pp]][0:1, :] + av[pp][c:c + dk])
        return tuple(new_states)

    init = tuple((sb0_ref if pp % N_PROB >= 2 else sf0_ref)[0, vhead(pp)] for pp in range(n_chain))
    fin = lax.fori_loop(0, n_chunks, body_b, init)
    for pp in range(n_chain):
        (sb_ref if pp % N_PROB >= 2 else sf_ref)[0, vhead(pp)] = fin[pp]
    for head in range(2 * kh_step):
        hs = slice(head * dv, (head + 1) * dv)
        o = o_s[:, hs]
        z = z_ref[:, hs].astype(F32)
        on = o * lax.rsqrt(jnp.mean(o * o, axis=-1, keepdims=True) + EPS) * gain_ref[...]
        on_ref[:, hs] = (on * (z * jax.nn.sigmoid(z))).astype(on_ref.dtype)


def delta_mixer(qkv, p, beta, g, gain, s_f0, s_b0, *, n_seq, seq_len, row0, prev_out=None):
    b, t = n_seq, seq_len
    tt = qkv.shape[0]
    h = DN_V_HEADS
    kh = DN_QK_HEADS
    rep = h // kh
    assert rep == 2 and row0 % t == 0
    rb0 = row0 // t
    c = DN_CHUNK
    nc = t // c
    gch = g.reshape(b, nc, c, 2, h)
    gcf = jnp.cumsum(gch[:, :, :, 0], axis=2).reshape(b, t, h)
    gcb = jnp.flip(jnp.cumsum(jnp.flip(gch[:, :, :, 1], axis=2), axis=2), axis=2).reshape(b, t, h)
    cols = jnp.stack([beta[:, :, 0], beta[:, :, 1], gcf, gcb], axis=2).reshape(b, t, 4, kh, rep)
    cols = jnp.transpose(cols, (0, 3, 1, 2, 4)).reshape(b, kh, t, 4 * rep)
    rows = jnp.stack([gcf, gcb], axis=2).reshape(b, nc, c, 2, kh, rep)
    rows = jnp.transpose(rows, (0, 4, 1, 3, 5, 2)).reshape(b, kh, nc, N_PROB, c)
    kh_step = 2 if t * 2 * DELTA_VMEM_BYTES_PER_ROW_HEAD <= VMEM_LIMIT // 2 else 1
    stspec = pl.BlockSpec((1, rep * kh_step, DN_K_DIM, DN_V_DIM), lambda bb, hh: (bb, hh, 0, 0))
    unroll_a = math.gcd(nc, 8)
    kw = kh_step * DN_K_DIM
    vw = kh_step * rep * DN_V_DIM
    k_blk0 = DN_QK_WIDTH // kw
    v_blk0 = 2 * DN_QK_WIDTH // vw
    z_blk0 = DN_CONV_CH // vw
    in_specs = [pl.BlockSpec((t, kw), lambda bb, hh: (rb0 + bb, hh)),
                pl.BlockSpec((t, kw), lambda bb, hh: (rb0 + bb, k_blk0 + hh)),
                pl.BlockSpec((t, vw), lambda bb, hh: (rb0 + bb, v_blk0 + hh)),
                pl.BlockSpec((t, vw), lambda bb, hh: (rb0 + bb, z_blk0 + hh)),
                pl.BlockSpec((1, DN_V_DIM), lambda bb, hh: (0, 0)),
                pl.BlockSpec((1, kh_step, t, 4 * rep), lambda bb, hh: (bb, hh, 0, 0)),
                pl.BlockSpec((1, kh_step, nc, N_PROB, c), lambda bb, hh: (bb, hh, 0, 0, 0)),
                stspec, stspec]
    args = [qkv, qkv, qkv, p, gain.reshape(1, DN_V_DIM), cols, rows, s_f0, s_b0]
    aliases = {}
    if prev_out is not None:
        in_specs.append(pl.BlockSpec(memory_space=pl.ANY))
        args.append(prev_out)
        aliases = {len(args) - 1: 0}
    on, sf, sb = pl.pallas_call(
        functools.partial(_delta_kernel, n_chunks=nc, unroll_a=unroll_a, kh_step=kh_step,
                          has_prev=prev_out is not None),
        grid=(b, kh // kh_step),
        in_specs=in_specs,
        out_specs=[pl.BlockSpec((t, vw), lambda bb, hh: (rb0 + bb, hh)), stspec, stspec],
        out_shape=[jax.ShapeDtypeStruct((tt, DN_V_WIDTH), BF16),
                   jax.ShapeDtypeStruct((b, h, DN_K_DIM, DN_V_DIM), F32),
                   jax.ShapeDtypeStruct((b, h, DN_K_DIM, DN_V_DIM), F32)],
        scratch_shapes=[pltpu.VMEM((kh_step * N_PROB, t, DN_V_DIM), F32),
                        pltpu.VMEM((kh_step * N_PROB, nc, 2 * c, DN_K_DIM), BF16),
                        pltpu.VMEM((kh_step * N_PROB, nc, c + DN_K_DIM, c), BF16),
                        pltpu.VMEM((kh_step * N_PROB, nc, 8, DN_V_DIM), F32),
                        pltpu.VMEM((t, vw), F32)],
        input_output_aliases=aliases,
        compiler_params=_cparams(("parallel", "parallel")),
        name="delta_rule",
    )(*args)
    return on, sf, sb


def _dn_prep_kernel(*refs, has_prev):
    p_ref, w_ref = refs[:2]
    o_ref = refs[2 + int(has_prev)]
    j = pl.program_id(1)
    x = p_ref[...].astype(F32)
    s, cb = x.shape
    w = w_ref[...]
    edge = 8
    te = lax.broadcasted_iota(jnp.int32, (edge, cb), 0)

    def conv_silu(xm2, xm1, x0, xp1):
        y = w[0:1] * xm2 + w[1:2] * xm1 + w[2:3] * x0 + w[3:4] * xp1
        return y * jax.nn.sigmoid(y)

    xm2, xm1, xp1 = pltpu.roll(x, 2, 0), pltpu.roll(x, 1, 0), pltpu.roll(x, s - 1, 0)
    y_all = conv_silu(xm2, xm1, x, xp1)
    y_head = conv_silu(jnp.where(te >= 2, xm2[0:edge], 0.0), jnp.where(te >= 1, xm1[0:edge], 0.0),
                       x[0:edge], xp1[0:edge])
    y_tail = conv_silu(xm2[s - edge:], xm1[s - edge:], x[s - edge:], jnp.where(te < edge - 1, xp1[s - edge:], 0.0))
    qk_blocks = DN_QK_WIDTH // cb

    def emit(normalise, scale):
        for rows, y in ((slice(0, s), y_all), (slice(0, edge), y_head), (slice(s - edge, s), y_tail)):
            for hh in range(cb // DN_K_DIM):
                hs = slice(hh * DN_K_DIM, (hh + 1) * DN_K_DIM)
                seg = y[:, hs]
                if normalise:
                    seg = seg * (lax.rsqrt(jnp.sum(seg * seg, axis=-1, keepdims=True) + EPS) * scale)
                o_ref[rows, hs] = seg.astype(o_ref.dtype)

    @pl.when(j < qk_blocks)
    def _():
        emit(True, DN_K_DIM ** -0.5)

    @pl.when(jnp.logical_and(j >= qk_blocks, j < 2 * qk_blocks))
    def _():
        emit(True, 1.0)

    @pl.when(j >= 2 * qk_blocks)
    def _():
        emit(False, 1.0)


def dn_prep(p, conv_w, *, n_seq, seq_len, row0, prev_out=None, cb=512):
    tt = p.shape[0]
    assert row0 % seq_len == 0 and DN_QK_WIDTH % cb == 0
    rb0 = row0 // seq_len
    in_specs = [pl.BlockSpec((seq_len, cb), lambda b, j: (rb0 + b, j)),
                pl.BlockSpec((conv_w.shape[0], cb), lambda b, j: (0, j))]
    args = [p, conv_w]
    aliases = {}
    if prev_out is not None:
        in_specs.append(pl.BlockSpec(memory_space=pl.ANY))
        args.append(prev_out)
        aliases = {2: 0}
    return pl.pallas_call(
        functools.partial(_dn_prep_kernel, has_prev=prev_out is not None),
        grid=(n_seq, DN_CONV_CH // cb),
        in_specs=in_specs,
        out_specs=pl.BlockSpec((seq_len, cb), lambda b, j: (rb0 + b, j)),
        out_shape=jax.ShapeDtypeStruct((tt, DN_CONV_CH), BF16),
        input_output_aliases=aliases,
        compiler_params=_cparams(("parallel", "arbitrary")),
        name="dn_prep",
    )(*args)


def _peer_route_kernel(q_ref, keys_ref, nc_ref, re_ref, v1_ref, v2_ref):
    kk = PEER_TOPK
    neg = -jnp.inf
    for h in range(PEER_HEADS):
        r0 = h * 2 * PEER_HALF
        s1 = _bdot_nt(keys_ref[0], q_ref[:, r0:r0 + PEER_HALF])
        s2 = _bdot_nt(keys_ref[1], q_ref[:, r0 + PEER_HALF:r0 + 2 * PEER_HALF])
        sc = s1
        for r in range(kk):
            m = jnp.max(sc, axis=0, keepdims=True)
            v1_ref[r:r + 1, :] = m
            sc = jnp.where(sc == m, neg, sc)
        sc = s2
        rank2 = jnp.full(s2.shape, float(kk), F32)
        for r in range(kk):
            m = jnp.max(sc, axis=0, keepdims=True)
            v2_ref[r:r + 1, :] = m
            hit = sc == m
            rank2 = jnp.where(hit, float(r), rank2)
            sc = jnp.where(hit, neg, sc)
        v2 = v2_ref[...]
        cand = jnp.concatenate([v1_ref[0:1, :] + v2] + [v1_ref[a:a + 1, :] + v2[0:8] for a in range(1, kk)],
                               axis=0)
        top = cand[0:1, :]
        work = cand
        tau = top
        for r in range(kk):
            tau = jnp.max(work, axis=0, keepdims=True)
            work = jnp.where(work == tau, neg, work)
        sel = cand >= tau
        z = jnp.sum(jnp.where(sel, jnp.exp(cand - top), 0.0), axis=0, keepdims=True)
        one = jnp.where(sel, 1.0, 0.0)
        n1 = jnp.zeros(s1.shape, F32)
        for a in range(kk):
            lo, hi = (0, kk) if a == 0 else (kk + 8 * (a - 1), kk + 8 * a)
            cnt = jnp.sum(one[lo:hi], axis=0, keepdims=True)
            n1 = jnp.where(s1 == v1_ref[a:a + 1, :], cnt, n1)
        nc_ref[0, h] = n1
        nc_ref[1, h] = (0.5 * jnp.exp(s1 - v1_ref[0:1, :])) / z
        re_ref[0, h] = rank2.astype(BF16)
        re_ref[1, h] = jnp.exp(s2 - v2[0:1, :]).astype(BF16)


def peer_route(q, sub_keys, *, tb):
    t, d2 = q.shape
    tb = min(tb, t)
    tspec = pl.BlockSpec((2, PEER_HEADS, PEER_N_KEYS, tb), lambda j: (0, 0, 0, j))
    return pl.pallas_call(
        _peer_route_kernel,
        grid=(t // tb,),
        in_specs=[pl.BlockSpec((tb, d2), lambda j: (j, 0)),
                  pl.BlockSpec((2, PEER_N_KEYS, PEER_HALF), lambda j: (0, 0, 0))],
        out_specs=[tspec, tspec],
        out_shape=[jax.ShapeDtypeStruct((2, PEER_HEADS, PEER_N_KEYS, t), F32),
                   jax.ShapeDtypeStruct((2, PEER_HEADS, PEER_N_KEYS, t), BF16)],
        scratch_shapes=[pltpu.VMEM((PEER_TOPK, tb), F32), pltpu.VMEM((PEER_TOPK, tb), F32)],
        compiler_params=_cparams(("parallel",)),
        name="peer_route",
    )(q, sub_keys)


def _cast_kernel(x_ref, o_ref):
    o_ref[...] = x_ref[0].astype(o_ref.dtype)


def cast_layer_bf16(w, layer, *, tr=1024):
    _, r, c = w.shape
    return pl.pallas_call(
        _cast_kernel,
        grid=(r // tr,),
        in_specs=[pl.BlockSpec((1, tr, c), lambda i: (layer, i, 0))],
        out_specs=pl.BlockSpec((tr, c), lambda i: (i, 0)),
        out_shape=jax.ShapeDtypeStruct((r, c), BF16),
        compiler_params=_cparams(("parallel",)),
        name="cast_bf16",
    )(w)


def _transpose_cast_kernel(x_ref, o_ref):
    o_ref[...] = jnp.transpose(x_ref[0]).astype(o_ref.dtype)


def transpose_layer_bf16(w, layer, *, tr=512):
    _, r, c = w.shape
    return pl.pallas_call(
        _transpose_cast_kernel,
        grid=(r // tr,),
        in_specs=[pl.BlockSpec((1, tr, c), lambda i: (layer, i, 0))],
        out_specs=pl.BlockSpec((c, tr), lambda i: (0, i)),
        out_shape=jax.ShapeDtypeStruct((c, r), BF16),
        compiler_params=_cparams(("parallel",)),
        name="transpose_bf16",
    )(w)


def _peer_dense_kernel(u_ref, wd_ref, wut_ref, nc_ref, re_ref, x_ref, g_ref, o_ref, acc_s, ut_s, *,
                       rows_per_block, n_blocks):
    i = pl.program_id(1)

    @pl.when(i == 0)
    def _():
        acc_s[...] = jnp.zeros_like(acc_s)
        ut_s[...] = jnp.transpose(u_ref[...].astype(F32)).astype(BF16)

    tb = u_ref.shape[0]
    cw = min(PEER_COL_CHUNK, tb)
    e1s = pl.ds(pl.multiple_of(i * rows_per_block, rows_per_block), rows_per_block)
    n1_all = [nc_ref[0, h, e1s, :] for h in range(PEER_HEADS)]
    c1_all = [nc_ref[1, h, e1s, :] for h in range(PEER_HEADS)]
    chunks = [slice(c0, c0 + cw) for c0 in range(0, tb, cw)]
    gate_rows = [[None] * rows_per_block for _ in chunks]
    for r in range(rows_per_block):
        for h in range(PEER_HEADS):
            n1 = jnp.broadcast_to(n1_all[h][r:r + 1, :], (PACK, tb)).astype(BF16)
            c1 = jnp.broadcast_to(c1_all[h][r:r + 1, :], (PACK, tb)).astype(BF16)
            for ic, cs in enumerate(chunks):
                rank2 = re_ref[0, h, :, cs].reshape(PEER_N_KEYS // PACK, PACK, cw)
                e2w = re_ref[1, h, :, cs].reshape(PEER_N_KEYS // PACK, PACK, cw)
                wgt = jnp.where(rank2 < n1[None, :, cs], e2w * c1[None, :, cs], jnp.zeros((), BF16))
                gate_rows[ic][r] = wgt if gate_rows[ic][r] is None else gate_rows[ic][r] + wgt
    hpres = [jnp.dot(wd_ref[...], ut_s[:, cs], preferred_element_type=F32) for cs in chunks]
    for cs, hpre, rows_c in zip(chunks, hpres, gate_rows):
        half_gate = jnp.concatenate([g.reshape(PEER_N_KEYS, cw) for g in rows_c], axis=0)
        act = (hpre * (1.0 + lax.erf(hpre * (2.0 ** -0.5)))).astype(BF16) * half_gate
        acc_s[:, cs] += jnp.dot(wut_ref[...], act, preferred_element_type=F32)

    @pl.when(i == n_blocks - 1)
    def _():
        o_ref[...] = x_ref[...] + g_ref[0] * jnp.transpose(acc_s[...])


def peer_dense(u, w_down, w_up_t, nc, re, x, gates, group_of_tile, *, tb, eb):
    t, d = u.shape
    ne = w_down.shape[0]
    tb = min(tb, t)
    assert eb % PEER_N_KEYS == 0 and t % tb == 0
    nblk = ne // eb
    once = pl.Buffered(1)
    tspec = pl.BlockSpec((2, PEER_HEADS, PEER_N_KEYS, tb), lambda j, i: (0, 0, 0, j), pipeline_mode=once)
    return pl.pallas_call(
        functools.partial(_peer_dense_kernel, rows_per_block=eb // PEER_N_KEYS, n_blocks=nblk),
        grid=(t // tb, nblk),
        in_specs=[pl.BlockSpec((tb, d), lambda j, i: (j, 0), pipeline_mode=once),
                  pl.BlockSpec((eb, d), lambda j, i: (i, 0)),
                  pl.BlockSpec((d, eb), lambda j, i: (0, i)),
                  tspec, tspec,
                  pl.BlockSpec((tb, d), lambda j, i: (j, 0), pipeline_mode=once),
                  pl.BlockSpec((1, 1, d), lambda j, i: (group_of_tile(j, tb), 0, 0))],
        out_specs=pl.BlockSpec((tb, d), lambda j, i: (j, 0)),
        out_shape=jax.ShapeDtypeStruct((t, d), F32),
        scratch_shapes=[pltpu.VMEM((d, tb), F32), pltpu.VMEM((d, tb), BF16)],
        compiler_params=_cparams(("parallel", "arbitrary")),
        name="peer_dense",
    )(u, w_down, w_up_t, nc, re, x, gates)


def peer_residual(x, u, layer, w_q, sub_keys, w_down, w_up, gates, group_of_tile, *, tb=512, eb=1024):
    q = proj(u, w_q, layer=layer, tm=1024, tn=512, name="peer_q")
    nc, re = peer_route(q, sub_keys[layer], tb=256)
    return peer_dense(u, cast_layer_bf16(w_down, layer), transpose_layer_bf16(w_up, layer), nc, re,
                      x, gates, group_of_tile, tb=tb, eb=eb)


def _rope_tables(n_prompt_rows, n_seq, seq_len):
    n_rows = seq_len // GRID_W
    rows = jnp.repeat(jnp.arange(n_rows), GRID_W).astype(F32)
    cols = jnp.tile(jnp.arange(GRID_W), n_rows).astype(F32)
    n_freq = ROPE_AXIS_DIM // 2
    inv = ROPE_THETA ** (-jnp.arange(n_freq, dtype=F32) / n_freq)
    ar, ac = rows[:, None] * inv, cols[:, None] * inv
    cos_t = jnp.concatenate([jnp.cos(ar), jnp.cos(ar), jnp.cos(ac), jnp.cos(ac)], axis=-1)
    sin_t = jnp.concatenate([-jnp.sin(ar), jnp.sin(ar), -jnp.sin(ac), jnp.sin(ac)], axis=-1)
    cos_t = jnp.concatenate([jnp.ones((n_prompt_rows, HEAD_DIM), F32)] + [cos_t] * n_seq, axis=0)
    sin_t = jnp.concatenate([jnp.zeros((n_prompt_rows, HEAD_DIM), F32)] + [sin_t] * n_seq, axis=0)
    return cos_t, sin_t


def _qk_prep_kernel(pq_ref, pk_ref, qg_ref, kg_ref, cos_ref, sin_ref, q_ref, k_ref):
    cos_t = cos_ref[...]
    sin_t = sin_ref[...]
    lane = lax.broadcasted_iota(jnp.int32, cos_t.shape, 1)
    first = (lane % (ROPE_AXIS_DIM)) < (ROPE_AXIS_DIM // 2)
    quarter = ROPE_AXIS_DIM // 2

    def norm_rope(x, gain):
        y = x * lax.rsqrt(jnp.mean(x * x, axis=-1, keepdims=True) + EPS) * gain
        swapped = jnp.where(first, pltpu.roll(y, HEAD_DIM - quarter, 1), pltpu.roll(y, quarter, 1))
        return y * cos_t + swapped * sin_t

    for h in range(N_HEADS):
        hs = slice(h * HEAD_DIM, (h + 1) * HEAD_DIM)
        q_ref[:, hs] = norm_rope(pq_ref[:, hs], qg_ref[...]).astype(q_ref.dtype)
    for h in range(KV_HEADS):
        hs = slice(h * HEAD_DIM, (h + 1) * HEAD_DIM)
        k_ref[:, hs] = norm_rope(pk_ref[:, hs], kg_ref[...])


def qk_prep(p, q_gain, k_gain, cos_t, sin_t, *, tm):
    t = p.shape[0]
    tm = min(tm, t)
    assert FOURIER_WIDTH % ATTN_WIDTH == 0 and (FOURIER_WIDTH + ATTN_WIDTH) % KV_WIDTH == 0
    q_blk, k_blk = FOURIER_WIDTH // ATTN_WIDTH, (FOURIER_WIDTH + ATTN_WIDTH) // KV_WIDTH
    rspec = pl.BlockSpec((tm, HEAD_DIM), lambda i: (i, 0))
    gspec = pl.BlockSpec((1, HEAD_DIM), lambda i: (0, 0))
    return pl.pallas_call(
        _qk_prep_kernel,
        grid=(t // tm,),
        in_specs=[pl.BlockSpec((tm, ATTN_WIDTH), lambda i: (i, q_blk)),
                  pl.BlockSpec((tm, KV_WIDTH), lambda i: (i, k_blk)), gspec, gspec, rspec, rspec],
        out_specs=[pl.BlockSpec((tm, ATTN_WIDTH), lambda i: (i, 0)), pl.BlockSpec((tm, KV_WIDTH), lambda i: (i, 0))],
        out_shape=[jax.ShapeDtypeStruct((t, ATTN_WIDTH), BF16), jax.ShapeDtypeStruct((t, KV_WIDTH), F32)],
        compiler_params=_cparams(("parallel",)),
        name="qk_prep",
    )(p, p, q_gain.reshape(1, HEAD_DIM), k_gain.reshape(1, HEAD_DIM), cos_t, sin_t)


def kernel(x_prompt, x_sample, cache_k, cache_v, state_fwd, state_bwd, c, c_ctx, ada_w, ada_b, norm1, norm2,
           af_w_in, af_q_norm, af_k_norm, af_w_out, dn_w_in, dn_conv_w, dn_a_log, dn_dt_bias, dn_o_norm,
           dn_w_out, peer_w_q, peer_sub_keys, peer_w_down, peer_w_up, final_norm):
    nb, seq, d = x_prompt.shape
    db, dseq, _ = x_sample.shape
    depth = ada_w.shape[0]
    tp = nb * seq
    ts = db * dseq
    tt = tp + ts
    tm = math.gcd(math.gcd(tp, dseq), 1024)

    def group_of_tile(i, tile):
        r = i * tile
        return jnp.where(r < tp, 0, 1 + (r - tp) // dseq)

    conds = jnp.concatenate([c_ctx[None], c, jnp.zeros((8 - 1 - db, d), F32)], axis=0)
    mods = ada_all(conds, ada_w, ada_b)
    mods = mods.reshape(depth, 8, 6, d)

    x = jnp.concatenate([x_prompt.reshape(tp, d), x_sample.reshape(ts, d)], axis=0)
    new_k, new_v, new_sf, new_sb = [], [], [], []
    for i in range(depth):
        j = i // 2
        md = [mods[i, :, n][:, None, :] for n in range(6)]
        sh1, sc1, g1, sh2, sc2, g2 = md
        u = modulate(x, norm1[i], sh1, sc1, group_of_tile, tm=tm)
        if i % 2 == 0:
            p = proj(u, af_w_in, layer=j, tm=tm, tn=512, name="af_in")
            cos_t, sin_t = _rope_tables(tp, db, dseq)
            q, k = qk_prep(p, af_q_norm[j], af_k_norm[j], cos_t, sin_t, tm=tm)
            v = p[:, FOURIER_WIDTH + ATTN_WIDTH + KV_WIDTH:]
            new_k.append(k[:tp].reshape(nb, seq, KV_HEADS, HEAD_DIM))
            new_v.append(v[:tp].reshape(nb, seq, KV_HEADS, HEAD_DIM))
            keys = jnp.concatenate([k[tp:].reshape(db, dseq, KV_WIDTH),
                                    cache_k[:, j].reshape(db, -1, KV_WIDTH)], axis=1)
            vals = jnp.concatenate([v[tp:].reshape(db, dseq, KV_WIDTH),
                                    cache_v[:, j].reshape(db, -1, KV_WIDTH)], axis=1)
            att = attention(q[:tp].reshape(nb, seq, ATTN_WIDTH), k[:tp].reshape(nb, seq, KV_WIDTH),
                            v[:tp].reshape(nb, seq, KV_WIDTH), total_rows=tt, row0=0)
            att = attention(q[tp:].reshape(db, dseq, ATTN_WIDTH), keys, vals, total_rows=tt, row0=tp, prev_out=att)
            r = chan_dft(p, tm=tm)
            fou = seq_dft(r, n_seq=nb, seq_len=seq, row0=0)
            fou = seq_dft(r, n_seq=db, seq_len=dseq, row0=tp, prev_out=fou)
            x = proj_residual([fou, att], af_w_out, x, g1, group_of_tile, layer=j, tm=tm, tn=512, name="af_out")
        else:
            p = proj(u, dn_w_in, layer=j, tm=2 * tm, tn=512, ncols=DN_CONV_CH + DN_V_WIDTH, out_dtype=BF16,
                     name="dn_in")
            ba = proj(u, dn_w_in, layer=j, tm=tm, tn=128, col0=DN_CONV_CH + DN_V_WIDTH, name="dn_in_ba")
            beta_all = jax.nn.sigmoid(ba.reshape(tt, 2, 2, DN_V_HEADS)[:, 0])
            g_all = -jnp.exp(dn_a_log[j]) * jax.nn.softplus(ba.reshape(tt, 2, 2, DN_V_HEADS)[:, 1] + dn_dt_bias[j])
            qkv = dn_prep(p, dn_conv_w[j], n_seq=nb, seq_len=seq, row0=0)
            qkv = dn_prep(p, dn_conv_w[j], n_seq=db, seq_len=dseq, row0=tp, prev_out=qkv)
            zeros = jnp.zeros((nb, DN_V_HEADS, DN_K_DIM, DN_V_DIM), F32)
            on, s_f, s_b = delta_mixer(qkv, p, beta_all[:tp].reshape(nb, seq, 2, DN_V_HEADS),
                                       g_all[:tp].reshape(nb, seq, 2, DN_V_HEADS), dn_o_norm[j], zeros, zeros,
                                       n_seq=nb, seq_len=seq, row0=0)
            new_sf.append(s_f)
            new_sb.append(s_b)
            on, _, _ = delta_mixer(qkv, p, beta_all[tp:].reshape(db, dseq, 2, DN_V_HEADS),
                                   g_all[tp:].reshape(db, dseq, 2, DN_V_HEADS), dn_o_norm[j],
                                   state_fwd[:, j], state_bwd[:, j], n_seq=db, seq_len=dseq, row0=tp, prev_out=on)
            x = proj_residual([on], dn_w_out, x, g1, group_of_tile, layer=j, tm=tm, tn=256, name="dn_out")
        u2 = modulate(x, norm2[i], sh2, sc2, group_of_tile, tm=tm)
        x = peer_residual(x, u2, i, peer_w_q, peer_sub_keys, peer_w_down, peer_w_up, g2, group_of_tile)
    return (rmsnorm_rows(x, final_norm, row0=0, nrows=tp, tm=tm).reshape(nb, seq, d),
            rmsnorm_rows(x, final_norm, row0=tp, nrows=ts, tm=tm).reshape(db, dseq, d),
            jnp.stack(new_k, axis=1), jnp.stack(new_v, axis=1),
            jnp.stack(new_sf, axis=1), jnp.stack(new_sb, axis=1))
```

```python
import functools
import math

import jax
import jax.numpy as jnp
import numpy as np
from jax import lax
from jax.experimental import pallas as pl
from jax.experimental.pallas import tpu as pltpu

F32 = jnp.float32
BF16 = jnp.bfloat16

EPS = 1e-6
HEAD_DIM = 128
N_HEADS = 8
KV_HEADS = 2
GQA_GROUP = N_HEADS // KV_HEADS
FOURIER_GROUPS = 4
FOURIER_GROUP_DIM = 256
FOURIER_WIDTH = FOURIER_GROUPS * FOURIER_GROUP_DIM
ATTN_WIDTH = N_HEADS * HEAD_DIM
KV_WIDTH = KV_HEADS * HEAD_DIM
GRID_W = 64
ROPE_THETA = 10000.0
ROPE_AXIS_DIM = HEAD_DIM // 2

DN_QK_HEADS = 16
DN_V_HEADS = 32
DN_K_DIM = 128
DN_V_DIM = 128
DN_QK_WIDTH = DN_QK_HEADS * DN_K_DIM
DN_V_WIDTH = DN_V_HEADS * DN_V_DIM
DN_CONV_CH = 2 * DN_QK_WIDTH + DN_V_WIDTH
DN_CHUNK = 64

PEER_HEADS = 8
PEER_N_KEYS = 128
PEER_HALF = 128
PEER_TOPK = 16
PEER_COL_CHUNK = 256
PACK = 16

VMEM_LIMIT = 56 * 1024 * 1024


def _cparams(sem):
    return pltpu.CompilerParams(dimension_semantics=sem, vmem_limit_bytes=VMEM_LIMIT)


def _bdot(a, b):
    return jnp.dot(a.astype(BF16), b.astype(BF16), preferred_element_type=F32)


def _bdot_nt(a, b):
    return lax.dot_general(a.astype(BF16), b.astype(BF16), (((1,), (1,)), ((), ())),
                           preferred_element_type=F32)


def _bmm(a, b):
    return lax.dot_general(a.astype(BF16), b.astype(BF16), (((2,), (1,)), ((0,), (0,))),
                           preferred_element_type=F32)


def _bmm_nt(a, b):
    return lax.dot_general(a.astype(BF16), b.astype(BF16), (((2,), (2,)), ((0,), (0,))),
                           preferred_element_type=F32)


def _proj_kernel(x_ref, w_ref, o_ref):
    o_ref[...] = _bdot(x_ref[...], w_ref[0]).astype(o_ref.dtype)


def proj(x, w, *, layer=0, tm, tn, col0=0, ncols=None, out_dtype=F32, name="proj"):
    m, k = x.shape
    ncols = w.shape[2] - col0 if ncols is None else ncols
    tm = min(tm, m)
    tn = min(tn, ncols)
    assert m % tm == 0 and ncols % tn == 0 and col0 % tn == 0
    cb = col0 // tn
    return pl.pallas_call(
        _proj_kernel,
        grid=(m // tm, ncols // tn),
        in_specs=[pl.BlockSpec((tm, k), lambda i, j: (i, 0)),
                  pl.BlockSpec((1, k, tn), lambda i, j: (layer, 0, j + cb))],
        out_specs=pl.BlockSpec((tm, tn), lambda i, j: (i, j)),
        out_shape=jax.ShapeDtypeStruct((m, ncols), out_dtype),
        compiler_params=_cparams(("parallel", "arbitrary")),
        name=name,
    )(x, w)


def _proj_res_kernel(*refs, n_parts):
    x_refs, w_refs = refs[:n_parts], refs[n_parts:2 * n_parts]
    res_ref, gate_ref, o_ref = refs[2 * n_parts:]
    acc = _bdot(x_refs[0][...], w_refs[0][0])
    for x_ref, w_ref in zip(x_refs[1:], w_refs[1:]):
        acc = acc + _bdot(x_ref[...], w_ref[0])
    o_ref[...] = res_ref[...] + gate_ref[0] * acc


def proj_residual(xs, w, res, gates, group_of_tile, *, layer=0, tm, tn, name="proj_res"):
    m = xs[0].shape[0]
    n = w.shape[2]
    tm = min(tm, m)
    tn = min(tn, n)
    kp = xs[0].shape[1]
    assert m % tm == 0 and n % tn == 0 and all(x.shape == (m, kp) for x in xs)
    x_specs = [pl.BlockSpec((tm, kp), lambda i, j: (i, 0)) for _ in xs]
    w_specs = [pl.BlockSpec((1, kp, tn), functools.partial(lambda i, j, part: (layer, part, j), part=part))
               for part in range(len(xs))]
    return pl.pallas_call(
        functools.partial(_proj_res_kernel, n_parts=len(xs)),
        grid=(m // tm, n // tn),
        in_specs=x_specs + w_specs + [pl.BlockSpec((tm, tn), lambda i, j: (i, j)),
                                      pl.BlockSpec((1, 1, tn), lambda i, j: (group_of_tile(i, tm), 0, j))],
        out_specs=pl.BlockSpec((tm, tn), lambda i, j: (i, j)),
        out_shape=jax.ShapeDtypeStruct((m, n), F32),
        compiler_params=_cparams(("parallel", "arbitrary")),
        name=name,
    )(*xs, *([w] * len(xs)), res, gates)


def _modulate_kernel(x_ref, gain_ref, shift_ref, scale_ref, o_ref):
    x = x_ref[...]
    y = x * lax.rsqrt(jnp.mean(x * x, axis=-1, keepdims=True) + EPS)
    o_ref[...] = (y * gain_ref[...] * (1.0 + scale_ref[0]) + shift_ref[0]).astype(o_ref.dtype)


def modulate(x, gain, shift, scale, group_of_tile, *, tm, out_dtype=BF16):
    m, d = x.shape
    tm = min(tm, m)
    gspec = pl.BlockSpec((1, 1, d), lambda i: (group_of_tile(i, tm), 0, 0))
    return pl.pallas_call(
        _modulate_kernel,
        grid=(m // tm,),
        in_specs=[pl.BlockSpec((tm, d), lambda i: (i, 0)),
                  pl.BlockSpec((1, d), lambda i: (0, 0)), gspec, gspec],
        out_specs=pl.BlockSpec((tm, d), lambda i: (i, 0)),
        out_shape=jax.ShapeDtypeStruct((m, d), out_dtype),
        compiler_params=_cparams(("parallel",)),
        name="modulate",
    )(x, gain.reshape(1, d), shift, scale)


def _rmsnorm_kernel(x_ref, gain_ref, o_ref):
    x = x_ref[...]
    o_ref[...] = x * lax.rsqrt(jnp.mean(x * x, axis=-1, keepdims=True) + EPS) * gain_ref[...]


def rmsnorm_rows(x, gain, *, row0, nrows, tm):
    d = x.shape[1]
    tm = min(tm, nrows)
    assert row0 % tm == 0 and nrows % tm == 0
    rb0 = row0 // tm
    return pl.pallas_call(
        _rmsnorm_kernel,
        grid=(nrows // tm,),
        in_specs=[pl.BlockSpec((tm, d), lambda i: (rb0 + i, 0)), pl.BlockSpec((1, d), lambda i: (0, 0))],
        out_specs=pl.BlockSpec((tm, d), lambda i: (i, 0)),
        out_shape=jax.ShapeDtypeStruct((nrows, d), F32),
        compiler_params=_cparams(("parallel",)),
        name="final_norm",
    )(x, gain.reshape(1, d))


def _ada_kernel(c_ref, w_ref, b_ref, o_ref):
    c = c_ref[...]
    o_ref[0] = _bdot(c * jax.nn.sigmoid(c), w_ref[0]) + b_ref[0]


def ada_all(conds, ada_w, ada_b, *, tn=1024):
    depth, d, n = ada_w.shape
    r = conds.shape[0]
    return pl.pallas_call(
        _ada_kernel,
        grid=(depth, n // tn),
        in_specs=[pl.BlockSpec((r, d), lambda l, j: (0, 0)),
                  pl.BlockSpec((1, d, tn), lambda l, j: (l, 0, j)),
                  pl.BlockSpec((1, 1, tn), lambda l, j: (l, 0, j))],
        out_specs=pl.BlockSpec((1, r, tn), lambda l, j: (l, 0, j)),
        out_shape=jax.ShapeDtypeStruct((depth, r, n), F32),
        compiler_params=_cparams(("parallel", "arbitrary")),
        name="ada",
    )(conds, ada_w, ada_b.reshape(depth, 1, n))


def _dft_tables(n, scale):
    idx = np.arange(n, dtype=np.int64)
    ang = 2.0 * np.pi * ((idx[:, None] * idx[None, :]) % n).astype(np.float64) / n
    return np.stack([np.cos(ang) * scale, np.sin(ang) * scale]).astype(np.float32)


def _chan_dft_kernel(x_ref, t_ref, o_ref):
    o_ref[0] = _bdot(x_ref[...], t_ref[0]).astype(o_ref.dtype)


def _seq_dft_kernel(t_ref, r_ref, *rest):
    o_ref = rest[-1]
    o_ref[...] = (_bdot(t_ref[0], r_ref[0]) - _bdot(t_ref[1], r_ref[1])).astype(o_ref.dtype)


def chan_dft(p, *, tm):
    t = p.shape[0]
    gd = FOURIER_GROUP_DIM
    tm = min(tm, t)
    ctab = jnp.asarray(_dft_tables(gd, gd ** -0.5), BF16)
    return pl.pallas_call(
        _chan_dft_kernel,
        grid=(t // tm, FOURIER_GROUPS, 2),
        in_specs=[pl.BlockSpec((tm, gd), lambda i, g, s: (i, g)),
                  pl.BlockSpec((1, gd, gd), lambda i, g, s: (s, 0, 0))],
        out_specs=pl.BlockSpec((1, tm, gd), lambda i, g, s: (s, i, g)),
        out_shape=jax.ShapeDtypeStruct((2, t, FOURIER_WIDTH), BF16),
        compiler_params=_cparams(("parallel", "arbitrary", "arbitrary")),
        name="chan_dft",
    )(p, ctab)


def seq_dft(r, *, n_seq, seq_len, row0, prev_out=None, out_dtype=BF16):
    t = r.shape[1]
    assert row0 % seq_len == 0
    stab = jnp.asarray(_dft_tables(seq_len, seq_len ** -0.5), BF16)
    ts = min(512, seq_len)
    tn = 512
    nrow = seq_len // ts
    sb0 = row0 // seq_len
    rb0 = row0 // ts
    in_specs = [pl.BlockSpec((2, ts, seq_len), lambda b, i, j: (0, i, 0)),
                pl.BlockSpec((2, seq_len, tn), lambda b, i, j: (0, sb0 + b, j))]
    args = [stab, r]
    aliases = {}
    if prev_out is not None:
        in_specs.append(pl.BlockSpec(memory_space=pl.ANY))
        args.append(prev_out)
        aliases = {2: 0}
    return pl.pallas_call(
        _seq_dft_kernel,
        grid=(n_seq, nrow, FOURIER_WIDTH // tn),
        in_specs=in_specs,
        out_specs=pl.BlockSpec((ts, tn), lambda b, i, j: (rb0 + b * nrow + i, j)),
        out_shape=jax.ShapeDtypeStruct((t, FOURIER_WIDTH), out_dtype),
        input_output_aliases=aliases,
        compiler_params=_cparams(("parallel", "arbitrary", "arbitrary")),
        name="seq_dft",
    )(*args)


def _attn_kernel(q_ref, k_ref, v_ref, *rest):
    o_ref = rest[-1]
    k = k_ref[0].astype(BF16)
    v = v_ref[0].astype(BF16)
    scale = HEAD_DIM ** -0.5
    for g in range(GQA_GROUP):
        q = q_ref[0, :, g * HEAD_DIM:(g + 1) * HEAD_DIM]
        s = _bdot_nt(q, k)
        e = jnp.exp((s - jnp.max(s, axis=-1, keepdims=True)) * scale)
        o = _bdot(e, v) / jnp.sum(e, axis=-1, keepdims=True)
        o_ref[:, g * HEAD_DIM:(g + 1) * HEAD_DIM] = o.astype(o_ref.dtype)


def attention(q, k, v, *, total_rows, row0, prev_out=None, tq=256, out_dtype=BF16):
    b, sq, _ = q.shape
    sk = k.shape[1]
    tq = min(tq, sq)
    assert row0 % tq == 0
    gw = GQA_GROUP * HEAD_DIM
    nq = sq // tq
    rb0 = row0 // tq
    in_specs = [pl.BlockSpec((1, tq, gw), lambda bb, h, i: (bb, i, h)),
                pl.BlockSpec((1, sk, HEAD_DIM), lambda bb, h, i: (bb, 0, h)),
                pl.BlockSpec((1, sk, HEAD_DIM), lambda bb, h, i: (bb, 0, h))]
    args = [q, k, v]
    aliases = {}
    if prev_out is not None:
        in_specs.append(pl.BlockSpec(memory_space=pl.ANY))
        args.append(prev_out)
        aliases = {3: 0}
    return pl.pallas_call(
        _attn_kernel,
        grid=(b, KV_HEADS, nq),
        in_specs=in_specs,
        out_specs=pl.BlockSpec((tq, gw), lambda bb, h, i: (rb0 + bb * nq + i, h)),
        out_shape=jax.ShapeDtypeStruct((total_rows, ATTN_WIDTH), out_dtype),
        input_output_aliases=aliases,
        compiler_params=_cparams(("parallel", "parallel", "arbitrary")),
        name="attention",
    )(*args)


N_PROB = 4
DELTA_VMEM_BYTES_PER_ROW_HEAD = 14 * 1024


def _tri_masks(c):
    ri = lax.broadcasted_iota(jnp.int32, (c, c), 0)
    ci = lax.broadcasted_iota(jnp.int32, (c, c), 1)
    return ri, ci


def _unit_tri_inverse(low, ri, ci):
    c = low.shape[-1]

    def same_block(size):
        sh = int(math.log2(size))
        return (ri >> sh) == (ci >> sh)

    leaf = 4
    dg = jnp.where(same_block(leaf), low, 0.0)
    imd = jnp.where(ri == ci, 1.0, 0.0) - dg
    inv = imd + _bmm(imd, _bmm(dg, dg))
    size = leaf
    while size < c:
        off = jnp.where(same_block(2 * size) & jnp.logical_not(same_block(size)), low, 0.0)
        inv = inv - _bmm(inv, _bmm(off, inv))
        size *= 2
    return inv


def _delta_kernel(*refs, n_chunks, unroll_a, kh_step, has_prev):
    n_in = 9 + int(has_prev)
    q_ref, k_ref, v_ref, z_ref, gain_ref, cols_ref, rows_ref, sf0_ref, sb0_ref = refs[:9]
    on_ref, sf_ref, sb_ref, u_s, wq_s, akt_s, gl_s, o_s = refs[n_in:]
    c = DN_CHUNK
    dk = DN_K_DIM
    dv = DN_V_DIM
    ri, ci = _tri_masks(c)

    ca = unroll_a
    n_chain = kh_step * N_PROB

    def vhead(pp):
        return 2 * (pp // N_PROB) + pp % 2

    def body_a(g0, carry):
        rows = pl.ds(pl.multiple_of(g0 * (ca * c), ca * c), ca * c)
        chs = pl.ds(g0 * ca, ca)
        ks, qs, qks = [], [], []
        lows, rhss, decays, egs, glasts, gcols = [], [], [], [], [], []
        for hq in range(kh_step):
            k = k_ref[rows, hq * dk:(hq + 1) * dk].astype(F32).reshape(ca, c, dk)
            q = q_ref[rows, hq * dk:(hq + 1) * dk].astype(F32).reshape(ca, c, dk)
            kk = _bmm_nt(k, k)
            ks.append(k)
            qs.append(q)
            qks.append(_bmm_nt(q, k))
            cols = cols_ref[0, hq, rows, :].reshape(ca, c, 2 * N_PROB)
            rws = rows_ref[0, hq, chs]
            for p in range(N_PROB):
                head, reverse = p % 2, p >= 2
                beta = cols[:, :, p:p + 1]
                gcol = cols[:, :, N_PROB + p:N_PROB + p + 1]
                grow = rws[:, p:p + 1, :]
                glast = grow[:, :, 0:1] if reverse else grow[:, :, c - 1:c]
                incl = (ri <= ci) if reverse else (ri >= ci)
                strict = (ri < ci) if reverse else (ri > ci)
                decay = jnp.where(incl, jnp.exp(jnp.where(incl, gcol - grow, 0.0)), 0.0)
                eg = jnp.exp(gcol)
                vh = 2 * hq + head
                v = v_ref[rows, vh * dv:(vh + 1) * dv].astype(F32).reshape(ca, c, dv)
                lows.append(jnp.where(strict, kk * beta * decay, 0.0))
                rhss.append(jnp.concatenate([v * beta, k * (beta * eg)], axis=-1))
                decays.append(decay)
                egs.append(eg)
                glasts.append(glast)
                gcols.append(gcol)
        inv = _unit_tri_inverse(jnp.concatenate(lows, axis=0), ri, ci)
        sol = _bmm(inv, jnp.concatenate(rhss, axis=0))
        for pp in range(n_chain):
            k, q, qk = ks[pp // N_PROB], qs[pp // N_PROB], qks[pp // N_PROB]
            sp = sol[pp * ca:(pp + 1) * ca]
            u_s[pp, rows, :] = sp[:, :, :dv].reshape(ca * c, dv)
            wq_s[pp, chs, 0:c, :] = sp[:, :, dv:].astype(BF16)
            wq_s[pp, chs, c:2 * c, :] = (q * egs[pp]).astype(BF16)
            akt_s[pp, chs, 0:c, :] = (qk * decays[pp]).astype(BF16)
            akt_s[pp, chs, c:c + dk, :] = jnp.swapaxes(k * jnp.exp(glasts[pp] - gcols[pp]), 1, 2).astype(BF16)
            gl_s[pp, chs] = jnp.broadcast_to(jnp.exp(glasts[pp]), (ca, 8, dv))
        return carry

    lax.fori_loop(0, n_chunks // ca, body_a, 0)
    o_s[...] = jnp.zeros_like(o_s)

    def body_b(i, states):
        chs = [(n_chunks - 1 - i) if pp % N_PROB >= 2 else i for pp in range(n_chain)]
        rows = [pl.ds(pl.multiple_of(ch * c, c), c) for ch in chs]
        ws = [_bdot(wq_s[pp, chs[pp]], states[pp]) for pp in range(n_chain)]
        vn = [u_s[pp, rows[pp], :] - ws[pp][0:c] for pp in range(n_chain)]
        av = [_bdot(akt_s[pp, chs[pp]], vn[pp]) for pp in range(n_chain)]
        new_states = []
        for pp in range(n_chain):
            head = vhead(pp)
            o_s[rows[pp], head * dv:(head + 1) * dv] += ws[pp][c:2 * c] + av[pp][0:c]
            new_states.append(states[pp] * gl_s[pp, chs[pp]][0:1, :] + av[pp][c:c + dk])
        return tuple(new_states)

    init = tuple((sb0_ref if pp % N_PROB >= 2 else sf0_ref)[0, vhead(pp)] for pp in range(n_chain))
    fin = lax.fori_loop(0, n_chunks, body_b, init)
    for pp in range(n_chain):
        (sb_ref if pp % N_PROB >= 2 else sf_ref)[0, vhead(pp)] = fin[pp]
    for head in range(2 * kh_step):
        hs = slice(head * dv, (head + 1) * dv)
        o = o_s[:, hs]
        z = z_ref[:, hs].astype(F32)
        on = o * lax.rsqrt(jnp.mean(o * o, axis=-1, keepdims=True) + EPS) * gain_ref[...]
        on_ref[:, hs] = (on * (z * jax.nn.sigmoid(z))).astype(on_ref.dtype)


def delta_mixer(qkv, p, beta, g, gain, s_f0, s_b0, *, n_seq, seq_len, row0, prev_out=None):
    b, t = n_seq, seq_len
    tt = qkv.shape[0]
    h = DN_V_HEADS
    kh = DN_QK_HEADS
    rep = h // kh
    assert rep == 2 and row0 % t == 0
    rb0 = row0 // t
    c = DN_CHUNK
    nc = t // c
    gch = g.reshape(b, nc, c, 2, h)
    gcf = jnp.cumsum(gch[:, :, :, 0], axis=2).reshape(b, t, h)
    gcb = jnp.flip(jnp.cumsum(jnp.flip(gch[:, :, :, 1], axis=2), axis=2), axis=2).reshape(b, t, h)
    cols = jnp.stack([beta[:, :, 0], beta[:, :, 1], gcf, gcb], axis=2).reshape(b, t, 4, kh, rep)
    cols = jnp.transpose(cols, (0, 3, 1, 2, 4)).reshape(b, kh, t, 4 * rep)
    rows = jnp.stack([gcf, gcb], axis=2).reshape(b, nc, c, 2, kh, rep)
    rows = jnp.transpose(rows, (0, 4, 1, 3, 5, 2)).reshape(b, kh, nc, N_PROB, c)
    kh_step = 2 if t * 2 * DELTA_VMEM_BYTES_PER_ROW_HEAD <= VMEM_LIMIT // 2 else 1
    stspec = pl.BlockSpec((1, rep * kh_step, DN_K_DIM, DN_V_DIM), lambda bb, hh: (bb, hh, 0, 0))
    unroll_a = math.gcd(nc, 8)
    kw = kh_step * DN_K_DIM
    vw = kh_step * rep * DN_V_DIM
    k_blk0 = DN_QK_WIDTH // kw
    v_blk0 = 2 * DN_QK_WIDTH // vw
    z_blk0 = DN_CONV_CH // vw
    in_specs = [pl.BlockSpec((t, kw), lambda bb, hh: (rb0 + bb, hh)),
                pl.BlockSpec((t, kw), lambda bb, hh: (rb0 + bb, k_blk0 + hh)),
                pl.BlockSpec((t, vw), lambda bb, hh: (rb0 + bb, v_blk0 + hh)),
                pl.BlockSpec((t, vw), lambda bb, hh: (rb0 + bb, z_blk0 + hh)),
                pl.BlockSpec((1, DN_V_DIM), lambda bb, hh: (0, 0)),
                pl.BlockSpec((1, kh_step, t, 4 * rep), lambda bb, hh: (bb, hh, 0, 0)),
                pl.BlockSpec((1, kh_step, nc, N_PROB, c), lambda bb, hh: (bb, hh, 0, 0, 0)),
                stspec, stspec]
    args = [qkv, qkv, qkv, p, gain.reshape(1, DN_V_DIM), cols, rows, s_f0, s_b0]
    aliases = {}
    if prev_out is not None:
        in_specs.append(pl.BlockSpec(memory_space=pl.ANY))
        args.append(prev_out)
        aliases = {len(args) - 1: 0}
    on, sf, sb = pl.pallas_call(
        functools.partial(_delta_kernel, n_chunks=nc, unroll_a=unroll_a, kh_step=kh_step,
                          has_prev=prev_out is not None),
        grid=(b, kh // kh_step),
        in_specs=in_specs,
        out_specs=[pl.BlockSpec((t, vw), lambda bb, hh: (rb0 + bb, hh)), stspec, stspec],
        out_shape=[jax.ShapeDtypeStruct((tt, DN_V_WIDTH), BF16),
                   jax.ShapeDtypeStruct((b, h, DN_K_DIM, DN_V_DIM), F32),
                   jax.ShapeDtypeStruct((b, h, DN_K_DIM, DN_V_DIM), F32)],
        scratch_shapes=[pltpu.VMEM((kh_step * N_PROB, t, DN_V_DIM), F32),
                        pltpu.VMEM((kh_step * N_PROB, nc, 2 * c, DN_K_DIM), BF16),
                        pltpu.VMEM((kh_step * N_PROB, nc, c + DN_K_DIM, c), BF16),
                        pltpu.VMEM((kh_step * N_PROB, nc, 8, DN_V_DIM), F32),
                        pltpu.VMEM((t, vw), F32)],
        input_output_aliases=aliases,
        compiler_params=_cparams(("parallel", "parallel")),
        name="delta_rule",
    )(*args)
    return on, sf, sb


def _dn_prep_kernel(*refs, has_prev):
    p_ref, w_ref = refs[:2]
    o_ref = refs[2 + int(has_prev)]
    j = pl.program_id(1)
    x = p_ref[...].astype(F32)
    s, cb = x.shape
    t = lax.broadcasted_iota(jnp.int32, (s, cb), 0)
    w = w_ref[...]
    y = (w[0:1] * jnp.where(t >= 2, pltpu.roll(x, 2, 0), 0.0) + w[1:2] * jnp.where(t >= 1, pltpu.roll(x, 1, 0), 0.0)
         + w[2:3] * x + w[3:4] * jnp.where(t < s - 1, pltpu.roll(x, s - 1, 0), 0.0))
    y = y * jax.nn.sigmoid(y)
    qk_blocks = DN_QK_WIDTH // cb
    for hh in range(cb // DN_K_DIM):
        seg = y[:, hh * DN_K_DIM:(hh + 1) * DN_K_DIM]
        inv = lax.rsqrt(jnp.sum(seg * seg, axis=-1, keepdims=True) + EPS)
        mult = jnp.where(j < qk_blocks, inv * (DN_K_DIM ** -0.5), jnp.where(j < 2 * qk_blocks, inv, 1.0))
        o_ref[:, hh * DN_K_DIM:(hh + 1) * DN_K_DIM] = (seg * mult).astype(o_ref.dtype)


def dn_prep(p, conv_w, *, n_seq, seq_len, row0, prev_out=None, cb=512):
    tt = p.shape[0]
    assert row0 % seq_len == 0 and DN_QK_WIDTH % cb == 0
    rb0 = row0 // seq_len
    in_specs = [pl.BlockSpec((seq_len, cb), lambda b, j: (rb0 + b, j)),
                pl.BlockSpec((conv_w.shape[0], cb), lambda b, j: (0, j))]
    args = [p, conv_w]
    aliases = {}
    if prev_out is not None:
        in_specs.append(pl.BlockSpec(memory_space=pl.ANY))
        args.append(prev_out)
        aliases = {2: 0}
    return pl.pallas_call(
        functools.partial(_dn_prep_kernel, has_prev=prev_out is not None),
        grid=(n_seq, DN_CONV_CH // cb),
        in_specs=in_specs,
        out_specs=pl.BlockSpec((seq_len, cb), lambda b, j: (rb0 + b, j)),
        out_shape=jax.ShapeDtypeStruct((tt, DN_CONV_CH), BF16),
        input_output_aliases=aliases,
        compiler_params=_cparams(("parallel", "arbitrary")),
        name="dn_prep",
    )(*args)


def _peer_route_kernel(q_ref, keys_ref, nc_ref, re_ref, v1_ref, v2_ref):
    kk = PEER_TOPK
    neg = -jnp.inf
    for h in range(PEER_HEADS):
        r0 = h * 2 * PEER_HALF
        s1 = _bdot_nt(keys_ref[0], q_ref[:, r0:r0 + PEER_HALF])
        s2 = _bdot_nt(keys_ref[1], q_ref[:, r0 + PEER_HALF:r0 + 2 * PEER_HALF])
        sc = s1
        for r in range(kk):
            m = jnp.max(sc, axis=0, keepdims=True)
            v1_ref[r:r + 1, :] = m
            sc = jnp.where(sc == m, neg, sc)
        sc = s2
        rank2 = jnp.full(s2.shape, float(kk), F32)
        for r in range(kk):
            m = jnp.max(sc, axis=0, keepdims=True)
            v2_ref[r:r + 1, :] = m
            hit = sc == m
            rank2 = jnp.where(hit, float(r), rank2)
            sc = jnp.where(hit, neg, sc)
        v2 = v2_ref[...]
        cand = jnp.concatenate([v1_ref[0:1, :] + v2] + [v1_ref[a:a + 1, :] + v2[0:8] for a in range(1, kk)],
                               axis=0)
        top = cand[0:1, :]
        work = cand
        tau = top
        for r in range(kk):
            tau = jnp.max(work, axis=0, keepdims=True)
            work = jnp.where(work == tau, neg, work)
        sel = cand >= tau
        z = jnp.sum(jnp.where(sel, jnp.exp(cand - top), 0.0), axis=0, keepdims=True)
        one = jnp.where(sel, 1.0, 0.0)
        n1 = jnp.zeros(s1.shape, F32)
        for a in range(kk):
            lo, hi = (0, kk) if a == 0 else (kk + 8 * (a - 1), kk + 8 * a)
            cnt = jnp.sum(one[lo:hi], axis=0, keepdims=True)
            n1 = jnp.where(s1 == v1_ref[a:a + 1, :], cnt, n1)
        nc_ref[0, h] = n1
        nc_ref[1, h] = (0.5 * jnp.exp(s1 - v1_ref[0:1, :])) / z
        re_ref[0, h] = rank2.astype(BF16)
        re_ref[1, h] = jnp.exp(s2 - v2[0:1, :]).astype(BF16)


def peer_route(q, sub_keys, *, tb):
    t, d2 = q.shape
    tb = min(tb, t)
    tspec = pl.BlockSpec((2, PEER_HEADS, PEER_N_KEYS, tb), lambda j: (0, 0, 0, j))
    return pl.pallas_call(
        _peer_route_kernel,
        grid=(t // tb,),
        in_specs=[pl.BlockSpec((tb, d2), lambda j: (j, 0)),
                  pl.BlockSpec((2, PEER_N_KEYS, PEER_HALF), lambda j: (0, 0, 0))],
        out_specs=[tspec, tspec],
        out_shape=[jax.ShapeDtypeStruct((2, PEER_HEADS, PEER_N_KEYS, t), F32),
                   jax.ShapeDtypeStruct((2, PEER_HEADS, PEER_N_KEYS, t), BF16)],
        scratch_shapes=[pltpu.VMEM((PEER_TOPK, tb), F32), pltpu.VMEM((PEER_TOPK, tb), F32)],
        compiler_params=_cparams(("parallel",)),
        name="peer_route",
    )(q, sub_keys)


def _cast_kernel(x_ref, o_ref):
    o_ref[...] = x_ref[0].astype(o_ref.dtype)


def cast_layer_bf16(w, layer, *, tr=1024):
    _, r, c = w.shape
    return pl.pallas_call(
        _cast_kernel,
        grid=(r // tr,),
        in_specs=[pl.BlockSpec((1, tr, c), lambda i: (layer, i, 0))],
        out_specs=pl.BlockSpec((tr, c), lambda i: (i, 0)),
        out_shape=jax.ShapeDtypeStruct((r, c), BF16),
        compiler_params=_cparams(("parallel",)),
        name="cast_bf16",
    )(w)


def _transpose_cast_kernel(x_ref, o_ref):
    o_ref[...] = jnp.transpose(x_ref[0]).astype(o_ref.dtype)


def transpose_layer_bf16(w, layer, *, tr=512):
    _, r, c = w.shape
    return pl.pallas_call(
        _transpose_cast_kernel,
        grid=(r // tr,),
        in_specs=[pl.BlockSpec((1, tr, c), lambda i: (layer, i, 0))],
        out_specs=pl.BlockSpec((c, tr), lambda i: (0, i)),
        out_shape=jax.ShapeDtypeStruct((c, r), BF16),
        compiler_params=_cparams(("parallel",)),
        name="transpose_bf16",
    )(w)


def _peer_dense_kernel(u_ref, wd_ref, wut_ref, nc_ref, re_ref, x_ref, g_ref, o_ref, acc_s, ut_s, *,
                       rows_per_block, n_blocks):
    i = pl.program_id(1)

    @pl.when(i == 0)
    def _():
        acc_s[...] = jnp.zeros_like(acc_s)
        ut_s[...] = jnp.transpose(u_ref[...].astype(F32)).astype(BF16)

    tb = u_ref.shape[0]
    cw = min(PEER_COL_CHUNK, tb)
    e1s = pl.ds(pl.multiple_of(i * rows_per_block, rows_per_block), rows_per_block)
    n1_all = [nc_ref[0, h, e1s, :] for h in range(PEER_HEADS)]
    c1_all = [nc_ref[1, h, e1s, :] for h in range(PEER_HEADS)]
    chunks = [slice(c0, c0 + cw) for c0 in range(0, tb, cw)]
    gate_rows = [[None] * rows_per_block for _ in chunks]
    for r in range(rows_per_block):
        for h in range(PEER_HEADS):
            n1 = jnp.broadcast_to(n1_all[h][r:r + 1, :], (PACK, tb)).astype(BF16)
            c1 = jnp.broadcast_to(c1_all[h][r:r + 1, :], (PACK, tb)).astype(BF16)
            for ic, cs in enumerate(chunks):
                rank2 = re_ref[0, h, :, cs].reshape(PEER_N_KEYS // PACK, PACK, cw)
                e2w = re_ref[1, h, :, cs].reshape(PEER_N_KEYS // PACK, PACK, cw)
                wgt = jnp.where(rank2 < n1[None, :, cs], e2w * c1[None, :, cs], jnp.zeros((), BF16))
                gate_rows[ic][r] = wgt if gate_rows[ic][r] is None else gate_rows[ic][r] + wgt
    hpres = [jnp.dot(wd_ref[...], ut_s[:, cs], preferred_element_type=F32) for cs in chunks]
    for cs, hpre, rows_c in zip(chunks, hpres, gate_rows):
        half_gate = jnp.concatenate([g.reshape(PEER_N_KEYS, cw) for g in rows_c], axis=0)
        act = (hpre * (1.0 + lax.erf(hpre * (2.0 ** -0.5)))).astype(BF16) * half_gate
        acc_s[:, cs] += jnp.dot(wut_ref[...], act, preferred_element_type=F32)

    @pl.when(i == n_blocks - 1)
    def _():
        o_ref[...] = x_ref[...] + g_ref[0] * jnp.transpose(acc_s[...])


def peer_dense(u, w_down, w_up_t, nc, re, x, gates, group_of_tile, *, tb, eb):
    t, d = u.shape
    ne = w_down.shape[0]
    tb = min(tb, t)
    assert eb % PEER_N_KEYS == 0 and t % tb == 0
    nblk = ne // eb
    once = pl.Buffered(1)
    tspec = pl.BlockSpec((2, PEER_HEADS, PEER_N_KEYS, tb), lambda j, i: (0, 0, 0, j), pipeline_mode=once)
    return pl.pallas_call(
        functools.partial(_peer_dense_kernel, rows_per_block=eb // PEER_N_KEYS, n_blocks=nblk),
        grid=(t // tb, nblk),
        in_specs=[pl.BlockSpec((tb, d), lambda j, i: (j, 0), pipeline_mode=once),
                  pl.BlockSpec((eb, d), lambda j, i: (i, 0)),
                  pl.BlockSpec((d, eb), lambda j, i: (0, i)),
                  tspec, tspec,
                  pl.BlockSpec((tb, d), lambda j, i: (j, 0), pipeline_mode=once),
                  pl.BlockSpec((1, 1, d), lambda j, i: (group_of_tile(j, tb), 0, 0))],
        out_specs=pl.BlockSpec((tb, d), lambda j, i: (j, 0)),
        out_shape=jax.ShapeDtypeStruct((t, d), F32),
        scratch_shapes=[pltpu.VMEM((d, tb), F32), pltpu.VMEM((d, tb), BF16)],
        compiler_params=_cparams(("parallel", "arbitrary")),
        name="peer_dense",
    )(u, w_down, w_up_t, nc, re, x, gates)


def peer_residual(x, u, layer, w_q, sub_keys, w_down, w_up, gates, group_of_tile, *, tb=512, eb=1024):
    q = proj(u, w_q, layer=layer, tm=1024, tn=512, name="peer_q")
    nc, re = peer_route(q, sub_keys[layer], tb=256)
    return peer_dense(u, cast_layer_bf16(w_down, layer), transpose_layer_bf16(w_up, layer), nc, re,
                      x, gates, group_of_tile, tb=tb, eb=eb)


def _rope_tables(n_prompt_rows, n_seq, seq_len):
    n_rows = seq_len // GRID_W
    rows = jnp.repeat(jnp.arange(n_rows), GRID_W).astype(F32)
    cols = jnp.tile(jnp.arange(GRID_W), n_rows).astype(F32)
    n_freq = ROPE_AXIS_DIM // 2
    inv = ROPE_THETA ** (-jnp.arange(n_freq, dtype=F32) / n_freq)
    ar, ac = rows[:, None] * inv, cols[:, None] * inv
    cos_t = jnp.concatenate([jnp.cos(ar), jnp.cos(ar), jnp.cos(ac), jnp.cos(ac)], axis=-1)
    sin_t = jnp.concatenate([-jnp.sin(ar), jnp.sin(ar), -jnp.sin(ac), jnp.sin(ac)], axis=-1)
    cos_t = jnp.concatenate([jnp.ones((n_prompt_rows, HEAD_DIM), F32)] + [cos_t] * n_seq, axis=0)
    sin_t = jnp.concatenate([jnp.zeros((n_prompt_rows, HEAD_DIM), F32)] + [sin_t] * n_seq, axis=0)
    return cos_t, sin_t


def _qk_prep_kernel(pq_ref, pk_ref, qg_ref, kg_ref, cos_ref, sin_ref, q_ref, k_ref):
    cos_t = cos_ref[...]
    sin_t = sin_ref[...]
    lane = lax.broadcasted_iota(jnp.int32, cos_t.shape, 1)
    first = (lane % (ROPE_AXIS_DIM)) < (ROPE_AXIS_DIM // 2)
    quarter = ROPE_AXIS_DIM // 2

    def norm_rope(x, gain):
        y = x * lax.rsqrt(jnp.mean(x * x, axis=-1, keepdims=True) + EPS) * gain
        swapped = jnp.where(first, pltpu.roll(y, HEAD_DIM - quarter, 1), pltpu.roll(y, quarter, 1))
        return y * cos_t + swapped * sin_t

    for h in range(N_HEADS):
        hs = slice(h * HEAD_DIM, (h + 1) * HEAD_DIM)
        q_ref[:, hs] = norm_rope(pq_ref[:, hs], qg_ref[...]).astype(q_ref.dtype)
    for h in range(KV_HEADS):
        hs = slice(h * HEAD_DIM, (h + 1) * HEAD_DIM)
        k_ref[:, hs] = norm_rope(pk_ref[:, hs], kg_ref[...])


def qk_prep(p, q_gain, k_gain, cos_t, sin_t, *, tm):
    t = p.shape[0]
    tm = min(tm, t)
    assert FOURIER_WIDTH % ATTN_WIDTH == 0 and (FOURIER_WIDTH + ATTN_WIDTH) % KV_WIDTH == 0
    q_blk, k_blk = FOURIER_WIDTH // ATTN_WIDTH, (FOURIER_WIDTH + ATTN_WIDTH) // KV_WIDTH
    rspec = pl.BlockSpec((tm, HEAD_DIM), lambda i: (i, 0))
    gspec = pl.BlockSpec((1, HEAD_DIM), lambda i: (0, 0))
    return pl.pallas_call(
        _qk_prep_kernel,
        grid=(t // tm,),
        in_specs=[pl.BlockSpec((tm, ATTN_WIDTH), lambda i: (i, q_blk)),
                  pl.BlockSpec((tm, KV_WIDTH), lambda i: (i, k_blk)), gspec, gspec, rspec, rspec],
        out_specs=[pl.BlockSpec((tm, ATTN_WIDTH), lambda i: (i, 0)), pl.BlockSpec((tm, KV_WIDTH), lambda i: (i, 0))],
        out_shape=[jax.ShapeDtypeStruct((t, ATTN_WIDTH), BF16), jax.ShapeDtypeStruct((t, KV_WIDTH), F32)],
        compiler_params=_cparams(("parallel",)),
        name="qk_prep",
    )(p, p, q_gain.reshape(1, HEAD_DIM), k_gain.reshape(1, HEAD_DIM), cos_t, sin_t)


def kernel(x_prompt, x_sample, cache_k, cache_v, state_fwd, state_bwd, c, c_ctx, ada_w, ada_b, norm1, norm2,
           af_w_in, af_q_norm, af_k_norm, af_w_out, dn_w_in, dn_conv_w, dn_a_log, dn_dt_bias, dn_o_norm,
           dn_w_out, peer_w_q, peer_sub_keys, peer_w_down, peer_w_up, final_norm):
    nb, seq, d = x_prompt.shape
    db, dseq, _ = x_sample.shape
    depth = ada_w.shape[0]
    tp = nb * seq
    ts = db * dseq
    tt = tp + ts
    tm = math.gcd(math.gcd(tp, dseq), 1024)

    def group_of_tile(i, tile):
        r = i * tile
        return jnp.where(r < tp, 0, 1 + (r - tp) // dseq)

    conds = jnp.concatenate([c_ctx[None], c, jnp.zeros((8 - 1 - db, d), F32)], axis=0)
    mods = ada_all(conds, ada_w, ada_b)
    mods = mods.reshape(depth, 8, 6, d)

    x = jnp.concatenate([x_prompt.reshape(tp, d), x_sample.reshape(ts, d)], axis=0)
    new_k, new_v, new_sf, new_sb = [], [], [], []
    for i in range(depth):
        j = i // 2
        md = [mods[i, :, n][:, None, :] for n in range(6)]
        sh1, sc1, g1, sh2, sc2, g2 = md
        u = modulate(x, norm1[i], sh1, sc1, group_of_tile, tm=tm)
        if i % 2 == 0:
            p = proj(u, af_w_in, layer=j, tm=tm, tn=512, name="af_in")
            cos_t, sin_t = _rope_tables(tp, db, dseq)
            q, k = qk_prep(p, af_q_norm[j], af_k_norm[j], cos_t, sin_t, tm=tm)
            v = p[:, FOURIER_WIDTH + ATTN_WIDTH + KV_WIDTH:]
            new_k.append(k[:tp].reshape(nb, seq, KV_HEADS, HEAD_DIM))
            new_v.append(v[:tp].reshape(nb, seq, KV_HEADS, HEAD_DIM))
            keys = jnp.concatenate([k[tp:].reshape(db, dseq, KV_WIDTH),
                                    cache_k[:, j].reshape(db, -1, KV_WIDTH)], axis=1)
            vals = jnp.concatenate([v[tp:].reshape(db, dseq, KV_WIDTH),
                                    cache_v[:, j].reshape(db, -1, KV_WIDTH)], axis=1)
            att = attention(q[:tp].reshape(nb, seq, ATTN_WIDTH), k[:tp].reshape(nb, seq, KV_WIDTH),
                            v[:tp].reshape(nb, seq, KV_WIDTH), total_rows=tt, row0=0)
            att = attention(q[tp:].reshape(db, dseq, ATTN_WIDTH), keys, vals, total_rows=tt, row0=tp, prev_out=att)
            r = chan_dft(p, tm=tm)
            fou = seq_dft(r, n_seq=nb, seq_len=seq, row0=0)
            fou = seq_dft(r, n_seq=db, seq_len=dseq, row0=tp, prev_out=fou)
            x = proj_residual([fou, att], af_w_out, x, g1, group_of_tile, layer=j, tm=tm, tn=512, name="af_out")
        else:
            p = proj(u, dn_w_in, layer=j, tm=2 * tm, tn=512, ncols=DN_CONV_CH + DN_V_WIDTH, out_dtype=BF16,
                     name="dn_in")
            ba = proj(u, dn_w_in, layer=j, tm=tm, tn=128, col0=DN_CONV_CH + DN_V_WIDTH, name="dn_in_ba")
            beta_all = jax.nn.sigmoid(ba.reshape(tt, 2, 2, DN_V_HEADS)[:, 0])
            g_all = -jnp.exp(dn_a_log[j]) * jax.nn.softplus(ba.reshape(tt, 2, 2, DN_V_HEADS)[:, 1] + dn_dt_bias[j])
            qkv = dn_prep(p, dn_conv_w[j], n_seq=nb, seq_len=seq, row0=0)
            qkv = dn_prep(p, dn_conv_w[j], n_seq=db, seq_len=dseq, row0=tp, prev_out=qkv)
            zeros = jnp.zeros((nb, DN_V_HEADS, DN_K_DIM, DN_V_DIM), F32)
            on, s_f, s_b = delta_mixer(qkv, p, beta_all[:tp].reshape(nb, seq, 2, DN_V_HEADS),
                                       g_all[:tp].reshape(nb, seq, 2, DN_V_HEADS), dn_o_norm[j], zeros, zeros,
                                       n_seq=nb, seq_len=seq, row0=0)
            new_sf.append(s_f)
            new_sb.append(s_b)
            on, _, _ = delta_mixer(qkv, p, beta_all[tp:].reshape(db, dseq, 2, DN_V_HEADS),
                                   g_all[tp:].reshape(db, dseq, 2, DN_V_HEADS), dn_o_norm[j],
                                   state_fwd[:, j], state_bwd[:, j], n_seq=db, seq_len=dseq, row0=tp, prev_out=on)
            x = proj_residual([on], dn_w_out, x, g1, group_of_tile, layer=j, tm=tm, tn=256, name="dn_out")
        u2 = modulate(x, norm2[i], sh2, sc2, group_of_tile, tm=tm)
        x = peer_residual(x, u2, i, peer_w_q, peer_sub_keys, peer_w_down, peer_w_up, g2, group_of_tile)
    return (rmsnorm_rows(x, final_norm, row0=0, nrows=tp, tm=tm).reshape(nb, seq, d),
            rmsnorm_rows(x, final_norm, row0=tp, nrows=ts, tm=tm).reshape(db, dseq, d),
            jnp.stack(new_k, axis=1), jnp.stack(new_v, axis=1),
            jnp.stack(new_sf, axis=1), jnp.stack(new_sb, axis=1))
```

```python
import functools
import math

import jax
import jax.numpy as jnp
import numpy as np
from jax import lax
from jax.experimental import pallas as pl
from jax.experimental.pallas import tpu as pltpu

F32 = jnp.float32
BF16 = jnp.bfloat16

EPS = 1e-6
HEAD_DIM = 128
N_HEADS = 8
KV_HEADS = 2
GQA_GROUP = N_HEADS // KV_HEADS
FOURIER_GROUPS = 4
FOURIER_GROUP_DIM = 256
FOURIER_WIDTH = FOURIER_GROUPS * FOURIER_GROUP_DIM
ATTN_WIDTH = N_HEADS * HEAD_DIM
KV_WIDTH = KV_HEADS * HEAD_DIM
GRID_W = 64
ROPE_THETA = 10000.0
ROPE_AXIS_DIM = HEAD_DIM // 2

DN_QK_HEADS = 16
DN_V_HEADS = 32
DN_K_DIM = 128
DN_V_DIM = 128
DN_QK_WIDTH = DN_QK_HEADS * DN_K_DIM
DN_V_WIDTH = DN_V_HEADS * DN_V_DIM
DN_CONV_CH = 2 * DN_QK_WIDTH + DN_V_WIDTH
DN_CHUNK = 64

PEER_HEADS = 8
PEER_N_KEYS = 128
PEER_HALF = 128
PEER_TOPK = 16
PEER_COL_CHUNK = 256
PACK = 16

VMEM_LIMIT = 56 * 1024 * 1024


def _cparams(sem):
    return pltpu.CompilerParams(dimension_semantics=sem, vmem_limit_bytes=VMEM_LIMIT)


def _bdot(a, b):
    return jnp.dot(a.astype(BF16), b.astype(BF16), preferred_element_type=F32)


def _bdot_nt(a, b):
    return lax.dot_general(a.astype(BF16), b.astype(BF16), (((1,), (1,)), ((), ())),
                           preferred_element_type=F32)


def _bmm(a, b):
    return lax.dot_general(a.astype(BF16), b.astype(BF16), (((2,), (1,)), ((0,), (0,))),
                           preferred_element_type=F32)


def _bmm_nt(a, b):
    return lax.dot_general(a.astype(BF16), b.astype(BF16), (((2,), (2,)), ((0,), (0,))),
                           preferred_element_type=F32)


def _proj_kernel(x_ref, w_ref, o_ref):
    o_ref[...] = _bdot(x_ref[...], w_ref[0]).astype(o_ref.dtype)


def proj(x, w, *, layer=0, tm, tn, col0=0, ncols=None, out_dtype=F32, name="proj"):
    m, k = x.shape
    ncols = w.shape[2] - col0 if ncols is None else ncols
    tm = min(tm, m)
    tn = min(tn, ncols)
    assert m % tm == 0 and ncols % tn == 0 and col0 % tn == 0
    cb = col0 // tn
    return pl.pallas_call(
        _proj_kernel,
        grid=(m // tm, ncols // tn),
        in_specs=[pl.BlockSpec((tm, k), lambda i, j: (i, 0)),
                  pl.BlockSpec((1, k, tn), lambda i, j: (layer, 0, j + cb))],
        out_specs=pl.BlockSpec((tm, tn), lambda i, j: (i, j)),
        out_shape=jax.ShapeDtypeStruct((m, ncols), out_dtype),
        compiler_params=_cparams(("parallel", "arbitrary")),
        name=name,
    )(x, w)


def _proj_res_kernel(*refs, n_parts):
    x_refs, w_refs = refs[:n_parts], refs[n_parts:2 * n_parts]
    res_ref, gate_ref, o_ref = refs[2 * n_parts:]
    acc = _bdot(x_refs[0][...], w_refs[0][0])
    for x_ref, w_ref in zip(x_refs[1:], w_refs[1:]):
        acc = acc + _bdot(x_ref[...], w_ref[0])
    o_ref[...] = res_ref[...] + gate_ref[0] * acc


def proj_residual(xs, w, res, gates, group_of_tile, *, layer=0, tm, tn, name="proj_res"):
    m = xs[0].shape[0]
    n = w.shape[2]
    tm = min(tm, m)
    tn = min(tn, n)
    kp = xs[0].shape[1]
    assert m % tm == 0 and n % tn == 0 and all(x.shape == (m, kp) for x in xs)
    x_specs = [pl.BlockSpec((tm, kp), lambda i, j: (i, 0)) for _ in xs]
    w_specs = [pl.BlockSpec((1, kp, tn), functools.partial(lambda i, j, part: (layer, part, j), part=part))
               for part in range(len(xs))]
    return pl.pallas_call(
        functools.partial(_proj_res_kernel, n_parts=len(xs)),
        grid=(m // tm, n // tn),
        in_specs=x_specs + w_specs + [pl.BlockSpec((tm, tn), lambda i, j: (i, j)),
                                      pl.BlockSpec((1, 1, tn), lambda i, j: (group_of_tile(i, tm), 0, j))],
        out_specs=pl.BlockSpec((tm, tn), lambda i, j: (i, j)),
        out_shape=jax.ShapeDtypeStruct((m, n), F32),
        compiler_params=_cparams(("parallel", "arbitrary")),
        name=name,
    )(*xs, *([w] * len(xs)), res, gates)


def _modulate_kernel(x_ref, gain_ref, shift_ref, scale_ref, o_ref):
    x = x_ref[...]
    y = x * lax.rsqrt(jnp.mean(x * x, axis=-1, keepdims=True) + EPS)
    o_ref[...] = (y * gain_ref[...] * (1.0 + scale_ref[0]) + shift_ref[0]).astype(o_ref.dtype)


def modulate(x, gain, shift, scale, group_of_tile, *, tm, out_dtype=BF16):
    m, d = x.shape
    tm = min(tm, m)
    gspec = pl.BlockSpec((1, 1, d), lambda i: (group_of_tile(i, tm), 0, 0))
    return pl.pallas_call(
        _modulate_kernel,
        grid=(m // tm,),
        in_specs=[pl.BlockSpec((tm, d), lambda i: (i, 0)),
                  pl.BlockSpec((1, d), lambda i: (0, 0)), gspec, gspec],
        out_specs=pl.BlockSpec((tm, d), lambda i: (i, 0)),
        out_shape=jax.ShapeDtypeStruct((m, d), out_dtype),
        compiler_params=_cparams(("parallel",)),
        name="modulate",
    )(x, gain.reshape(1, d), shift, scale)


def _rmsnorm_kernel(x_ref, gain_ref, o_ref):
    x = x_ref[...]
    o_ref[...] = x * lax.rsqrt(jnp.mean(x * x, axis=-1, keepdims=True) + EPS) * gain_ref[...]


def rmsnorm_rows(x, gain, *, row0, nrows, tm):
    d = x.shape[1]
    tm = min(tm, nrows)
    assert row0 % tm == 0 and nrows % tm == 0
    rb0 = row0 // tm
    return pl.pallas_call(
        _rmsnorm_kernel,
        grid=(nrows // tm,),
        in_specs=[pl.BlockSpec((tm, d), lambda i: (rb0 + i, 0)), pl.BlockSpec((1, d), lambda i: (0, 0))],
        out_specs=pl.BlockSpec((tm, d), lambda i: (i, 0)),
        out_shape=jax.ShapeDtypeStruct((nrows, d), F32),
        compiler_params=_cparams(("parallel",)),
        name="final_norm",
    )(x, gain.reshape(1, d))


def _ada_kernel(c_ref, w_ref, b_ref, o_ref):
    c = c_ref[...]
    o_ref[0] = _bdot(c * jax.nn.sigmoid(c), w_ref[0]) + b_ref[0]


def ada_all(conds, ada_w, ada_b, *, tn=1024):
    depth, d, n = ada_w.shape
    r = conds.shape[0]
    return pl.pallas_call(
        _ada_kernel,
        grid=(depth, n // tn),
        in_specs=[pl.BlockSpec((r, d), lambda l, j: (0, 0)),
                  pl.BlockSpec((1, d, tn), lambda l, j: (l, 0, j)),
                  pl.BlockSpec((1, 1, tn), lambda l, j: (l, 0, j))],
        out_specs=pl.BlockSpec((1, r, tn), lambda l, j: (l, 0, j)),
        out_shape=jax.ShapeDtypeStruct((depth, r, n), F32),
        compiler_params=_cparams(("parallel", "arbitrary")),
        name="ada",
    )(conds, ada_w, ada_b.reshape(depth, 1, n))


def _dft_tables(n, scale):
    idx = np.arange(n, dtype=np.int64)
    ang = 2.0 * np.pi * ((idx[:, None] * idx[None, :]) % n).astype(np.float64) / n
    return np.stack([np.cos(ang) * scale, np.sin(ang) * scale]).astype(np.float32)


def _chan_dft_kernel(x_ref, t_ref, o_ref):
    o_ref[0] = _bdot(x_ref[...], t_ref[0]).astype(o_ref.dtype)


def _seq_dft_kernel(t_ref, r_ref, *rest):
    o_ref = rest[-1]
    o_ref[...] = (_bdot(t_ref[0], r_ref[0]) - _bdot(t_ref[1], r_ref[1])).astype(o_ref.dtype)


def chan_dft(p, *, tm):
    t = p.shape[0]
    gd = FOURIER_GROUP_DIM
    tm = min(tm, t)
    ctab = jnp.asarray(_dft_tables(gd, gd ** -0.5), BF16)
    return pl.pallas_call(
        _chan_dft_kernel,
        grid=(t // tm, FOURIER_GROUPS, 2),
        in_specs=[pl.BlockSpec((tm, gd), lambda i, g, s: (i, g)),
                  pl.BlockSpec((1, gd, gd), lambda i, g, s: (s, 0, 0))],
        out_specs=pl.BlockSpec((1, tm, gd), lambda i, g, s: (s, i, g)),
        out_shape=jax.ShapeDtypeStruct((2, t, FOURIER_WIDTH), BF16),
        compiler_params=_cparams(("parallel", "arbitrary", "arbitrary")),
        name="chan_dft",
    )(p, ctab)


def seq_dft(r, *, n_seq, seq_len, row0, prev_out=None, out_dtype=BF16):
    t = r.shape[1]
    assert row0 % seq_len == 0
    stab = jnp.asarray(_dft_tables(seq_len, seq_len ** -0.5), BF16)
    ts = min(512, seq_len)
    tn = 512
    nrow = seq_len // ts
    sb0 = row0 // seq_len
    rb0 = row0 // ts
    in_specs = [pl.BlockSpec((2, ts, seq_len), lambda b, i, j: (0, i, 0)),
                pl.BlockSpec((2, seq_len, tn), lambda b, i, j: (0, sb0 + b, j))]
    args = [stab, r]
    aliases = {}
    if prev_out is not None:
        in_specs.append(pl.BlockSpec(memory_space=pl.ANY))
        args.append(prev_out)
        aliases = {2: 0}
    return pl.pallas_call(
        _seq_dft_kernel,
        grid=(n_seq, nrow, FOURIER_WIDTH // tn),
        in_specs=in_specs,
        out_specs=pl.BlockSpec((ts, tn), lambda b, i, j: (rb0 + b * nrow + i, j)),
        out_shape=jax.ShapeDtypeStruct((t, FOURIER_WIDTH), out_dtype),
        input_output_aliases=aliases,
        compiler_params=_cparams(("parallel", "arbitrary", "arbitrary")),
        name="seq_dft",
    )(*args)


def _attn_kernel(q_ref, k_ref, v_ref, *rest):
    o_ref = rest[-1]
    k = k_ref[0].astype(BF16)
    v = v_ref[0].astype(BF16)
    scale = HEAD_DIM ** -0.5
    for g in range(GQA_GROUP):
        q = q_ref[0, :, g * HEAD_DIM:(g + 1) * HEAD_DIM]
        s = _bdot_nt(q, k)
        e = jnp.exp((s - jnp.max(s, axis=-1, keepdims=True)) * scale)
        o = _bdot(e, v) / jnp.sum(e, axis=-1, keepdims=True)
        o_ref[:, g * HEAD_DIM:(g + 1) * HEAD_DIM] = o.astype(o_ref.dtype)


def attention(q, k, v, *, total_rows, row0, prev_out=None, tq=256, out_dtype=BF16):
    b, sq, _ = q.shape
    sk = k.shape[1]
    tq = min(tq, sq)
    assert row0 % tq == 0
    gw = GQA_GROUP * HEAD_DIM
    nq = sq // tq
    rb0 = row0 // tq
    in_specs = [pl.BlockSpec((1, tq, gw), lambda bb, h, i: (bb, i, h)),
                pl.BlockSpec((1, sk, HEAD_DIM), lambda bb, h, i: (bb, 0, h)),
                pl.BlockSpec((1, sk, HEAD_DIM), lambda bb, h, i: (bb, 0, h))]
    args = [q, k, v]
    aliases = {}
    if prev_out is not None:
        in_specs.append(pl.BlockSpec(memory_space=pl.ANY))
        args.append(prev_out)
        aliases = {3: 0}
    return pl.pallas_call(
        _attn_kernel,
        grid=(b, KV_HEADS, nq),
        in_specs=in_specs,
        out_specs=pl.BlockSpec((tq, gw), lambda bb, h, i: (rb0 + bb * nq + i, h)),
        out_shape=jax.ShapeDtypeStruct((total_rows, ATTN_WIDTH), out_dtype),
        input_output_aliases=aliases,
        compiler_params=_cparams(("parallel", "parallel", "arbitrary")),
        name="attention",
    )(*args)


N_PROB = 4
DELTA_VMEM_BYTES_PER_ROW_HEAD = 14 * 1024


def _tri_masks(c):
    ri = lax.broadcasted_iota(jnp.int32, (c, c), 0)
    ci = lax.broadcasted_iota(jnp.int32, (c, c), 1)
    return ri, ci


def _unit_tri_inverse(low, ri, ci):
    c = low.shape[-1]

    def same_block(size):
        sh = int(math.log2(size))
        return (ri >> sh) == (ci >> sh)

    leaf = 4
    dg = jnp.where(same_block(leaf), low, 0.0)
    imd = jnp.where(ri == ci, 1.0, 0.0) - dg
    inv = imd + _bmm(imd, _bmm(dg, dg))
    size = leaf
    while size < c:
        off = jnp.where(same_block(2 * size) & jnp.logical_not(same_block(size)), low, 0.0)
        inv = inv - _bmm(inv, _bmm(off, inv))
        size *= 2
    return inv


def _delta_kernel(*refs, n_chunks, unroll_a, kh_step, has_prev):
    n_in = 9 + int(has_prev)
    q_ref, k_ref, v_ref, z_ref, gain_ref, cols_ref, rows_ref, sf0_ref, sb0_ref = refs[:9]
    on_ref, sf_ref, sb_ref, u_s, wq_s, akt_s, gl_s, o_s = refs[n_in:]
    c = DN_CHUNK
    dk = DN_K_DIM
    dv = DN_V_DIM
    ri, ci = _tri_masks(c)

    ca = unroll_a
    n_chain = kh_step * N_PROB

    def vhead(pp):
        return 2 * (pp // N_PROB) + pp % 2

    def body_a(g0, carry):
        rows = pl.ds(pl.multiple_of(g0 * (ca * c), ca * c), ca * c)
        chs = pl.ds(g0 * ca, ca)
        ks, qs, qks = [], [], []
        lows, rhss, decays, egs, glasts, gcols = [], [], [], [], [], []
        for hq in range(kh_step):
            k = k_ref[rows, hq * dk:(hq + 1) * dk].astype(F32).reshape(ca, c, dk)
            q = q_ref[rows, hq * dk:(hq + 1) * dk].astype(F32).reshape(ca, c, dk)
            kk = _bmm_nt(k, k)
            ks.append(k)
            qs.append(q)
            qks.append(_bmm_nt(q, k))
            cols = cols_ref[0, hq, rows, :].reshape(ca, c, 2 * N_PROB)
            rws = rows_ref[0, hq, chs]
            for p in range(N_PROB):
                head, reverse = p % 2, p >= 2
                beta = cols[:, :, p:p + 1]
                gcol = cols[:, :, N_PROB + p:N_PROB + p + 1]
                grow = rws[:, p:p + 1, :]
                glast = grow[:, :, 0:1] if reverse else grow[:, :, c - 1:c]
                incl = (ri <= ci) if reverse else (ri >= ci)
                strict = (ri < ci) if reverse else (ri > ci)
                decay = jnp.where(incl, jnp.exp(jnp.where(incl, gcol - grow, 0.0)), 0.0)
                eg = jnp.exp(gcol)
                vh = 2 * hq + head
                v = v_ref[rows, vh * dv:(vh + 1) * dv].astype(F32).reshape(ca, c, dv)
                lows.append(jnp.where(strict, kk * beta * decay, 0.0))
                rhss.append(jnp.concatenate([v * beta, k * (beta * eg)], axis=-1))
                decays.append(decay)
                egs.append(eg)
                glasts.append(glast)
                gcols.append(gcol)
        inv = _unit_tri_inverse(jnp.concatenate(lows, axis=0), ri, ci)
        sol = _bmm(inv, jnp.concatenate(rhss, axis=0))
        for pp in range(n_chain):
            k, q, qk = ks[pp // N_PROB], qs[pp // N_PROB], qks[pp // N_PROB]
            sp = sol[pp * ca:(pp + 1) * ca]
            u_s[pp, rows, :] = sp[:, :, :dv].reshape(ca * c, dv)
            wq_s[pp, chs, 0:c, :] = sp[:, :, dv:].astype(BF16)
            wq_s[pp, chs, c:2 * c, :] = (q * egs[pp]).astype(BF16)
            akt_s[pp, chs, 0:c, :] = (qk * decays[pp]).astype(BF16)
            akt_s[pp, chs, c:c + dk, :] = jnp.swapaxes(k * jnp.exp(glasts[pp] - gcols[pp]), 1, 2).astype(BF16)
            gl_s[pp, chs] = jnp.broadcast_to(jnp.exp(glasts[pp]), (ca, 8, dv))
        return carry

    lax.fori_loop(0, n_chunks // ca, body_a, 0)
    o_s[...] = jnp.zeros_like(o_s)

    def body_b(i, states):
        chs = [(n_chunks - 1 - i) if pp % N_PROB >= 2 else i for pp in range(n_chain)]
        rows = [pl.ds(pl.multiple_of(ch * c, c), c) for ch in chs]
        ws = [_bdot(wq_s[pp, chs[pp]], states[pp]) for pp in range(n_chain)]
        vn = [u_s[pp, rows[pp], :] - ws[pp][0:c] for pp in range(n_chain)]
        av = [_bdot(akt_s[pp, chs[pp]], vn[pp]) for pp in range(n_chain)]
        new_states = []
        for pp in range(n_chain):
            head = vhead(pp)
            o_s[rows[pp], head * dv:(head + 1) * dv] += ws[pp][c:2 * c] + av[pp][0:c]
            new_states.append(states[pp] * gl_s[pp, chs[pp]][0:1, :] + av[pp][c:c + dk])
        return tuple(new_states)

    init = tuple((sb0_ref if pp % N_PROB >= 2 else sf0_ref)[0, vhead(pp)] for pp in range(n_chain))
    fin = lax.fori_loop(0, n_chunks, body_b, init)
    for pp in range(n_chain):
        (sb_ref if pp % N_PROB >= 2 else sf_ref)[0, vhead(pp)] = fin[pp]
    for head in range(2 * kh_step):
        hs = slice(head * dv, (head + 1) * dv)
        o = o_s[:, hs]
        z = z_ref[:, hs].astype(F32)
        on = o * lax.rsqrt(jnp.mean(o * o, axis=-1, keepdims=True) + EPS) * gain_ref[...]
        on_ref[:, hs] = (on * (z * jax.nn.sigmoid(z))).astype(on_ref.dtype)


def delta_mixer(qkv, p, beta, g, gain, s_f0, s_b0, *, n_seq, seq_len, row0, prev_out=None):
    b, t = n_seq, seq_len
    tt = qkv.shape[0]
    h = DN_V_HEADS
    kh = DN_QK_HEADS
    rep = h // kh
    assert rep == 2 and row0 % t == 0
    rb0 = row0 // t
    c = DN_CHUNK
    nc = t // c
    gch = g.reshape(b, nc, c, 2, h)
    gcf = jnp.cumsum(gch[:, :, :, 0], axis=2).reshape(b, t, h)
    gcb = jnp.flip(jnp.cumsum(jnp.flip(gch[:, :, :, 1], axis=2), axis=2), axis=2).reshape(b, t, h)
    cols = jnp.stack([beta[:, :, 0], beta[:, :, 1], gcf, gcb], axis=2).reshape(b, t, 4, kh, rep)
    cols = jnp.transpose(cols, (0, 3, 1, 2, 4)).reshape(b, kh, t, 4 * rep)
    rows = jnp.stack([gcf, gcb], axis=2).reshape(b, nc, c, 2, kh, rep)
    rows = jnp.transpose(rows, (0, 4, 1, 3, 5, 2)).reshape(b, kh, nc, N_PROB, c)
    kh_step = max(n for n in (4, 2, 1) if n == 1 or t * n * DELTA_VMEM_BYTES_PER_ROW_HEAD <= VMEM_LIMIT // 2)
    stspec = pl.BlockSpec((1, rep * kh_step, DN_K_DIM, DN_V_DIM), lambda bb, hh: (bb, hh, 0, 0))
    unroll_a = math.gcd(nc, 8)
    kw = kh_step * DN_K_DIM
    vw = kh_step * rep * DN_V_DIM
    k_blk0 = DN_QK_WIDTH // kw
    v_blk0 = 2 * DN_QK_WIDTH // vw
    z_blk0 = DN_CONV_CH // vw
    in_specs = [pl.BlockSpec((t, kw), lambda bb, hh: (rb0 + bb, hh)),
                pl.BlockSpec((t, kw), lambda bb, hh: (rb0 + bb, k_blk0 + hh)),
                pl.BlockSpec((t, vw), lambda bb, hh: (rb0 + bb, v_blk0 + hh)),
                pl.BlockSpec((t, vw), lambda bb, hh: (rb0 + bb, z_blk0 + hh)),
                pl.BlockSpec((1, DN_V_DIM), lambda bb, hh: (0, 0)),
                pl.BlockSpec((1, kh_step, t, 4 * rep), lambda bb, hh: (bb, hh, 0, 0)),
                pl.BlockSpec((1, kh_step, nc, N_PROB, c), lambda bb, hh: (bb, hh, 0, 0, 0)),
                stspec, stspec]
    args = [qkv, qkv, qkv, p, gain.reshape(1, DN_V_DIM), cols, rows, s_f0, s_b0]
    aliases = {}
    if prev_out is not None:
        in_specs.append(pl.BlockSpec(memory_space=pl.ANY))
        args.append(prev_out)
        aliases = {len(args) - 1: 0}
    on, sf, sb = pl.pallas_call(
        functools.partial(_delta_kernel, n_chunks=nc, unroll_a=unroll_a, kh_step=kh_step,
                          has_prev=prev_out is not None),
        grid=(b, kh // kh_step),
        in_specs=in_specs,
        out_specs=[pl.BlockSpec((t, vw), lambda bb, hh: (rb0 + bb, hh)), stspec, stspec],
        out_shape=[jax.ShapeDtypeStruct((tt, DN_V_WIDTH), BF16),
                   jax.ShapeDtypeStruct((b, h, DN_K_DIM, DN_V_DIM), F32),
                   jax.ShapeDtypeStruct((b, h, DN_K_DIM, DN_V_DIM), F32)],
        scratch_shapes=[pltpu.VMEM((kh_step * N_PROB, t, DN_V_DIM), F32),
                        pltpu.VMEM((kh_step * N_PROB, nc, 2 * c, DN_K_DIM), BF16),
                        pltpu.VMEM((kh_step * N_PROB, nc, c + DN_K_DIM, c), BF16),
                        pltpu.VMEM((kh_step * N_PROB, nc, 8, DN_V_DIM), F32),
                        pltpu.VMEM((t, vw), F32)],
        input_output_aliases=aliases,
        compiler_params=_cparams(("parallel", "parallel")),
        name="delta_rule",
    )(*args)
    return on, sf, sb


def _dn_prep_kernel(*refs, has_prev):
    p_ref, w_ref = refs[:2]
    o_ref = refs[2 + int(has_prev)]
    j = pl.program_id(1)
    x = p_ref[...].astype(F32)
    s, cb = x.shape
    t = lax.broadcasted_iota(jnp.int32, (s, cb), 0)
    w = w_ref[...]
    y = (w[0:1] * jnp.where(t >= 2, pltpu.roll(x, 2, 0), 0.0) + w[1:2] * jnp.where(t >= 1, pltpu.roll(x, 1, 0), 0.0)
         + w[2:3] * x + w[3:4] * jnp.where(t < s - 1, pltpu.roll(x, s - 1, 0), 0.0))
    y = y * jax.nn.sigmoid(y)
    qk_blocks = DN_QK_WIDTH // cb
    for hh in range(cb // DN_K_DIM):
        seg = y[:, hh * DN_K_DIM:(hh + 1) * DN_K_DIM]
        inv = lax.rsqrt(jnp.sum(seg * seg, axis=-1, keepdims=True) + EPS)
        mult = jnp.where(j < qk_blocks, inv * (DN_K_DIM ** -0.5), jnp.where(j < 2 * qk_blocks, inv, 1.0))
        o_ref[:, hh * DN_K_DIM:(hh + 1) * DN_K_DIM] = (seg * mult).astype(o_ref.dtype)


def dn_prep(p, conv_w, *, n_seq, seq_len, row0, prev_out=None, cb=512):
    tt = p.shape[0]
    assert row0 % seq_len == 0 and DN_QK_WIDTH % cb == 0
    rb0 = row0 // seq_len
    in_specs = [pl.BlockSpec((seq_len, cb), lambda b, j: (rb0 + b, j)),
                pl.BlockSpec((conv_w.shape[0], cb), lambda b, j: (0, j))]
    args = [p, conv_w]
    aliases = {}
    if prev_out is not None:
        in_specs.append(pl.BlockSpec(memory_space=pl.ANY))
        args.append(prev_out)
        aliases = {2: 0}
    return pl.pallas_call(
        functools.partial(_dn_prep_kernel, has_prev=prev_out is not None),
        grid=(n_seq, DN_CONV_CH // cb),
        in_specs=in_specs,
        out_specs=pl.BlockSpec((seq_len, cb), lambda b, j: (rb0 + b, j)),
        out_shape=jax.ShapeDtypeStruct((tt, DN_CONV_CH), BF16),
        input_output_aliases=aliases,
        compiler_params=_cparams(("parallel", "arbitrary")),
        name="dn_prep",
    )(*args)


def _peer_route_kernel(q_ref, keys_ref, nc_ref, re_ref, v1_ref, v2_ref):
    kk = PEER_TOPK
    neg = -jnp.inf
    for h in range(PEER_HEADS):
        r0 = h * 2 * PEER_HALF
        s1 = _bdot_nt(keys_ref[0], q_ref[:, r0:r0 + PEER_HALF])
        s2 = _bdot_nt(keys_ref[1], q_ref[:, r0 + PEER_HALF:r0 + 2 * PEER_HALF])
        sc = s1
        for r in range(kk):
            m = jnp.max(sc, axis=0, keepdims=True)
            v1_ref[r:r + 1, :] = m
            sc = jnp.where(sc == m, neg, sc)
        sc = s2
        rank2 = jnp.full(s2.shape, float(kk), F32)
        for r in range(kk):
            m = jnp.max(sc, axis=0, keepdims=True)
            v2_ref[r:r + 1, :] = m
            hit = sc == m
            rank2 = jnp.where(hit, float(r), rank2)
            sc = jnp.where(hit, neg, sc)
        v2 = v2_ref[...]
        cand = jnp.concatenate([v1_ref[0:1, :] + v2] + [v1_ref[a:a + 1, :] + v2[0:8] for a in range(1, kk)],
                               axis=0)
        top = cand[0:1, :]
        work = cand
        tau = top
        for r in range(kk):
            tau = jnp.max(work, axis=0, keepdims=True)
            work = jnp.where(work == tau, neg, work)
        sel = cand >= tau
        z = jnp.sum(jnp.where(sel, jnp.exp(cand - top), 0.0), axis=0, keepdims=True)
        one = jnp.where(sel, 1.0, 0.0)
        n1 = jnp.zeros(s1.shape, F32)
        for a in range(kk):
            lo, hi = (0, kk) if a == 0 else (kk + 8 * (a - 1), kk + 8 * a)
            cnt = jnp.sum(one[lo:hi], axis=0, keepdims=True)
            n1 = jnp.where(s1 == v1_ref[a:a + 1, :], cnt, n1)
        nc_ref[0, h] = n1
        nc_ref[1, h] = (0.5 * jnp.exp(s1 - v1_ref[0:1, :])) / z
        re_ref[0, h] = rank2.astype(BF16)
        re_ref[1, h] = jnp.exp(s2 - v2[0:1, :]).astype(BF16)


def peer_route(q, sub_keys, *, tb):
    t, d2 = q.shape
    tb = min(tb, t)
    tspec = pl.BlockSpec((2, PEER_HEADS, PEER_N_KEYS, tb), lambda j: (0, 0, 0, j))
    return pl.pallas_call(
        _peer_route_kernel,
        grid=(t // tb,),
        in_specs=[pl.BlockSpec((tb, d2), lambda j: (j, 0)),
                  pl.BlockSpec((2, PEER_N_KEYS, PEER_HALF), lambda j: (0, 0, 0))],
        out_specs=[tspec, tspec],
        out_shape=[jax.ShapeDtypeStruct((2, PEER_HEADS, PEER_N_KEYS, t), F32),
                   jax.ShapeDtypeStruct((2, PEER_HEADS, PEER_N_KEYS, t), BF16)],
        scratch_shapes=[pltpu.VMEM((PEER_TOPK, tb), F32), pltpu.VMEM((PEER_TOPK, tb), F32)],
        compiler_params=_cparams(("parallel",)),
        name="peer_route",
    )(q, sub_keys)


def _cast_kernel(x_ref, o_ref):
    o_ref[...] = x_ref[0].astype(o_ref.dtype)


def cast_layer_bf16(w, layer, *, tr=1024):
    _, r, c = w.shape
    return pl.pallas_call(
        _cast_kernel,
        grid=(r // tr,),
        in_specs=[pl.BlockSpec((1, tr, c), lambda i: (layer, i, 0))],
        out_specs=pl.BlockSpec((tr, c), lambda i: (i, 0)),
        out_shape=jax.ShapeDtypeStruct((r, c), BF16),
        compiler_params=_cparams(("parallel",)),
        name="cast_bf16",
    )(w)


def _transpose_cast_kernel(x_ref, o_ref):
    o_ref[...] = jnp.transpose(x_ref[0]).astype(o_ref.dtype)


def transpose_layer_bf16(w, layer, *, tr=512):
    _, r, c = w.shape
    return pl.pallas_call(
        _transpose_cast_kernel,
        grid=(r // tr,),
        in_specs=[pl.BlockSpec((1, tr, c), lambda i: (layer, i, 0))],
        out_specs=pl.BlockSpec((c, tr), lambda i: (0, i)),
        out_shape=jax.ShapeDtypeStruct((c, r), BF16),
        compiler_params=_cparams(("parallel",)),
        name="transpose_bf16",
    )(w)


def _peer_dense_kernel(u_ref, wd_ref, wut_ref, nc_ref, re_ref, x_ref, g_ref, o_ref, acc_s, ut_s, *,
                       rows_per_block, n_blocks):
    i = pl.program_id(1)

    @pl.when(i == 0)
    def _():
        acc_s[...] = jnp.zeros_like(acc_s)
        ut_s[...] = jnp.transpose(u_ref[...].astype(F32)).astype(BF16)

    tb = u_ref.shape[0]
    cw = min(PEER_COL_CHUNK, tb)
    e1s = pl.ds(pl.multiple_of(i * rows_per_block, rows_per_block), rows_per_block)
    n1_all = [nc_ref[0, h, e1s, :] for h in range(PEER_HEADS)]
    c1_all = [nc_ref[1, h, e1s, :] for h in range(PEER_HEADS)]
    chunks = [slice(c0, c0 + cw) for c0 in range(0, tb, cw)]
    gate_rows = [[None] * rows_per_block for _ in chunks]
    for r in range(rows_per_block):
        for h in range(PEER_HEADS):
            n1 = jnp.broadcast_to(n1_all[h][r:r + 1, :], (PACK, tb)).astype(BF16)
            c1 = jnp.broadcast_to(c1_all[h][r:r + 1, :], (PACK, tb)).astype(BF16)
            for ic, cs in enumerate(chunks):
                rank2 = re_ref[0, h, :, cs].reshape(PEER_N_KEYS // PACK, PACK, cw)
                e2w = re_ref[1, h, :, cs].reshape(PEER_N_KEYS // PACK, PACK, cw)
                wgt = jnp.where(rank2 < n1[None, :, cs], e2w * c1[None, :, cs], jnp.zeros((), BF16))
                gate_rows[ic][r] = wgt if gate_rows[ic][r] is None else gate_rows[ic][r] + wgt
    hpres = [jnp.dot(wd_ref[...], ut_s[:, cs], preferred_element_type=F32) for cs in chunks]
    for cs, hpre, rows_c in zip(chunks, hpres, gate_rows):
        half_gate = jnp.concatenate([g.reshape(PEER_N_KEYS, cw) for g in rows_c], axis=0)
        act = (hpre * (1.0 + lax.erf(hpre * (2.0 ** -0.5)))).astype(BF16) * half_gate
        acc_s[:, cs] += jnp.dot(wut_ref[...], act, preferred_element_type=F32)

    @pl.when(i == n_blocks - 1)
    def _():
        o_ref[...] = x_ref[...] + g_ref[0] * jnp.transpose(acc_s[...])


def peer_dense(u, w_down, w_up_t, nc, re, x, gates, group_of_tile, *, tb, eb):
    t, d = u.shape
    ne = w_down.shape[0]
    tb = min(tb, t)
    assert eb % PEER_N_KEYS == 0 and t % tb == 0
    nblk = ne // eb
    once = pl.Buffered(1)
    tspec = pl.BlockSpec((2, PEER_HEADS, PEER_N_KEYS, tb), lambda j, i: (0, 0, 0, j), pipeline_mode=once)
    return pl.pallas_call(
        functools.partial(_peer_dense_kernel, rows_per_block=eb // PEER_N_KEYS, n_blocks=nblk),
        grid=(t // tb, nblk),
        in_specs=[pl.BlockSpec((tb, d), lambda j, i: (j, 0), pipeline_mode=once),
                  pl.BlockSpec((eb, d), lambda j, i: (i, 0)),
                  pl.BlockSpec((d, eb), lambda j, i: (0, i)),
                  tspec, tspec,
                  pl.BlockSpec((tb, d), lambda j, i: (j, 0), pipeline_mode=once),
                  pl.BlockSpec((1, 1, d), lambda j, i: (group_of_tile(j, tb), 0, 0))],
        out_specs=pl.BlockSpec((tb, d), lambda j, i: (j, 0)),
        out_shape=jax.ShapeDtypeStruct((t, d), F32),
        scratch_shapes=[pltpu.VMEM((d, tb), F32), pltpu.VMEM((d, tb), BF16)],
        compiler_params=_cparams(("parallel", "arbitrary")),
        name="peer_dense",
    )(u, w_down, w_up_t, nc, re, x, gates)


def peer_residual(x, u, layer, w_q, sub_keys, w_down, w_up, gates, group_of_tile, *, tb=512, eb=1024):
    q = proj(u, w_q, layer=layer, tm=1024, tn=512, name="peer_q")
    nc, re = peer_route(q, sub_keys[layer], tb=256)
    return peer_dense(u, cast_layer_bf16(w_down, layer), transpose_layer_bf16(w_up, layer), nc, re,
                      x, gates, group_of_tile, tb=tb, eb=eb)


def _rope_tables(n_prompt_rows, n_seq, seq_len):
    n_rows = seq_len // GRID_W
    rows = jnp.repeat(jnp.arange(n_rows), GRID_W).astype(F32)
    cols = jnp.tile(jnp.arange(GRID_W), n_rows).astype(F32)
    n_freq = ROPE_AXIS_DIM // 2
    inv = ROPE_THETA ** (-jnp.arange(n_freq, dtype=F32) / n_freq)
    ar, ac = rows[:, None] * inv, cols[:, None] * inv
    cos_t = jnp.concatenate([jnp.cos(ar), jnp.cos(ar), jnp.cos(ac), jnp.cos(ac)], axis=-1)
    sin_t = jnp.concatenate([-jnp.sin(ar), jnp.sin(ar), -jnp.sin(ac), jnp.sin(ac)], axis=-1)
    cos_t = jnp.concatenate([jnp.ones((n_prompt_rows, HEAD_DIM), F32)] + [cos_t] * n_seq, axis=0)
    sin_t = jnp.concatenate([jnp.zeros((n_prompt_rows, HEAD_DIM), F32)] + [sin_t] * n_seq, axis=0)
    return cos_t, sin_t


def _qk_prep_kernel(pq_ref, pk_ref, qg_ref, kg_ref, cos_ref, sin_ref, q_ref, k_ref):
    cos_t = cos_ref[...]
    sin_t = sin_ref[...]
    lane = lax.broadcasted_iota(jnp.int32, cos_t.shape, 1)
    first = (lane % (ROPE_AXIS_DIM)) < (ROPE_AXIS_DIM // 2)
    quarter = ROPE_AXIS_DIM // 2

    def norm_rope(x, gain):
        y = x * lax.rsqrt(jnp.mean(x * x, axis=-1, keepdims=True) + EPS) * gain
        swapped = jnp.where(first, pltpu.roll(y, HEAD_DIM - quarter, 1), pltpu.roll(y, quarter, 1))
        return y * cos_t + swapped * sin_t

    for h in range(N_HEADS):
        hs = slice(h * HEAD_DIM, (h + 1) * HEAD_DIM)
        q_ref[:, hs] = norm_rope(pq_ref[:, hs], qg_ref[...]).astype(q_ref.dtype)
    for h in range(KV_HEADS):
        hs = slice(h * HEAD_DIM, (h + 1) * HEAD_DIM)
        k_ref[:, hs] = norm_rope(pk_ref[:, hs], kg_ref[...])


def qk_prep(p, q_gain, k_gain, cos_t, sin_t, *, tm):
    t = p.shape[0]
    tm = min(tm, t)
    assert FOURIER_WIDTH % ATTN_WIDTH == 0 and (FOURIER_WIDTH + ATTN_WIDTH) % KV_WIDTH == 0
    q_blk, k_blk = FOURIER_WIDTH // ATTN_WIDTH, (FOURIER_WIDTH + ATTN_WIDTH) // KV_WIDTH
    rspec = pl.BlockSpec((tm, HEAD_DIM), lambda i: (i, 0))
    gspec = pl.BlockSpec((1, HEAD_DIM), lambda i: (0, 0))
    return pl.pallas_call(
        _qk_prep_kernel,
        grid=(t // tm,),
        in_specs=[pl.BlockSpec((tm, ATTN_WIDTH), lambda i: (i, q_blk)),
                  pl.BlockSpec((tm, KV_WIDTH), lambda i: (i, k_blk)), gspec, gspec, rspec, rspec],
        out_specs=[pl.BlockSpec((tm, ATTN_WIDTH), lambda i: (i, 0)), pl.BlockSpec((tm, KV_WIDTH), lambda i: (i, 0))],
        out_shape=[jax.ShapeDtypeStruct((t, ATTN_WIDTH), BF16), jax.ShapeDtypeStruct((t, KV_WIDTH), F32)],
        compiler_params=_cparams(("parallel",)),
        name="qk_prep",
    )(p, p, q_gain.reshape(1, HEAD_DIM), k_gain.reshape(1, HEAD_DIM), cos_t, sin_t)


def kernel(x_prompt, x_sample, cache_k, cache_v, state_fwd, state_bwd, c, c_ctx, ada_w, ada_b, norm1, norm2,
           af_w_in, af_q_norm, af_k_norm, af_w_out, dn_w_in, dn_conv_w, dn_a_log, dn_dt_bias, dn_o_norm,
           dn_w_out, peer_w_q, peer_sub_keys, peer_w_down, peer_w_up, final_norm):
    nb, seq, d = x_prompt.shape
    db, dseq, _ = x_sample.shape
    depth = ada_w.shape[0]
    tp = nb * seq
    ts = db * dseq
    tt = tp + ts
    tm = math.gcd(math.gcd(tp, dseq), 1024)

    def group_of_tile(i, tile):
        r = i * tile
        return jnp.where(r < tp, 0, 1 + (r - tp) // dseq)

    conds = jnp.concatenate([c_ctx[None], c, jnp.zeros((8 - 1 - db, d), F32)], axis=0)
    mods = ada_all(conds, ada_w, ada_b)
    mods = mods.reshape(depth, 8, 6, d)

    x = jnp.concatenate([x_prompt.reshape(tp, d), x_sample.reshape(ts, d)], axis=0)
    new_k, new_v, new_sf, new_sb = [], [], [], []
    for i in range(depth):
        j = i // 2
        md = [mods[i, :, n][:, None, :] for n in range(6)]
        sh1, sc1, g1, sh2, sc2, g2 = md
        u = modulate(x, norm1[i], sh1, sc1, group_of_tile, tm=tm)
        if i % 2 == 0:
            p = proj(u, af_w_in, layer=j, tm=tm, tn=512, name="af_in")
            cos_t, sin_t = _rope_tables(tp, db, dseq)
            q, k = qk_prep(p, af_q_norm[j], af_k_norm[j], cos_t, sin_t, tm=tm)
            v = p[:, FOURIER_WIDTH + ATTN_WIDTH + KV_WIDTH:]
            new_k.append(k[:tp].reshape(nb, seq, KV_HEADS, HEAD_DIM))
            new_v.append(v[:tp].reshape(nb, seq, KV_HEADS, HEAD_DIM))
            keys = jnp.concatenate([k[tp:].reshape(db, dseq, KV_WIDTH),
                                    cache_k[:, j].reshape(db, -1, KV_WIDTH)], axis=1)
            vals = jnp.concatenate([v[tp:].reshape(db, dseq, KV_WIDTH),
                                    cache_v[:, j].reshape(db, -1, KV_WIDTH)], axis=1)
            att = attention(q[:tp].reshape(nb, seq, ATTN_WIDTH), k[:tp].reshape(nb, seq, KV_WIDTH),
                            v[:tp].reshape(nb, seq, KV_WIDTH), total_rows=tt, row0=0)
            att = attention(q[tp:].reshape(db, dseq, ATTN_WIDTH), keys, vals, total_rows=tt, row0=tp, prev_out=att)
            r = chan_dft(p, tm=tm)
            fou = seq_dft(r, n_seq=nb, seq_len=seq, row0=0)
            fou = seq_dft(r, n_seq=db, seq_len=dseq, row0=tp, prev_out=fou)
            x = proj_residual([fou, att], af_w_out, x, g1, group_of_tile, layer=j, tm=tm, tn=512, name="af_out")
        else:
            p = proj(u, dn_w_in, layer=j, tm=2 * tm, tn=512, ncols=DN_CONV_CH + DN_V_WIDTH, out_dtype=BF16,
                     name="dn_in")
            ba = proj(u, dn_w_in, layer=j, tm=tm, tn=128, col0=DN_CONV_CH + DN_V_WIDTH, name="dn_in_ba")
            beta_all = jax.nn.sigmoid(ba.reshape(tt, 2, 2, DN_V_HEADS)[:, 0])
            g_all = -jnp.exp(dn_a_log[j]) * jax.nn.softplus(ba.reshape(tt, 2, 2, DN_V_HEADS)[:, 1] + dn_dt_bias[j])
            qkv = dn_prep(p, dn_conv_w[j], n_seq=nb, seq_len=seq, row0=0)
            qkv = dn_prep(p, dn_conv_w[j], n_seq=db, seq_len=dseq, row0=tp, prev_out=qkv)
            zeros = jnp.zeros((nb, DN_V_HEADS, DN_K_DIM, DN_V_DIM), F32)
            on, s_f, s_b = delta_mixer(qkv, p, beta_all[:tp].reshape(nb, seq, 2, DN_V_HEADS),
                                       g_all[:tp].reshape(nb, seq, 2, DN_V_HEADS), dn_o_norm[j], zeros, zeros,
                                       n_seq=nb, seq_len=seq, row0=0)
            new_sf.append(s_f)
            new_sb.append(s_b)
            on, _, _ = delta_mixer(qkv, p, beta_all[tp:].reshape(db, dseq, 2, DN_V_HEADS),
                                   g_all[tp:].reshape(db, dseq, 2, DN_V_HEADS), dn_o_norm[j],
                                   state_fwd[:, j], state_bwd[:, j], n_seq=db, seq_len=dseq, row0=tp, prev_out=on)
            x = proj_residual([on], dn_w_out, x, g1, group_of_tile, layer=j, tm=tm, tn=256, name="dn_out")
        u2 = modulate(x, norm2[i], sh2, sc2, group_of_tile, tm=tm)
        x = peer_residual(x, u2, i, peer_w_q, peer_sub_keys, peer_w_down, peer_w_up, g2, group_of_tile)
    return (rmsnorm_rows(x, final_norm, row0=0, nrows=tp, tm=tm).reshape(nb, seq, d),
            rmsnorm_rows(x, final_norm, row0=tp, nrows=ts, tm=tm).reshape(db, dseq, d),
            jnp.stack(new_k, axis=1), jnp.stack(new_v, axis=1),
            jnp.stack(new_sf, axis=1), jnp.stack(new_sb, axis=1))
```

```python
import functools
import math

import jax
import jax.numpy as jnp
import numpy as np
from jax import lax
from jax.experimental import pallas as pl
from jax.experimental.pallas import tpu as pltpu

F32 = jnp.float32
BF16 = jnp.bfloat16

EPS = 1e-6
HEAD_DIM = 128
N_HEADS = 8
KV_HEADS = 2
GQA_GROUP = N_HEADS // KV_HEADS
FOURIER_GROUPS = 4
FOURIER_GROUP_DIM = 256
FOURIER_WIDTH = FOURIER_GROUPS * FOURIER_GROUP_DIM
ATTN_WIDTH = N_HEADS * HEAD_DIM
KV_WIDTH = KV_HEADS * HEAD_DIM
GRID_W = 64
ROPE_THETA = 10000.0
ROPE_AXIS_DIM = HEAD_DIM // 2

DN_QK_HEADS = 16
DN_V_HEADS = 32
DN_K_DIM = 128
DN_V_DIM = 128
DN_QK_WIDTH = DN_QK_HEADS * DN_K_DIM
DN_V_WIDTH = DN_V_HEADS * DN_V_DIM
DN_CONV_CH = 2 * DN_QK_WIDTH + DN_V_WIDTH
DN_CHUNK = 64

PEER_HEADS = 8
PEER_N_KEYS = 128
PEER_HALF = 128
PEER_TOPK = 16
PEER_COL_CHUNK = 256
PACK = 16

VMEM_LIMIT = 56 * 1024 * 1024


def _cparams(sem):
    return pltpu.CompilerParams(dimension_semantics=sem, vmem_limit_bytes=VMEM_LIMIT)


def _bdot(a, b):
    return jnp.dot(a.astype(BF16), b.astype(BF16), preferred_element_type=F32)


def _bdot_nt(a, b):
    return lax.dot_general(a.astype(BF16), b.astype(BF16), (((1,), (1,)), ((), ())),
                           preferred_element_type=F32)


def _bmm(a, b):
    return lax.dot_general(a.astype(BF16), b.astype(BF16), (((2,), (1,)), ((0,), (0,))),
                           preferred_element_type=F32)


def _bmm_nt(a, b):
    return lax.dot_general(a.astype(BF16), b.astype(BF16), (((2,), (2,)), ((0,), (0,))),
                           preferred_element_type=F32)


def _proj_kernel(x_ref, w_ref, o_ref):
    o_ref[...] = _bdot(x_ref[...], w_ref[0]).astype(o_ref.dtype)


def proj(x, w, *, layer=0, tm, tn, col0=0, ncols=None, out_dtype=F32, name="proj"):
    m, k = x.shape
    ncols = w.shape[2] - col0 if ncols is None else ncols
    tm = min(tm, m)
    tn = min(tn, ncols)
    assert m % tm == 0 and ncols % tn == 0 and col0 % tn == 0
    cb = col0 // tn
    return pl.pallas_call(
        _proj_kernel,
        grid=(m // tm, ncols // tn),
        in_specs=[pl.BlockSpec((tm, k), lambda i, j: (i, 0)),
                  pl.BlockSpec((1, k, tn), lambda i, j: (layer, 0, j + cb))],
        out_specs=pl.BlockSpec((tm, tn), lambda i, j: (i, j)),
        out_shape=jax.ShapeDtypeStruct((m, ncols), out_dtype),
        compiler_params=_cparams(("parallel", "arbitrary")),
        name=name,
    )(x, w)


def _proj_res_kernel(*refs, n_parts):
    x_refs, w_refs = refs[:n_parts], refs[n_parts:2 * n_parts]
    res_ref, gate_ref, o_ref = refs[2 * n_parts:]
    acc = _bdot(x_refs[0][...], w_refs[0][0])
    for x_ref, w_ref in zip(x_refs[1:], w_refs[1:]):
        acc = acc + _bdot(x_ref[...], w_ref[0])
    o_ref[...] = res_ref[...] + gate_ref[0] * acc


def proj_residual(xs, w, res, gates, group_of_tile, *, layer=0, tm, tn, name="proj_res"):
    m = xs[0].shape[0]
    n = w.shape[2]
    tm = min(tm, m)
    tn = min(tn, n)
    kp = xs[0].shape[1]
    assert m % tm == 0 and n % tn == 0 and all(x.shape == (m, kp) for x in xs)
    x_specs = [pl.BlockSpec((tm, kp), lambda i, j: (i, 0)) for _ in xs]
    w_specs = [pl.BlockSpec((1, kp, tn), functools.partial(lambda i, j, part: (layer, part, j), part=part))
               for part in range(len(xs))]
    return pl.pallas_call(
        functools.partial(_proj_res_kernel, n_parts=len(xs)),
        grid=(m // tm, n // tn),
        in_specs=x_specs + w_specs + [pl.BlockSpec((tm, tn), lambda i, j: (i, j)),
                                      pl.BlockSpec((1, 1, tn), lambda i, j: (group_of_tile(i, tm), 0, j))],
        out_specs=pl.BlockSpec((tm, tn), lambda i, j: (i, j)),
        out_shape=jax.ShapeDtypeStruct((m, n), F32),
        compiler_params=_cparams(("parallel", "arbitrary")),
        name=name,
    )(*xs, *([w] * len(xs)), res, gates)


def _modulate_kernel(x_ref, gain_ref, shift_ref, scale_ref, o_ref):
    x = x_ref[...]
    y = x * lax.rsqrt(jnp.mean(x * x, axis=-1, keepdims=True) + EPS)
    o_ref[...] = (y * gain_ref[...] * (1.0 + scale_ref[0]) + shift_ref[0]).astype(o_ref.dtype)


def modulate(x, gain, shift, scale, group_of_tile, *, tm, out_dtype=BF16):
    m, d = x.shape
    tm = min(tm, m)
    gspec = pl.BlockSpec((1, 1, d), lambda i: (group_of_tile(i, tm), 0, 0))
    return pl.pallas_call(
        _modulate_kernel,
        grid=(m // tm,),
        in_specs=[pl.BlockSpec((tm, d), lambda i: (i, 0)),
                  pl.BlockSpec((1, d), lambda i: (0, 0)), gspec, gspec],
        out_specs=pl.BlockSpec((tm, d), lambda i: (i, 0)),
        out_shape=jax.ShapeDtypeStruct((m, d), out_dtype),
        compiler_params=_cparams(("parallel",)),
        name="modulate",
    )(x, gain.reshape(1, d), shift, scale)


def _rmsnorm_kernel(x_ref, gain_ref, o_ref):
    x = x_ref[...]
    o_ref[...] = x * lax.rsqrt(jnp.mean(x * x, axis=-1, keepdims=True) + EPS) * gain_ref[...]


def rmsnorm_rows(x, gain, *, row0, nrows, tm):
    d = x.shape[1]
    tm = min(tm, nrows)
    assert row0 % tm == 0 and nrows % tm == 0
    rb0 = row0 // tm
    return pl.pallas_call(
        _rmsnorm_kernel,
        grid=(nrows // tm,),
        in_specs=[pl.BlockSpec((tm, d), lambda i: (rb0 + i, 0)), pl.BlockSpec((1, d), lambda i: (0, 0))],
        out_specs=pl.BlockSpec((tm, d), lambda i: (i, 0)),
        out_shape=jax.ShapeDtypeStruct((nrows, d), F32),
        compiler_params=_cparams(("parallel",)),
        name="final_norm",
    )(x, gain.reshape(1, d))


def _ada_kernel(c_ref, w_ref, b_ref, o_ref):
    c = c_ref[...]
    o_ref[0] = _bdot(c * jax.nn.sigmoid(c), w_ref[0]) + b_ref[0]


def ada_all(conds, ada_w, ada_b, *, tn=1024):
    depth, d, n = ada_w.shape
    r = conds.shape[0]
    return pl.pallas_call(
        _ada_kernel,
        grid=(depth, n // tn),
        in_specs=[pl.BlockSpec((r, d), lambda l, j: (0, 0)),
                  pl.BlockSpec((1, d, tn), lambda l, j: (l, 0, j)),
                  pl.BlockSpec((1, 1, tn), lambda l, j: (l, 0, j))],
        out_specs=pl.BlockSpec((1, r, tn), lambda l, j: (l, 0, j)),
        out_shape=jax.ShapeDtypeStruct((depth, r, n), F32),
        compiler_params=_cparams(("parallel", "arbitrary")),
        name="ada",
    )(conds, ada_w, ada_b.reshape(depth, 1, n))


def _dft_tables(n, scale):
    idx = np.arange(n, dtype=np.int64)
    ang = 2.0 * np.pi * ((idx[:, None] * idx[None, :]) % n).astype(np.float64) / n
    return np.stack([np.cos(ang) * scale, np.sin(ang) * scale]).astype(np.float32)


def _chan_dft_kernel(x_ref, t_ref, o_ref):
    o_ref[0] = _bdot(x_ref[...], t_ref[0]).astype(o_ref.dtype)


def _seq_dft_kernel(t_ref, r_ref, *rest):
    o_ref = rest[-1]
    o_ref[...] = (_bdot(t_ref[0], r_ref[0]) - _bdot(t_ref[1], r_ref[1])).astype(o_ref.dtype)


def chan_dft(p, *, tm):
    t = p.shape[0]
    gd = FOURIER_GROUP_DIM
    tm = min(tm, t)
    ctab = jnp.asarray(_dft_tables(gd, gd ** -0.5), BF16)
    return pl.pallas_call(
        _chan_dft_kernel,
        grid=(t // tm, FOURIER_GROUPS, 2),
        in_specs=[pl.BlockSpec((tm, gd), lambda i, g, s: (i, g)),
                  pl.BlockSpec((1, gd, gd), lambda i, g, s: (s, 0, 0))],
        out_specs=pl.BlockSpec((1, tm, gd), lambda i, g, s: (s, i, g)),
        out_shape=jax.ShapeDtypeStruct((2, t, FOURIER_WIDTH), BF16),
        compiler_params=_cparams(("parallel", "arbitrary", "arbitrary")),
        name="chan_dft",
    )(p, ctab)


def seq_dft(r, *, n_seq, seq_len, row0, prev_out=None, out_dtype=BF16):
    t = r.shape[1]
    assert row0 % seq_len == 0
    stab = jnp.asarray(_dft_tables(seq_len, seq_len ** -0.5), BF16)
    ts = min(512, seq_len)
    tn = 512
    nrow = seq_len // ts
    sb0 = row0 // seq_len
    rb0 = row0 // ts
    in_specs = [pl.BlockSpec((2, ts, seq_len), lambda b, i, j: (0, i, 0)),
                pl.BlockSpec((2, seq_len, tn), lambda b, i, j: (0, sb0 + b, j))]
    args = [stab, r]
    aliases = {}
    if prev_out is not None:
        in_specs.append(pl.BlockSpec(memory_space=pl.ANY))
        args.append(prev_out)
        aliases = {2: 0}
    return pl.pallas_call(
        _seq_dft_kernel,
        grid=(n_seq, nrow, FOURIER_WIDTH // tn),
        in_specs=in_specs,
        out_specs=pl.BlockSpec((ts, tn), lambda b, i, j: (rb0 + b * nrow + i, j)),
        out_shape=jax.ShapeDtypeStruct((t, FOURIER_WIDTH), out_dtype),
        input_output_aliases=aliases,
        compiler_params=_cparams(("parallel", "arbitrary", "arbitrary")),
        name="seq_dft",
    )(*args)


def _attn_kernel(q_ref, k_ref, v_ref, *rest):
    o_ref = rest[-1]
    k = k_ref[0].astype(BF16)
    v = v_ref[0].astype(BF16)
    scale = HEAD_DIM ** -0.5
    for g in range(GQA_GROUP):
        q = q_ref[0, :, g * HEAD_DIM:(g + 1) * HEAD_DIM]
        s = _bdot_nt(q, k)
        e = jnp.exp((s - jnp.max(s, axis=-1, keepdims=True)) * scale)
        o = _bdot(e, v) / jnp.sum(e, axis=-1, keepdims=True)
        o_ref[:, g * HEAD_DIM:(g + 1) * HEAD_DIM] = o.astype(o_ref.dtype)


def attention(q, k, v, *, total_rows, row0, prev_out=None, tq=256, out_dtype=BF16):
    b, sq, _ = q.shape
    sk = k.shape[1]
    tq = min(tq, sq)
    assert row0 % tq == 0
    gw = GQA_GROUP * HEAD_DIM
    nq = sq // tq
    rb0 = row0 // tq
    in_specs = [pl.BlockSpec((1, tq, gw), lambda bb, h, i: (bb, i, h)),
                pl.BlockSpec((1, sk, HEAD_DIM), lambda bb, h, i: (bb, 0, h)),
                pl.BlockSpec((1, sk, HEAD_DIM), lambda bb, h, i: (bb, 0, h))]
    args = [q, k, v]
    aliases = {}
    if prev_out is not None:
        in_specs.append(pl.BlockSpec(memory_space=pl.ANY))
        args.append(prev_out)
        aliases = {3: 0}
    return pl.pallas_call(
        _attn_kernel,
        grid=(b, KV_HEADS, nq),
        in_specs=in_specs,
        out_specs=pl.BlockSpec((tq, gw), lambda bb, h, i: (rb0 + bb * nq + i, h)),
        out_shape=jax.ShapeDtypeStruct((total_rows, ATTN_WIDTH), out_dtype),
        input_output_aliases=aliases,
        compiler_params=_cparams(("parallel", "parallel", "arbitrary")),
        name="attention",
    )(*args)


N_PROB = 4
DELTA_VMEM_BYTES_PER_ROW_HEAD = 14 * 1024


def _tri_masks(c):
    ri = lax.broadcasted_iota(jnp.int32, (c, c), 0)
    ci = lax.broadcasted_iota(jnp.int32, (c, c), 1)
    return ri, ci


def _unit_tri_inverse(low, ri, ci):
    c = low.shape[-1]

    def same_block(size):
        sh = int(math.log2(size))
        return (ri >> sh) == (ci >> sh)

    leaf = 4
    dg = jnp.where(same_block(leaf), low, 0.0)
    imd = jnp.where(ri == ci, 1.0, 0.0) - dg
    inv = imd + _bmm(imd, _bmm(dg, dg))
    size = leaf
    while size < c:
        off = jnp.where(same_block(2 * size) & jnp.logical_not(same_block(size)), low, 0.0)
        inv = inv - _bmm(inv, _bmm(off, inv))
        size *= 2
    return inv


def _delta_kernel(*refs, n_chunks, unroll_a, kh_step, has_prev):
    n_in = 9 + int(has_prev)
    q_ref, k_ref, v_ref, z_ref, gain_ref, cols_ref, rows_ref, sf0_ref, sb0_ref = refs[:9]
    on_ref, sf_ref, sb_ref, u_s, wq_s, akt_s, gl_s, o_s = refs[n_in:]
    c = DN_CHUNK
    dk = DN_K_DIM
    dv = DN_V_DIM
    ri, ci = _tri_masks(c)

    ca = unroll_a
    n_chain = kh_step * N_PROB

    def vhead(pp):
        return 2 * (pp // N_PROB) + pp % 2

    def body_a(g0, carry):
        rows = pl.ds(pl.multiple_of(g0 * (ca * c), ca * c), ca * c)
        chs = pl.ds(g0 * ca, ca)
        ks, qs, qks = [], [], []
        lows, rhss, decays, egs, glasts, gcols = [], [], [], [], [], []
        for hq in range(kh_step):
            k = k_ref[rows, hq * dk:(hq + 1) * dk].astype(F32).reshape(ca, c, dk)
            q = q_ref[rows, hq * dk:(hq + 1) * dk].astype(F32).reshape(ca, c, dk)
            kk = _bmm_nt(k, k)
            ks.append(k)
            qs.append(q)
            qks.append(_bmm_nt(q, k))
            cols = cols_ref[0, hq, rows, :].reshape(ca, c, 2 * N_PROB)
            rws = rows_ref[0, hq, chs]
            for p in range(N_PROB):
                head, reverse = p % 2, p >= 2
                beta = cols[:, :, p:p + 1]
                gcol = cols[:, :, N_PROB + p:N_PROB + p + 1]
                grow = rws[:, p:p + 1, :]
                glast = grow[:, :, 0:1] if reverse else grow[:, :, c - 1:c]
                incl = (ri <= ci) if reverse else (ri >= ci)
                strict = (ri < ci) if reverse else (ri > ci)
                decay = jnp.where(incl, jnp.exp(jnp.where(incl, gcol - grow, 0.0)), 0.0)
                eg = jnp.exp(gcol)
                vh = 2 * hq + head
                v = v_ref[rows, vh * dv:(vh + 1) * dv].astype(F32).reshape(ca, c, dv)
                lows.append(jnp.where(strict, kk * beta * decay, 0.0))
                rhss.append(jnp.concatenate([v * beta, k * (beta * eg)], axis=-1))
                decays.append(decay)
                egs.append(eg)
                glasts.append(glast)
                gcols.append(gcol)
        inv = _unit_tri_inverse(jnp.concatenate(lows, axis=0), ri, ci)
        sol = _bmm(inv, jnp.concatenate(rhss, axis=0))
        for pp in range(n_chain):
            k, q, qk = ks[pp // N_PROB], qs[pp // N_PROB], qks[pp // N_PROB]
            sp = sol[pp * ca:(pp + 1) * ca]
            u_s[pp, rows, :] = sp[:, :, :dv].reshape(ca * c, dv)
            wq_s[pp, chs, 0:c, :] = sp[:, :, dv:].astype(BF16)
            wq_s[pp, chs, c:2 * c, :] = (q * egs[pp]).astype(BF16)
            akt_s[pp, chs, 0:c, :] = (qk * decays[pp]).astype(BF16)
            akt_s[pp, chs, c:c + dk, :] = jnp.swapaxes(k * jnp.exp(glasts[pp] - gcols[pp]), 1, 2).astype(BF16)
            gl_s[pp, chs] = jnp.broadcast_to(jnp.exp(glasts[pp]), (ca, 8, dv))
        return carry

    lax.fori_loop(0, n_chunks // ca, body_a, 0)
    o_s[...] = jnp.zeros_like(o_s)

    def body_b(i, states):
        chs = [(n_chunks - 1 - i) if pp % N_PROB >= 2 else i for pp in range(n_chain)]
        rows = [pl.ds(pl.multiple_of(ch * c, c), c) for ch in chs]
        ws = [_bdot(wq_s[pp, chs[pp]], states[pp]) for pp in range(n_chain)]
        vn = [u_s[pp, rows[pp], :] - ws[pp][0:c] for pp in range(n_chain)]
        av = [_bdot(akt_s[pp, chs[pp]], vn[pp]) for pp in range(n_chain)]
        new_states = []
        for pp in range(n_chain):
            head = vhead(pp)
            o_s[rows[pp], head * dv:(head + 1) * dv] += ws[pp][c:2 * c] + av[pp][0:c]
            new_states.append(states[pp] * gl_s[pp, chs[pp]][0:1, :] + av[pp][c:c + dk])
        return tuple(new_states)

    init = tuple((sb0_ref if pp % N_PROB >= 2 else sf0_ref)[0, vhead(pp)] for pp in range(n_chain))
    fin = lax.fori_loop(0, n_chunks, body_b, init)
    for pp in range(n_chain):
        (sb_ref if pp % N_PROB >= 2 else sf_ref)[0, vhead(pp)] = fin[pp]
    for head in range(2 * kh_step):
        hs = slice(head * dv, (head + 1) * dv)
        o = o_s[:, hs]
        z = z_ref[:, hs].astype(F32)
        on = o * lax.rsqrt(jnp.mean(o * o, axis=-1, keepdims=True) + EPS) * gain_ref[...]
        on_ref[:, hs] = (on * (z * jax.nn.sigmoid(z))).astype(on_ref.dtype)


def delta_mixer(qkv, p, beta, g, gain, s_f0, s_b0, *, n_seq, seq_len, row0, prev_out=None):
    b, t = n_seq, seq_len
    tt = qkv.shape[0]
    h = DN_V_HEADS
    kh = DN_QK_HEADS
    rep = h // kh
    assert rep == 2 and row0 % t == 0
    rb0 = row0 // t
    c = DN_CHUNK
    nc = t // c
    gch = g.reshape(b, nc, c, 2, h)
    gcf = jnp.cumsum(gch[:, :, :, 0], axis=2).reshape(b, t, h)
    gcb = jnp.flip(jnp.cumsum(jnp.flip(gch[:, :, :, 1], axis=2), axis=2), axis=2).reshape(b, t, h)
    cols = jnp.stack([beta[:, :, 0], beta[:, :, 1], gcf, gcb], axis=2).reshape(b, t, 4, kh, rep)
    cols = jnp.transpose(cols, (0, 3, 1, 2, 4)).reshape(b, kh, t, 4 * rep)
    rows = jnp.stack([gcf, gcb], axis=2).reshape(b, nc, c, 2, kh, rep)
    rows = jnp.transpose(rows, (0, 4, 1, 3, 5, 2)).reshape(b, kh, nc, N_PROB, c)
    kh_step = max(n for n in (4, 2, 1) if n == 1 or t * n * DELTA_VMEM_BYTES_PER_ROW_HEAD <= VMEM_LIMIT // 2)
    stspec = pl.BlockSpec((1, rep * kh_step, DN_K_DIM, DN_V_DIM), lambda bb, hh: (bb, hh, 0, 0))
    unroll_a = math.gcd(nc, 8)
    kw = kh_step * DN_K_DIM
    vw = kh_step * rep * DN_V_DIM
    k_blk0 = DN_QK_WIDTH // kw
    v_blk0 = 2 * DN_QK_WIDTH // vw
    z_blk0 = DN_CONV_CH // vw
    in_specs = [pl.BlockSpec((t, kw), lambda bb, hh: (rb0 + bb, hh)),
                pl.BlockSpec((t, kw), lambda bb, hh: (rb0 + bb, k_blk0 + hh)),
                pl.BlockSpec((t, vw), lambda bb, hh: (rb0 + bb, v_blk0 + hh)),
                pl.BlockSpec((t, vw), lambda bb, hh: (rb0 + bb, z_blk0 + hh)),
                pl.BlockSpec((1, DN_V_DIM), lambda bb, hh: (0, 0)),
                pl.BlockSpec((1, kh_step, t, 4 * rep), lambda bb, hh: (bb, hh, 0, 0)),
                pl.BlockSpec((1, kh_step, nc, N_PROB, c), lambda bb, hh: (bb, hh, 0, 0, 0)),
                stspec, stspec]
    args = [qkv, qkv, qkv, p, gain.reshape(1, DN_V_DIM), cols, rows, s_f0, s_b0]
    aliases = {}
    if prev_out is not None:
        in_specs.append(pl.BlockSpec(memory_space=pl.ANY))
        args.append(prev_out)
        aliases = {len(args) - 1: 0}
    on, sf, sb = pl.pallas_call(
        functools.partial(_delta_kernel, n_chunks=nc, unroll_a=unroll_a, kh_step=kh_step,
                          has_prev=prev_out is not None),
        grid=(b, kh // kh_step),
        in_specs=in_specs,
        out_specs=[pl.BlockSpec((t, vw), lambda bb, hh: (rb0 + bb, hh)), stspec, stspec],
        out_shape=[jax.ShapeDtypeStruct((tt, DN_V_WIDTH), BF16),
                   jax.ShapeDtypeStruct((b, h, DN_K_DIM, DN_V_DIM), F32),
                   jax.ShapeDtypeStruct((b, h, DN_K_DIM, DN_V_DIM), F32)],
        scratch_shapes=[pltpu.VMEM((kh_step * N_PROB, t, DN_V_DIM), F32),
                        pltpu.VMEM((kh_step * N_PROB, nc, 2 * c, DN_K_DIM), BF16),
                        pltpu.VMEM((kh_step * N_PROB, nc, c + DN_K_DIM, c), BF16),
                        pltpu.VMEM((kh_step * N_PROB, nc, 8, DN_V_DIM), F32),
                        pltpu.VMEM((t, vw), F32)],
        input_output_aliases=aliases,
        compiler_params=_cparams(("parallel", "parallel")),
        name="delta_rule",
    )(*args)
    return on, sf, sb


def _dn_prep_kernel(*refs, has_prev):
    p_ref, w_ref = refs[:2]
    o_ref = refs[2 + int(has_prev)]
    j = pl.program_id(1)
    x = p_ref[...].astype(F32)
    s, cb = x.shape
    t = lax.broadcasted_iota(jnp.int32, (s, cb), 0)
    w = w_ref[...]
    y = (w[0:1] * jnp.where(t >= 2, pltpu.roll(x, 2, 0), 0.0) + w[1:2] * jnp.where(t >= 1, pltpu.roll(x, 1, 0), 0.0)
         + w[2:3] * x + w[3:4] * jnp.where(t < s - 1, pltpu.roll(x, s - 1, 0), 0.0))
    y = y * jax.nn.sigmoid(y)
    qk_blocks = DN_QK_WIDTH // cb
    for hh in range(cb // DN_K_DIM):
        seg = y[:, hh * DN_K_DIM:(hh + 1) * DN_K_DIM]
        inv = lax.rsqrt(jnp.sum(seg * seg, axis=-1, keepdims=True) + EPS)
        mult = jnp.where(j < qk_blocks, inv * (DN_K_DIM ** -0.5), jnp.where(j < 2 * qk_blocks, inv, 1.0))
        o_ref[:, hh * DN_K_DIM:(hh + 1) * DN_K_DIM] = (seg * mult).astype(o_ref.dtype)


def dn_prep(p, conv_w, *, n_seq, seq_len, row0, prev_out=None, cb=512):
    tt = p.shape[0]
    assert row0 % seq_len == 0 and DN_QK_WIDTH % cb == 0
    rb0 = row0 // seq_len
    in_specs = [pl.BlockSpec((seq_len, cb), lambda b, j: (rb0 + b, j)),
                pl.BlockSpec((conv_w.shape[0], cb), lambda b, j: (0, j))]
    args = [p, conv_w]
    aliases = {}
    if prev_out is not None:
        in_specs.append(pl.BlockSpec(memory_space=pl.ANY))
        args.append(prev_out)
        aliases = {2: 0}
    return pl.pallas_call(
        functools.partial(_dn_prep_kernel, has_prev=prev_out is not None),
        grid=(n_seq, DN_CONV_CH // cb),
        in_specs=in_specs,
        out_specs=pl.BlockSpec((seq_len, cb), lambda b, j: (rb0 + b, j)),
        out_shape=jax.ShapeDtypeStruct((tt, DN_CONV_CH), BF16),
        input_output_aliases=aliases,
        compiler_params=_cparams(("parallel", "arbitrary")),
        name="dn_prep",
    )(*args)


def _peer_route_kernel(u_ref, wq_ref, keys_ref, nc_ref, re_ref, v1_ref, v2_ref):
    kk = PEER_TOPK
    neg = -jnp.inf
    for h in range(PEER_HEADS):
        r0 = h * 2 * PEER_HALF
        qh = jnp.dot(u_ref[...], wq_ref[:, r0:r0 + 2 * PEER_HALF], preferred_element_type=F32)
        s1 = _bdot_nt(keys_ref[0], qh[:, :PEER_HALF])
        s2 = _bdot_nt(keys_ref[1], qh[:, PEER_HALF:])
        sc = s1
        for r in range(kk):
            m = jnp.max(sc, axis=0, keepdims=True)
            v1_ref[r:r + 1, :] = m
            sc = jnp.where(sc == m, neg, sc)
        sc = s2
        rank2 = jnp.full(s2.shape, float(kk), F32)
        for r in range(kk):
            m = jnp.max(sc, axis=0, keepdims=True)
            v2_ref[r:r + 1, :] = m
            hit = sc == m
            rank2 = jnp.where(hit, float(r), rank2)
            sc = jnp.where(hit, neg, sc)
        v2 = v2_ref[...]
        cand = jnp.concatenate([v1_ref[0:1, :] + v2] + [v1_ref[a:a + 1, :] + v2[0:8] for a in range(1, kk)],
                               axis=0)
        top = cand[0:1, :]
        work = cand
        tau = top
        for r in range(kk):
            tau = jnp.max(work, axis=0, keepdims=True)
            work = jnp.where(work == tau, neg, work)
        sel = cand >= tau
        z = jnp.sum(jnp.where(sel, jnp.exp(cand - top), 0.0), axis=0, keepdims=True)
        one = jnp.where(sel, 1.0, 0.0)
        n1 = jnp.zeros(s1.shape, F32)
        for a in range(kk):
            lo, hi = (0, kk) if a == 0 else (kk + 8 * (a - 1), kk + 8 * a)
            cnt = jnp.sum(one[lo:hi], axis=0, keepdims=True)
            n1 = jnp.where(s1 == v1_ref[a:a + 1, :], cnt, n1)
        nc_ref[0, h] = n1
        nc_ref[1, h] = (0.5 * jnp.exp(s1 - v1_ref[0:1, :])) / z
        re_ref[0, h] = rank2.astype(BF16)
        re_ref[1, h] = jnp.exp(s2 - v2[0:1, :]).astype(BF16)


def peer_route(u, w_q, sub_keys, *, tb):
    t, d = u.shape
    d2 = w_q.shape[1]
    tb = min(tb, t)
    tspec = pl.BlockSpec((2, PEER_HEADS, PEER_N_KEYS, tb), lambda j: (0, 0, 0, j))
    return pl.pallas_call(
        _peer_route_kernel,
        grid=(t // tb,),
        in_specs=[pl.BlockSpec((tb, d), lambda j: (j, 0)),
                  pl.BlockSpec((d, d2), lambda j: (0, 0), pipeline_mode=pl.Buffered(1)),
                  pl.BlockSpec((2, PEER_N_KEYS, PEER_HALF), lambda j: (0, 0, 0))],
        out_specs=[tspec, tspec],
        out_shape=[jax.ShapeDtypeStruct((2, PEER_HEADS, PEER_N_KEYS, t), F32),
                   jax.ShapeDtypeStruct((2, PEER_HEADS, PEER_N_KEYS, t), BF16)],
        scratch_shapes=[pltpu.VMEM((PEER_TOPK, tb), F32), pltpu.VMEM((PEER_TOPK, tb), F32)],
        compiler_params=_cparams(("parallel",)),
        name="peer_route",
    )(u, w_q, sub_keys)


def _cast_kernel(x_ref, o_ref):
    o_ref[...] = x_ref[0].astype(o_ref.dtype)


def cast_layer_bf16(w, layer, *, tr=1024):
    _, r, c = w.shape
    return pl.pallas_call(
        _cast_kernel,
        grid=(r // tr,),
        in_specs=[pl.BlockSpec((1, tr, c), lambda i: (layer, i, 0))],
        out_specs=pl.BlockSpec((tr, c), lambda i: (i, 0)),
        out_shape=jax.ShapeDtypeStruct((r, c), BF16),
        compiler_params=_cparams(("parallel",)),
        name="cast_bf16",
    )(w)


def _transpose_cast_kernel(x_ref, o_ref):
    o_ref[...] = jnp.transpose(x_ref[0]).astype(o_ref.dtype)


def transpose_layer_bf16(w, layer, *, tr=512):
    _, r, c = w.shape
    return pl.pallas_call(
        _transpose_cast_kernel,
        grid=(r // tr,),
        in_specs=[pl.BlockSpec((1, tr, c), lambda i: (layer, i, 0))],
        out_specs=pl.BlockSpec((c, tr), lambda i: (0, i)),
        out_shape=jax.ShapeDtypeStruct((c, r), BF16),
        compiler_params=_cparams(("parallel",)),
        name="transpose_bf16",
    )(w)


def _peer_dense_kernel(u_ref, wd_ref, wut_ref, nc_ref, re_ref, x_ref, g_ref, o_ref, acc_s, ut_s, *,
                       rows_per_block, n_blocks):
    i = pl.program_id(1)

    @pl.when(i == 0)
    def _():
        acc_s[...] = jnp.zeros_like(acc_s)
        ut_s[...] = jnp.transpose(u_ref[...].astype(F32)).astype(BF16)

    tb = u_ref.shape[0]
    cw = min(PEER_COL_CHUNK, tb)
    e1s = pl.ds(pl.multiple_of(i * rows_per_block, rows_per_block), rows_per_block)
    n1_all = [nc_ref[0, h, e1s, :] for h in range(PEER_HEADS)]
    c1_all = [nc_ref[1, h, e1s, :] for h in range(PEER_HEADS)]
    chunks = [slice(c0, c0 + cw) for c0 in range(0, tb, cw)]
    gate_rows = [[None] * rows_per_block for _ in chunks]
    for r in range(rows_per_block):
        for h in range(PEER_HEADS):
            n1 = jnp.broadcast_to(n1_all[h][r:r + 1, :], (PACK, tb)).astype(BF16)
            c1 = jnp.broadcast_to(c1_all[h][r:r + 1, :], (PACK, tb)).astype(BF16)
            for ic, cs in enumerate(chunks):
                rank2 = re_ref[0, h, :, cs].reshape(PEER_N_KEYS // PACK, PACK, cw)
                e2w = re_ref[1, h, :, cs].reshape(PEER_N_KEYS // PACK, PACK, cw)
                wgt = jnp.where(rank2 < n1[None, :, cs], e2w * c1[None, :, cs], jnp.zeros((), BF16))
                gate_rows[ic][r] = wgt if gate_rows[ic][r] is None else gate_rows[ic][r] + wgt
    hpres = [jnp.dot(wd_ref[...], ut_s[:, cs], preferred_element_type=F32) for cs in chunks]
    for cs, hpre, rows_c in zip(chunks, hpres, gate_rows):
        half_gate = jnp.concatenate([g.reshape(PEER_N_KEYS, cw) for g in rows_c], axis=0)
        act = (hpre * (1.0 + lax.erf(hpre * (2.0 ** -0.5)))).astype(BF16) * half_gate
        acc_s[:, cs] += jnp.dot(wut_ref[...], act, preferred_element_type=F32)

    @pl.when(i == n_blocks - 1)
    def _():
        o_ref[...] = x_ref[...] + g_ref[0] * jnp.transpose(acc_s[...])


def peer_dense(u, w_down, w_up_t, nc, re, x, gates, group_of_tile, *, tb, eb):
    t, d = u.shape
    ne = w_down.shape[0]
    tb = min(tb, t)
    assert eb % PEER_N_KEYS == 0 and t % tb == 0
    nblk = ne // eb
    once = pl.Buffered(1)
    tspec = pl.BlockSpec((2, PEER_HEADS, PEER_N_KEYS, tb), lambda j, i: (0, 0, 0, j), pipeline_mode=once)
    return pl.pallas_call(
        functools.partial(_peer_dense_kernel, rows_per_block=eb // PEER_N_KEYS, n_blocks=nblk),
        grid=(t // tb, nblk),
        in_specs=[pl.BlockSpec((tb, d), lambda j, i: (j, 0), pipeline_mode=once),
                  pl.BlockSpec((eb, d), lambda j, i: (i, 0)),
                  pl.BlockSpec((d, eb), lambda j, i: (0, i)),
                  tspec, tspec,
                  pl.BlockSpec((tb, d), lambda j, i: (j, 0), pipeline_mode=once),
                  pl.BlockSpec((1, 1, d), lambda j, i: (group_of_tile(j, tb), 0, 0))],
        out_specs=pl.BlockSpec((tb, d), lambda j, i: (j, 0)),
        out_shape=jax.ShapeDtypeStruct((t, d), F32),
        scratch_shapes=[pltpu.VMEM((d, tb), F32), pltpu.VMEM((d, tb), BF16)],
        compiler_params=_cparams(("parallel", "arbitrary")),
        name="peer_dense",
    )(u, w_down, w_up_t, nc, re, x, gates)


def peer_residual(x, u, layer, w_q, sub_keys, w_down, w_up, gates, group_of_tile, *, tb=512, eb=1024):
    nc, re = peer_route(u, cast_layer_bf16(w_q, layer), sub_keys[layer], tb=256)
    return peer_dense(u, cast_layer_bf16(w_down, layer), transpose_layer_bf16(w_up, layer), nc, re,
                      x, gates, group_of_tile, tb=tb, eb=eb)


def _rope_tables(n_prompt_rows, n_seq, seq_len):
    n_rows = seq_len // GRID_W
    rows = jnp.repeat(jnp.arange(n_rows), GRID_W).astype(F32)
    cols = jnp.tile(jnp.arange(GRID_W), n_rows).astype(F32)
    n_freq = ROPE_AXIS_DIM // 2
    inv = ROPE_THETA ** (-jnp.arange(n_freq, dtype=F32) / n_freq)
    ar, ac = rows[:, None] * inv, cols[:, None] * inv
    cos_t = jnp.concatenate([jnp.cos(ar), jnp.cos(ar), jnp.cos(ac), jnp.cos(ac)], axis=-1)
    sin_t = jnp.concatenate([-jnp.sin(ar), jnp.sin(ar), -jnp.sin(ac), jnp.sin(ac)], axis=-1)
    cos_t = jnp.concatenate([jnp.ones((n_prompt_rows, HEAD_DIM), F32)] + [cos_t] * n_seq, axis=0)
    sin_t = jnp.concatenate([jnp.zeros((n_prompt_rows, HEAD_DIM), F32)] + [sin_t] * n_seq, axis=0)
    return cos_t, sin_t


def _qk_prep_kernel(pq_ref, pk_ref, qg_ref, kg_ref, cos_ref, sin_ref, q_ref, k_ref):
    cos_t = cos_ref[...]
    sin_t = sin_ref[...]
    lane = lax.broadcasted_iota(jnp.int32, cos_t.shape, 1)
    first = (lane % (ROPE_AXIS_DIM)) < (ROPE_AXIS_DIM // 2)
    quarter = ROPE_AXIS_DIM // 2

    def norm_rope(x, gain):
        y = x * lax.rsqrt(jnp.mean(x * x, axis=-1, keepdims=True) + EPS) * gain
        swapped = jnp.where(first, pltpu.roll(y, HEAD_DIM - quarter, 1), pltpu.roll(y, quarter, 1))
        return y * cos_t + swapped * sin_t

    for h in range(N_HEADS):
        hs = slice(h * HEAD_DIM, (h + 1) * HEAD_DIM)
        q_ref[:, hs] = norm_rope(pq_ref[:, hs], qg_ref[...]).astype(q_ref.dtype)
    for h in range(KV_HEADS):
        hs = slice(h * HEAD_DIM, (h + 1) * HEAD_DIM)
        k_ref[:, hs] = norm_rope(pk_ref[:, hs], kg_ref[...])


def qk_prep(p, q_gain, k_gain, cos_t, sin_t, *, tm):
    t = p.shape[0]
    tm = min(tm, t)
    assert FOURIER_WIDTH % ATTN_WIDTH == 0 and (FOURIER_WIDTH + ATTN_WIDTH) % KV_WIDTH == 0
    q_blk, k_blk = FOURIER_WIDTH // ATTN_WIDTH, (FOURIER_WIDTH + ATTN_WIDTH) // KV_WIDTH
    rspec = pl.BlockSpec((tm, HEAD_DIM), lambda i: (i, 0))
    gspec = pl.BlockSpec((1, HEAD_DIM), lambda i: (0, 0))
    return pl.pallas_call(
        _qk_prep_kernel,
        grid=(t // tm,),
        in_specs=[pl.BlockSpec((tm, ATTN_WIDTH), lambda i: (i, q_blk)),
                  pl.BlockSpec((tm, KV_WIDTH), lambda i: (i, k_blk)), gspec, gspec, rspec, rspec],
        out_specs=[pl.BlockSpec((tm, ATTN_WIDTH), lambda i: (i, 0)), pl.BlockSpec((tm, KV_WIDTH), lambda i: (i, 0))],
        out_shape=[jax.ShapeDtypeStruct((t, ATTN_WIDTH), BF16), jax.ShapeDtypeStruct((t, KV_WIDTH), F32)],
        compiler_params=_cparams(("parallel",)),
        name="qk_prep",
    )(p, p, q_gain.reshape(1, HEAD_DIM), k_gain.reshape(1, HEAD_DIM), cos_t, sin_t)


def kernel(x_prompt, x_sample, cache_k, cache_v, state_fwd, state_bwd, c, c_ctx, ada_w, ada_b, norm1, norm2,
           af_w_in, af_q_norm, af_k_norm, af_w_out, dn_w_in, dn_conv_w, dn_a_log, dn_dt_bias, dn_o_norm,
           dn_w_out, peer_w_q, peer_sub_keys, peer_w_down, peer_w_up, final_norm):
    nb, seq, d = x_prompt.shape
    db, dseq, _ = x_sample.shape
    depth = ada_w.shape[0]
    tp = nb * seq
    ts = db * dseq
    tt = tp + ts
    tm = math.gcd(math.gcd(tp, dseq), 1024)

    def group_of_tile(i, tile):
        r = i * tile
        return jnp.where(r < tp, 0, 1 + (r - tp) // dseq)

    conds = jnp.concatenate([c_ctx[None], c, jnp.zeros((8 - 1 - db, d), F32)], axis=0)
    mods = ada_all(conds, ada_w, ada_b)
    mods = mods.reshape(depth, 8, 6, d)

    x = jnp.concatenate([x_prompt.reshape(tp, d), x_sample.reshape(ts, d)], axis=0)
    new_k, new_v, new_sf, new_sb = [], [], [], []
    for i in range(depth):
        j = i // 2
        md = [mods[i, :, n][:, None, :] for n in range(6)]
        sh1, sc1, g1, sh2, sc2, g2 = md
        u = modulate(x, norm1[i], sh1, sc1, group_of_tile, tm=tm)
        if i % 2 == 0:
            p = proj(u, af_w_in, layer=j, tm=tm, tn=512, name="af_in")
            cos_t, sin_t = _rope_tables(tp, db, dseq)
            q, k = qk_prep(p, af_q_norm[j], af_k_norm[j], cos_t, sin_t, tm=tm)
            v = p[:, FOURIER_WIDTH + ATTN_WIDTH + KV_WIDTH:]
            new_k.append(k[:tp].reshape(nb, seq, KV_HEADS, HEAD_DIM))
            new_v.append(v[:tp].reshape(nb, seq, KV_HEADS, HEAD_DIM))
            keys = jnp.concatenate([k[tp:].reshape(db, dseq, KV_WIDTH),
                                    cache_k[:, j].reshape(db, -1, KV_WIDTH)], axis=1)
            vals = jnp.concatenate([v[tp:].reshape(db, dseq, KV_WIDTH),
                                    cache_v[:, j].reshape(db, -1, KV_WIDTH)], axis=1)
            att = attention(q[:tp].reshape(nb, seq, ATTN_WIDTH), k[:tp].reshape(nb, seq, KV_WIDTH),
                            v[:tp].reshape(nb, seq, KV_WIDTH), total_rows=tt, row0=0)
            att = attention(q[tp:].reshape(db, dseq, ATTN_WIDTH), keys, vals, total_rows=tt, row0=tp, prev_out=att)
            r = chan_dft(p, tm=tm)
            fou = seq_dft(r, n_seq=nb, seq_len=seq, row0=0)
            fou = seq_dft(r, n_seq=db, seq_len=dseq, row0=tp, prev_out=fou)
            x = proj_residual([fou, att], af_w_out, x, g1, group_of_tile, layer=j, tm=tm, tn=512, name="af_out")
        else:
            p = proj(u, dn_w_in, layer=j, tm=2 * tm, tn=512, ncols=DN_CONV_CH + DN_V_WIDTH, out_dtype=BF16,
                     name="dn_in")
            ba = proj(u, dn_w_in, layer=j, tm=tm, tn=128, col0=DN_CONV_CH + DN_V_WIDTH, name="dn_in_ba")
            beta_all = jax.nn.sigmoid(ba.reshape(tt, 2, 2, DN_V_HEADS)[:, 0])
            g_all = -jnp.exp(dn_a_log[j]) * jax.nn.softplus(ba.reshape(tt, 2, 2, DN_V_HEADS)[:, 1] + dn_dt_bias[j])
            qkv = dn_prep(p, dn_conv_w[j], n_seq=nb, seq_len=seq, row0=0)
            qkv = dn_prep(p, dn_conv_w[j], n_seq=db, seq_len=dseq, row0=tp, prev_out=qkv)
            zeros = jnp.zeros((nb, DN_V_HEADS, DN_K_DIM, DN_V_DIM), F32)
            on, s_f, s_b = delta_mixer(qkv, p, beta_all[:tp].reshape(nb, seq, 2, DN_V_HEADS),
                                       g_all[:tp].reshape(nb, seq, 2, DN_V_HEADS), dn_o_norm[j], zeros, zeros,
                                       n_seq=nb, seq_len=seq, row0=0)
            new_sf.append(s_f)
            new_sb.append(s_b)
            on, _, _ = delta_mixer(qkv, p, beta_all[tp:].reshape(db, dseq, 2, DN_V_HEADS),
                                   g_all[tp:].reshape(db, dseq, 2, DN_V_HEADS), dn_o_norm[j],
                                   state_fwd[:, j], state_bwd[:, j], n_seq=db, seq_len=dseq, row0=tp, prev_out=on)
            x = proj_residual([on], dn_w_out, x, g1, group_of_tile, layer=j, tm=tm, tn=256, name="dn_out")
        u2 = modulate(x, norm2[i], sh2, sc2, group_of_tile, tm=tm)
        x = peer_residual(x, u2, i, peer_w_q, peer_sub_keys, peer_w_down, peer_w_up, g2, group_of_tile)
    return (rmsnorm_rows(x, final_norm, row0=0, nrows=tp, tm=tm).reshape(nb, seq, d),
            rmsnorm_rows(x, final_norm, row0=tp, nrows=ts, tm=tm).reshape(db, dseq, d),
            jnp.stack(new_k, axis=1), jnp.stack(new_v, axis=1),
            jnp.stack(new_sf, axis=1), jnp.stack(new_sb, axis=1))
```

```python
import functools
import math

import jax
import jax.numpy as jnp
import numpy as np
from jax import lax
from jax.experimental import pallas as pl
from jax.experimental.pallas import tpu as pltpu

F32 = jnp.float32
BF16 = jnp.bfloat16

EPS = 1e-6
HEAD_DIM = 128
N_HEADS = 8
KV_HEADS = 2
GQA_GROUP = N_HEADS // KV_HEADS
FOURIER_GROUPS = 4
FOURIER_GROUP_DIM = 256
FOURIER_WIDTH = FOURIER_GROUPS * FOURIER_GROUP_DIM
ATTN_WIDTH = N_HEADS * HEAD_DIM
KV_WIDTH = KV_HEADS * HEAD_DIM
GRID_W = 64
ROPE_THETA = 10000.0
ROPE_AXIS_DIM = HEAD_DIM // 2

DN_QK_HEADS = 16
DN_V_HEADS = 32
DN_K_DIM = 128
DN_V_DIM = 128
DN_QK_WIDTH = DN_QK_HEADS * DN_K_DIM
DN_V_WIDTH = DN_V_HEADS * DN_V_DIM
DN_CONV_CH = 2 * DN_QK_WIDTH + DN_V_WIDTH
DN_CHUNK = 64

PEER_HEADS = 8
PEER_N_KEYS = 128
PEER_HALF = 128
PEER_TOPK = 16
PEER_COL_CHUNK = 256
PACK = 16

VMEM_LIMIT = 56 * 1024 * 1024


def _cparams(sem):
    return pltpu.CompilerParams(dimension_semantics=sem, vmem_limit_bytes=VMEM_LIMIT)


def _bdot(a, b):
    return jnp.dot(a.astype(BF16), b.astype(BF16), preferred_element_type=F32)


def _bdot_nt(a, b):
    return lax.dot_general(a.astype(BF16), b.astype(BF16), (((1,), (1,)), ((), ())),
                           preferred_element_type=F32)


def _bmm(a, b):
    return lax.dot_general(a.astype(BF16), b.astype(BF16), (((2,), (1,)), ((0,), (0,))),
                           preferred_element_type=F32)


def _bmm_nt(a, b):
    return lax.dot_general(a.astype(BF16), b.astype(BF16), (((2,), (2,)), ((0,), (0,))),
                           preferred_element_type=F32)


def _proj_kernel(x_ref, w_ref, o_ref):
    o_ref[...] = _bdot(x_ref[...], w_ref[0]).astype(o_ref.dtype)


def proj(x, w, *, layer=0, tm, tn, col0=0, ncols=None, out_dtype=F32, name="proj"):
    m, k = x.shape
    ncols = w.shape[2] - col0 if ncols is None else ncols
    tm = min(tm, m)
    tn = min(tn, ncols)
    assert m % tm == 0 and ncols % tn == 0 and col0 % tn == 0
    cb = col0 // tn
    return pl.pallas_call(
        _proj_kernel,
        grid=(m // tm, ncols // tn),
        in_specs=[pl.BlockSpec((tm, k), lambda i, j: (i, 0)),
                  pl.BlockSpec((1, k, tn), lambda i, j: (layer, 0, j + cb))],
        out_specs=pl.BlockSpec((tm, tn), lambda i, j: (i, j)),
        out_shape=jax.ShapeDtypeStruct((m, ncols), out_dtype),
        compiler_params=_cparams(("parallel", "arbitrary")),
        name=name,
    )(x, w)


def _proj_res_kernel(*refs, n_parts):
    x_refs, w_refs = refs[:n_parts], refs[n_parts:2 * n_parts]
    res_ref, gate_ref, o_ref = refs[2 * n_parts:]
    acc = _bdot(x_refs[0][...], w_refs[0][0])
    for x_ref, w_ref in zip(x_refs[1:], w_refs[1:]):
        acc = acc + _bdot(x_ref[...], w_ref[0])
    o_ref[...] = res_ref[...] + gate_ref[0] * acc


def proj_residual(xs, w, res, gates, group_of_tile, *, layer=0, tm, tn, name="proj_res"):
    m = xs[0].shape[0]
    n = w.shape[2]
    tm = min(tm, m)
    tn = min(tn, n)
    kp = xs[0].shape[1]
    assert m % tm == 0 and n % tn == 0 and all(x.shape == (m, kp) for x in xs)
    x_specs = [pl.BlockSpec((tm, kp), lambda i, j: (i, 0)) for _ in xs]
    w_specs = [pl.BlockSpec((1, kp, tn), functools.partial(lambda i, j, part: (layer, part, j), part=part))
               for part in range(len(xs))]
    return pl.pallas_call(
        functools.partial(_proj_res_kernel, n_parts=len(xs)),
        grid=(m // tm, n // tn),
        in_specs=x_specs + w_specs + [pl.BlockSpec((tm, tn), lambda i, j: (i, j)),
                                      pl.BlockSpec((1, 1, tn), lambda i, j: (group_of_tile(i, tm), 0, j))],
        out_specs=pl.BlockSpec((tm, tn), lambda i, j: (i, j)),
        out_shape=jax.ShapeDtypeStruct((m, n), F32),
        compiler_params=_cparams(("parallel", "arbitrary")),
        name=name,
    )(*xs, *([w] * len(xs)), res, gates)


def _modulate_kernel(x_ref, gain_ref, shift_ref, scale_ref, o_ref):
    x = x_ref[...]
    y = x * lax.rsqrt(jnp.mean(x * x, axis=-1, keepdims=True) + EPS)
    o_ref[...] = (y * gain_ref[...] * (1.0 + scale_ref[0]) + shift_ref[0]).astype(o_ref.dtype)


def modulate(x, gain, shift, scale, group_of_tile, *, tm, out_dtype=BF16):
    m, d = x.shape
    tm = min(tm, m)
    gspec = pl.BlockSpec((1, 1, d), lambda i: (group_of_tile(i, tm), 0, 0))
    return pl.pallas_call(
        _modulate_kernel,
        grid=(m // tm,),
        in_specs=[pl.BlockSpec((tm, d), lambda i: (i, 0)),
                  pl.BlockSpec((1, d), lambda i: (0, 0)), gspec, gspec],
        out_specs=pl.BlockSpec((tm, d), lambda i: (i, 0)),
        out_shape=jax.ShapeDtypeStruct((m, d), out_dtype),
        compiler_params=_cparams(("parallel",)),
        name="modulate",
    )(x, gain.reshape(1, d), shift, scale)


def _rmsnorm_kernel(x_ref, gain_ref, o_ref):
    x = x_ref[...]
    o_ref[...] = x * lax.rsqrt(jnp.mean(x * x, axis=-1, keepdims=True) + EPS) * gain_ref[...]


def rmsnorm_rows(x, gain, *, row0, nrows, tm):
    d = x.shape[1]
    tm = min(tm, nrows)
    assert row0 % tm == 0 and nrows % tm == 0
    rb0 = row0 // tm
    return pl.pallas_call(
        _rmsnorm_kernel,
        grid=(nrows // tm,),
        in_specs=[pl.BlockSpec((tm, d), lambda i: (rb0 + i, 0)), pl.BlockSpec((1, d), lambda i: (0, 0))],
        out_specs=pl.BlockSpec((tm, d), lambda i: (i, 0)),
        out_shape=jax.ShapeDtypeStruct((nrows, d), F32),
        compiler_params=_cparams(("parallel",)),
        name="final_norm",
    )(x, gain.reshape(1, d))


def _ada_kernel(c_ref, w_ref, b_ref, o_ref):
    c = c_ref[...]
    o_ref[0] = _bdot(c * jax.nn.sigmoid(c), w_ref[0]) + b_ref[0]


def ada_all(conds, ada_w, ada_b, *, tn=1024):
    depth, d, n = ada_w.shape
    r = conds.shape[0]
    return pl.pallas_call(
        _ada_kernel,
        grid=(depth, n // tn),
        in_specs=[pl.BlockSpec((r, d), lambda l, j: (0, 0)),
                  pl.BlockSpec((1, d, tn), lambda l, j: (l, 0, j)),
                  pl.BlockSpec((1, 1, tn), lambda l, j: (l, 0, j))],
        out_specs=pl.BlockSpec((1, r, tn), lambda l, j: (l, 0, j)),
        out_shape=jax.ShapeDtypeStruct((depth, r, n), F32),
        compiler_params=_cparams(("parallel", "arbitrary")),
        name="ada",
    )(conds, ada_w, ada_b.reshape(depth, 1, n))


def _dft_tables(n, scale):
    idx = np.arange(n, dtype=np.int64)
    ang = 2.0 * np.pi * ((idx[:, None] * idx[None, :]) % n).astype(np.float64) / n
    return np.stack([np.cos(ang) * scale, np.sin(ang) * scale]).astype(np.float32)


def _chan_dft_kernel(x_ref, t_ref, o_ref):
    o_ref[0] = _bdot(x_ref[...], t_ref[0]).astype(o_ref.dtype)


def _seq_dft_kernel(t_ref, r_ref, *rest):
    o_ref = rest[-1]
    o_ref[...] = (_bdot(t_ref[0], r_ref[0]) - _bdot(t_ref[1], r_ref[1])).astype(o_ref.dtype)


def chan_dft(p, *, tm):
    t = p.shape[0]
    gd = FOURIER_GROUP_DIM
    tm = min(tm, t)
    ctab = jnp.asarray(_dft_tables(gd, gd ** -0.5), BF16)
    return pl.pallas_call(
        _chan_dft_kernel,
        grid=(t // tm, FOURIER_GROUPS, 2),
        in_specs=[pl.BlockSpec((tm, gd), lambda i, g, s: (i, g)),
                  pl.BlockSpec((1, gd, gd), lambda i, g, s: (s, 0, 0))],
        out_specs=pl.BlockSpec((1, tm, gd), lambda i, g, s: (s, i, g)),
        out_shape=jax.ShapeDtypeStruct((2, t, FOURIER_WIDTH), BF16),
        compiler_params=_cparams(("parallel", "arbitrary", "arbitrary")),
        name="chan_dft",
    )(p, ctab)


def seq_dft(r, *, n_seq, seq_len, row0, prev_out=None, out_dtype=BF16):
    t = r.shape[1]
    assert row0 % seq_len == 0
    stab = jnp.asarray(_dft_tables(seq_len, seq_len ** -0.5), BF16)
    ts = min(512, seq_len)
    tn = 512
    nrow = seq_len // ts
    sb0 = row0 // seq_len
    rb0 = row0 // ts
    in_specs = [pl.BlockSpec((2, ts, seq_len), lambda b, i, j: (0, i, 0)),
                pl.BlockSpec((2, seq_len, tn), lambda b, i, j: (0, sb0 + b, j))]
    args = [stab, r]
    aliases = {}
    if prev_out is not None:
        in_specs.append(pl.BlockSpec(memory_space=pl.ANY))
        args.append(prev_out)
        aliases = {2: 0}
    return pl.pallas_call(
        _seq_dft_kernel,
        grid=(n_seq, nrow, FOURIER_WIDTH // tn),
        in_specs=in_specs,
        out_specs=pl.BlockSpec((ts, tn), lambda b, i, j: (rb0 + b * nrow + i, j)),
        out_shape=jax.ShapeDtypeStruct((t, FOURIER_WIDTH), out_dtype),
        input_output_aliases=aliases,
        compiler_params=_cparams(("parallel", "arbitrary", "arbitrary")),
        name="seq_dft",
    )(*args)


def _attn_kernel(q_ref, k_ref, v_ref, *rest):
    o_ref = rest[-1]
    k = k_ref[0].astype(BF16)
    v = v_ref[0].astype(BF16)
    scale = HEAD_DIM ** -0.5
    for g in range(GQA_GROUP):
        q = q_ref[0, :, g * HEAD_DIM:(g + 1) * HEAD_DIM]
        s = _bdot_nt(q, k)
        e = jnp.exp((s - jnp.max(s, axis=-1, keepdims=True)) * scale)
        o = _bdot(e, v) / jnp.sum(e, axis=-1, keepdims=True)
        o_ref[:, g * HEAD_DIM:(g + 1) * HEAD_DIM] = o.astype(o_ref.dtype)


def attention(q, k, v, *, total_rows, row0, prev_out=None, tq=512, out_dtype=BF16):
    b, sq, _ = q.shape
    sk = k.shape[1]
    tq = min(tq, sq)
    assert row0 % tq == 0
    gw = GQA_GROUP * HEAD_DIM
    nq = sq // tq
    rb0 = row0 // tq
    in_specs = [pl.BlockSpec((1, tq, gw), lambda bb, h, i: (bb, i, h)),
                pl.BlockSpec((1, sk, HEAD_DIM), lambda bb, h, i: (bb, 0, h)),
                pl.BlockSpec((1, sk, HEAD_DIM), lambda bb, h, i: (bb, 0, h))]
    args = [q, k, v]
    aliases = {}
    if prev_out is not None:
        in_specs.append(pl.BlockSpec(memory_space=pl.ANY))
        args.append(prev_out)
        aliases = {3: 0}
    return pl.pallas_call(
        _attn_kernel,
        grid=(b, KV_HEADS, nq),
        in_specs=in_specs,
        out_specs=pl.BlockSpec((tq, gw), lambda bb, h, i: (rb0 + bb * nq + i, h)),
        out_shape=jax.ShapeDtypeStruct((total_rows, ATTN_WIDTH), out_dtype),
        input_output_aliases=aliases,
        compiler_params=_cparams(("parallel", "parallel", "arbitrary")),
        name="attention",
    )(*args)


N_PROB = 4
DELTA_VMEM_BYTES_PER_ROW_HEAD = 14 * 1024


def _tri_masks(c):
    ri = lax.broadcasted_iota(jnp.int32, (c, c), 0)
    ci = lax.broadcasted_iota(jnp.int32, (c, c), 1)
    return ri, ci


def _unit_tri_inverse(low, ri, ci):
    c = low.shape[-1]

    def same_block(size):
        sh = int(math.log2(size))
        return (ri >> sh) == (ci >> sh)

    leaf = 4
    dg = jnp.where(same_block(leaf), low, 0.0)
    imd = jnp.where(ri == ci, 1.0, 0.0) - dg
    inv = imd + _bmm(imd, _bmm(dg, dg))
    size = leaf
    while size < c:
        off = jnp.where(same_block(2 * size) & jnp.logical_not(same_block(size)), low, 0.0)
        inv = inv - _bmm(inv, _bmm(off, inv))
        size *= 2
    return inv


def _delta_kernel(*refs, n_chunks, unroll_a, kh_step, has_prev):
    n_in = 9 + int(has_prev)
    q_ref, k_ref, v_ref, z_ref, gain_ref, cols_ref, rows_ref, sf0_ref, sb0_ref = refs[:9]
    on_ref, sf_ref, sb_ref, u_s, wq_s, akt_s, gl_s, o_s = refs[n_in:]
    c = DN_CHUNK
    dk = DN_K_DIM
    dv = DN_V_DIM
    ri, ci = _tri_masks(c)

    ca = unroll_a
    n_chain = kh_step * N_PROB

    def vhead(pp):
        return 2 * (pp // N_PROB) + pp % 2

    def body_a(g0, carry):
        rows = pl.ds(pl.multiple_of(g0 * (ca * c), ca * c), ca * c)
        chs = pl.ds(g0 * ca, ca)
        ks, qs, qks = [], [], []
        lows, rhss, decays, egs, glasts, gcols = [], [], [], [], [], []
        for hq in range(kh_step):
            k = k_ref[rows, hq * dk:(hq + 1) * dk].astype(F32).reshape(ca, c, dk)
            q = q_ref[rows, hq * dk:(hq + 1) * dk].astype(F32).reshape(ca, c, dk)
            kk = _bmm_nt(k, k)
            ks.append(k)
            qs.append(q)
            qks.append(_bmm_nt(q, k))
            cols = cols_ref[0, hq, rows, :].reshape(ca, c, 2 * N_PROB)
            rws = rows_ref[0, hq, chs]
            for p in range(N_PROB):
                head, reverse = p % 2, p >= 2
                beta = cols[:, :, p:p + 1]
                gcol = cols[:, :, N_PROB + p:N_PROB + p + 1]
                grow = rws[:, p:p + 1, :]
                glast = grow[:, :, 0:1] if reverse else grow[:, :, c - 1:c]
                incl = (ri <= ci) if reverse else (ri >= ci)
                strict = (ri < ci) if reverse else (ri > ci)
                decay = jnp.where(incl, jnp.exp(jnp.where(incl, gcol - grow, 0.0)), 0.0)
                eg = jnp.exp(gcol)
                vh = 2 * hq + head
                v = v_ref[rows, vh * dv:(vh + 1) * dv].astype(F32).reshape(ca, c, dv)
                lows.append(jnp.where(strict, kk * beta * decay, 0.0))
                rhss.append(jnp.concatenate([v * beta, k * (beta * eg)], axis=-1))
                decays.append(decay)
                egs.append(eg)
                glasts.append(glast)
                gcols.append(gcol)
        inv = _unit_tri_inverse(jnp.concatenate(lows, axis=0), ri, ci)
        sol = _bmm(inv, jnp.concatenate(rhss, axis=0))
        for pp in range(n_chain):
            k, q, qk = ks[pp // N_PROB], qs[pp // N_PROB], qks[pp // N_PROB]
            sp = sol[pp * ca:(pp + 1) * ca]
            u_s[pp, rows, :] = sp[:, :, :dv].reshape(ca * c, dv)
            wq_s[pp, chs, 0:c, :] = sp[:, :, dv:].astype(BF16)
            wq_s[pp, chs, c:2 * c, :] = (q * egs[pp]).astype(BF16)
            akt_s[pp, chs, 0:c, :] = (qk * decays[pp]).astype(BF16)
            akt_s[pp, chs, c:c + dk, :] = jnp.swapaxes(k * jnp.exp(glasts[pp] - gcols[pp]), 1, 2).astype(BF16)
            gl_s[pp, chs] = jnp.broadcast_to(jnp.exp(glasts[pp]), (ca, 8, dv))
        return carry

    lax.fori_loop(0, n_chunks // ca, body_a, 0)
    o_s[...] = jnp.zeros_like(o_s)

    def body_b(i, states):
        chs = [(n_chunks - 1 - i) if pp % N_PROB >= 2 else i for pp in range(n_chain)]
        rows = [pl.ds(pl.multiple_of(ch * c, c), c) for ch in chs]
        ws = [_bdot(wq_s[pp, chs[pp]], states[pp]) for pp in range(n_chain)]
        vn = [u_s[pp, rows[pp], :] - ws[pp][0:c] for pp in range(n_chain)]
        av = [_bdot(akt_s[pp, chs[pp]], vn[pp]) for pp in range(n_chain)]
        new_states = []
        for pp in range(n_chain):
            head = vhead(pp)
            o_s[rows[pp], head * dv:(head + 1) * dv] += ws[pp][c:2 * c] + av[pp][0:c]
            new_states.append(states[pp] * gl_s[pp, chs[pp]][0:1, :] + av[pp][c:c + dk])
        return tuple(new_states)

    init = tuple((sb0_ref if pp % N_PROB >= 2 else sf0_ref)[0, vhead(pp)] for pp in range(n_chain))
    fin = lax.fori_loop(0, n_chunks, body_b, init)
    for pp in range(n_chain):
        (sb_ref if pp % N_PROB >= 2 else sf_ref)[0, vhead(pp)] = fin[pp]
    for head in range(2 * kh_step):
        hs = slice(head * dv, (head + 1) * dv)
        o = o_s[:, hs]
        z = z_ref[:, hs].astype(F32)
        on = o * lax.rsqrt(jnp.mean(o * o, axis=-1, keepdims=True) + EPS) * gain_ref[...]
        on_ref[:, hs] = (on * (z * jax.nn.sigmoid(z))).astype(on_ref.dtype)


def delta_mixer(qkv, p, beta, g, gain, s_f0, s_b0, *, n_seq, seq_len, row0, prev_out=None):
    b, t = n_seq, seq_len
    tt = qkv.shape[0]
    h = DN_V_HEADS
    kh = DN_QK_HEADS
    rep = h // kh
    assert rep == 2 and row0 % t == 0
    rb0 = row0 // t
    c = DN_CHUNK
    nc = t // c
    gch = g.reshape(b, nc, c, 2, h)
    gcf = jnp.cumsum(gch[:, :, :, 0], axis=2).reshape(b, t, h)
    gcb = jnp.flip(jnp.cumsum(jnp.flip(gch[:, :, :, 1], axis=2), axis=2), axis=2).reshape(b, t, h)
    cols = jnp.stack([beta[:, :, 0], beta[:, :, 1], gcf, gcb], axis=2).reshape(b, t, 4, kh, rep)
    cols = jnp.transpose(cols, (0, 3, 1, 2, 4)).reshape(b, kh, t, 4 * rep)
    rows = jnp.stack([gcf, gcb], axis=2).reshape(b, nc, c, 2, kh, rep)
    rows = jnp.transpose(rows, (0, 4, 1, 3, 5, 2)).reshape(b, kh, nc, N_PROB, c)
    kh_step = max(n for n in (4, 2, 1) if n == 1 or t * n * DELTA_VMEM_BYTES_PER_ROW_HEAD <= VMEM_LIMIT // 2)
    stspec = pl.BlockSpec((1, rep * kh_step, DN_K_DIM, DN_V_DIM), lambda bb, hh: (bb, hh, 0, 0))
    unroll_a = math.gcd(nc, 8)
    kw = kh_step * DN_K_DIM
    vw = kh_step * rep * DN_V_DIM
    k_blk0 = DN_QK_WIDTH // kw
    v_blk0 = 2 * DN_QK_WIDTH // vw
    z_blk0 = DN_CONV_CH // vw
    in_specs = [pl.BlockSpec((t, kw), lambda bb, hh: (rb0 + bb, hh)),
                pl.BlockSpec((t, kw), lambda bb, hh: (rb0 + bb, k_blk0 + hh)),
                pl.BlockSpec((t, vw), lambda bb, hh: (rb0 + bb, v_blk0 + hh)),
                pl.BlockSpec((t, vw), lambda bb, hh: (rb0 + bb, z_blk0 + hh)),
                pl.BlockSpec((1, DN_V_DIM), lambda bb, hh: (0, 0)),
                pl.BlockSpec((1, kh_step, t, 4 * rep), lambda bb, hh: (bb, hh, 0, 0)),
                pl.BlockSpec((1, kh_step, nc, N_PROB, c), lambda bb, hh: (bb, hh, 0, 0, 0)),
                stspec, stspec]
    args = [qkv, qkv, qkv, p, gain.reshape(1, DN_V_DIM), cols, rows, s_f0, s_b0]
    aliases = {}
    if prev_out is not None:
        in_specs.append(pl.BlockSpec(memory_space=pl.ANY))
        args.append(prev_out)
        aliases = {len(args) - 1: 0}
    on, sf, sb = pl.pallas_call(
        functools.partial(_delta_kernel, n_chunks=nc, unroll_a=unroll_a, kh_step=kh_step,
                          has_prev=prev_out is not None),
        grid=(b, kh // kh_step),
        in_specs=in_specs,
        out_specs=[pl.BlockSpec((t, vw), lambda bb, hh: (rb0 + bb, hh)), stspec, stspec],
        out_shape=[jax.ShapeDtypeStruct((tt, DN_V_WIDTH), BF16),
                   jax.ShapeDtypeStruct((b, h, DN_K_DIM, DN_V_DIM), F32),
                   jax.ShapeDtypeStruct((b, h, DN_K_DIM, DN_V_DIM), F32)],
        scratch_shapes=[pltpu.VMEM((kh_step * N_PROB, t, DN_V_DIM), F32),
                        pltpu.VMEM((kh_step * N_PROB, nc, 2 * c, DN_K_DIM), BF16),
                        pltpu.VMEM((kh_step * N_PROB, nc, c + DN_K_DIM, c), BF16),
                        pltpu.VMEM((kh_step * N_PROB, nc, 8, DN_V_DIM), F32),
                        pltpu.VMEM((t, vw), F32)],
        input_output_aliases=aliases,
        compiler_params=_cparams(("parallel", "parallel")),
        name="delta_rule",
    )(*args)
    return on, sf, sb


def _dn_prep_kernel(*refs, has_prev):
    p_ref, w_ref = refs[:2]
    o_ref = refs[2 + int(has_prev)]
    j = pl.program_id(1)
    x = p_ref[...].astype(F32)
    s, cb = x.shape
    t = lax.broadcasted_iota(jnp.int32, (s, cb), 0)
    w = w_ref[...]
    y = (w[0:1] * jnp.where(t >= 2, pltpu.roll(x, 2, 0), 0.0) + w[1:2] * jnp.where(t >= 1, pltpu.roll(x, 1, 0), 0.0)
         + w[2:3] * x + w[3:4] * jnp.where(t < s - 1, pltpu.roll(x, s - 1, 0), 0.0))
    y = y * jax.nn.sigmoid(y)
    qk_blocks = DN_QK_WIDTH // cb
    for hh in range(cb // DN_K_DIM):
        seg = y[:, hh * DN_K_DIM:(hh + 1) * DN_K_DIM]
        inv = lax.rsqrt(jnp.sum(seg * seg, axis=-1, keepdims=True) + EPS)
        mult = jnp.where(j < qk_blocks, inv * (DN_K_DIM ** -0.5), jnp.where(j < 2 * qk_blocks, inv, 1.0))
        o_ref[:, hh * DN_K_DIM:(hh + 1) * DN_K_DIM] = (seg * mult).astype(o_ref.dtype)


def dn_prep(p, conv_w, *, n_seq, seq_len, row0, prev_out=None, cb=512):
    tt = p.shape[0]
    assert row0 % seq_len == 0 and DN_QK_WIDTH % cb == 0
    rb0 = row0 // seq_len
    in_specs = [pl.BlockSpec((seq_len, cb), lambda b, j: (rb0 + b, j)),
                pl.BlockSpec((conv_w.shape[0], cb), lambda b, j: (0, j))]
    args = [p, conv_w]
    aliases = {}
    if prev_out is not None:
        in_specs.append(pl.BlockSpec(memory_space=pl.ANY))
        args.append(prev_out)
        aliases = {2: 0}
    return pl.pallas_call(
        functools.partial(_dn_prep_kernel, has_prev=prev_out is not None),
        grid=(n_seq, DN_CONV_CH // cb),
        in_specs=in_specs,
        out_specs=pl.BlockSpec((seq_len, cb), lambda b, j: (rb0 + b, j)),
        out_shape=jax.ShapeDtypeStruct((tt, DN_CONV_CH), BF16),
        input_output_aliases=aliases,
        compiler_params=_cparams(("parallel", "arbitrary")),
        name="dn_prep",
    )(*args)


def _peer_route_kernel(q_ref, keys_ref, nc_ref, re_ref, v1_ref, v2_ref):
    kk = PEER_TOPK
    neg = -jnp.inf
    for h in range(PEER_HEADS):
        r0 = h * 2 * PEER_HALF
        s1 = _bdot_nt(keys_ref[0], q_ref[:, r0:r0 + PEER_HALF])
        s2 = _bdot_nt(keys_ref[1], q_ref[:, r0 + PEER_HALF:r0 + 2 * PEER_HALF])
        sc = s1
        for r in range(kk):
            m = jnp.max(sc, axis=0, keepdims=True)
            v1_ref[r:r + 1, :] = m
            sc = jnp.where(sc == m, neg, sc)
        sc = s2
        rank2 = jnp.full(s2.shape, float(kk), F32)
        for r in range(kk):
            m = jnp.max(sc, axis=0, keepdims=True)
            v2_ref[r:r + 1, :] = m
            hit = sc == m
            rank2 = jnp.where(hit, float(r), rank2)
            sc = jnp.where(hit, neg, sc)
        v2 = v2_ref[...]
        cand = jnp.concatenate([v1_ref[0:1, :] + v2] + [v1_ref[a:a + 1, :] + v2[0:8] for a in range(1, kk)],
                               axis=0)
        top = cand[0:1, :]
        work = cand
        tau = top
        for r in range(kk):
            tau = jnp.max(work, axis=0, keepdims=True)
            work = jnp.where(work == tau, neg, work)
        sel = cand >= tau
        z = jnp.sum(jnp.where(sel, jnp.exp(cand - top), 0.0), axis=0, keepdims=True)
        one = jnp.where(sel, 1.0, 0.0)
        n1 = jnp.zeros(s1.shape, F32)
        for a in range(kk):
            lo, hi = (0, kk) if a == 0 else (kk + 8 * (a - 1), kk + 8 * a)
            cnt = jnp.sum(one[lo:hi], axis=0, keepdims=True)
            n1 = jnp.where(s1 == v1_ref[a:a + 1, :], cnt, n1)
        nc_ref[0, h] = n1
        nc_ref[1, h] = (0.5 * jnp.exp(s1 - v1_ref[0:1, :])) / z
        re_ref[0, h] = rank2.astype(BF16)
        re_ref[1, h] = jnp.exp(s2 - v2[0:1, :]).astype(BF16)


def peer_route(q, sub_keys, *, tb):
    t, d2 = q.shape
    tb = min(tb, t)
    tspec = pl.BlockSpec((2, PEER_HEADS, PEER_N_KEYS, tb), lambda j: (0, 0, 0, j))
    return pl.pallas_call(
        _peer_route_kernel,
        grid=(t // tb,),
        in_specs=[pl.BlockSpec((tb, d2), lambda j: (j, 0)),
                  pl.BlockSpec((2, PEER_N_KEYS, PEER_HALF), lambda j: (0, 0, 0))],
        out_specs=[tspec, tspec],
        out_shape=[jax.ShapeDtypeStruct((2, PEER_HEADS, PEER_N_KEYS, t), F32),
                   jax.ShapeDtypeStruct((2, PEER_HEADS, PEER_N_KEYS, t), BF16)],
        scratch_shapes=[pltpu.VMEM((PEER_TOPK, tb), F32), pltpu.VMEM((PEER_TOPK, tb), F32)],
        compiler_params=_cparams(("parallel",)),
        name="peer_route",
    )(q, sub_keys)


def _cast_kernel(x_ref, o_ref):
    o_ref[...] = x_ref[0].astype(o_ref.dtype)


def cast_layer_bf16(w, layer, *, tr=1024):
    _, r, c = w.shape
    return pl.pallas_call(
        _cast_kernel,
        grid=(r // tr,),
        in_specs=[pl.BlockSpec((1, tr, c), lambda i: (layer, i, 0))],
        out_specs=pl.BlockSpec((tr, c), lambda i: (i, 0)),
        out_shape=jax.ShapeDtypeStruct((r, c), BF16),
        compiler_params=_cparams(("parallel",)),
        name="cast_bf16",
    )(w)


def _transpose_cast_kernel(x_ref, o_ref):
    o_ref[...] = jnp.transpose(x_ref[0]).astype(o_ref.dtype)


def transpose_layer_bf16(w, layer, *, tr=512):
    _, r, c = w.shape
    return pl.pallas_call(
        _transpose_cast_kernel,
        grid=(r // tr,),
        in_specs=[pl.BlockSpec((1, tr, c), lambda i: (layer, i, 0))],
        out_specs=pl.BlockSpec((c, tr), lambda i: (0, i)),
        out_shape=jax.ShapeDtypeStruct((c, r), BF16),
        compiler_params=_cparams(("parallel",)),
        name="transpose_bf16",
    )(w)


def _peer_dense_kernel(u_ref, wd_ref, wut_ref, nc_ref, re_ref, x_ref, g_ref, o_ref, acc_s, ut_s, *,
                       rows_per_block, n_blocks):
    i = pl.program_id(1)

    @pl.when(i == 0)
    def _():
        acc_s[...] = jnp.zeros_like(acc_s)
        ut_s[...] = jnp.transpose(u_ref[...].astype(F32)).astype(BF16)

    tb = u_ref.shape[0]
    cw = min(PEER_COL_CHUNK, tb)
    e1s = pl.ds(pl.multiple_of(i * rows_per_block, rows_per_block), rows_per_block)
    n1_all = [nc_ref[0, h, e1s, :] for h in range(PEER_HEADS)]
    c1_all = [nc_ref[1, h, e1s, :] for h in range(PEER_HEADS)]
    chunks = [slice(c0, c0 + cw) for c0 in range(0, tb, cw)]
    gate_rows = [[None] * rows_per_block for _ in chunks]
    for r in range(rows_per_block):
        for h in range(PEER_HEADS):
            n1 = jnp.broadcast_to(n1_all[h][r:r + 1, :], (PACK, tb)).astype(BF16)
            c1 = jnp.broadcast_to(c1_all[h][r:r + 1, :], (PACK, tb)).astype(BF16)
            for ic, cs in enumerate(chunks):
                rank2 = re_ref[0, h, :, cs].reshape(PEER_N_KEYS // PACK, PACK, cw)
                e2w = re_ref[1, h, :, cs].reshape(PEER_N_KEYS // PACK, PACK, cw)
                wgt = jnp.where(rank2 < n1[None, :, cs], e2w * c1[None, :, cs], jnp.zeros((), BF16))
                gate_rows[ic][r] = wgt if gate_rows[ic][r] is None else gate_rows[ic][r] + wgt
    hpres = [jnp.dot(wd_ref[...], ut_s[:, cs], preferred_element_type=F32) for cs in chunks]
    for cs, hpre, rows_c in zip(chunks, hpres, gate_rows):
        half_gate = jnp.concatenate([g.reshape(PEER_N_KEYS, cw) for g in rows_c], axis=0)
        act = (hpre * (1.0 + lax.erf(hpre * (2.0 ** -0.5)))).astype(BF16) * half_gate
        acc_s[:, cs] += jnp.dot(wut_ref[...], act, preferred_element_type=F32)

    @pl.when(i == n_blocks - 1)
    def _():
        o_ref[...] = x_ref[...] + g_ref[0] * jnp.transpose(acc_s[...])


def peer_dense(u, w_down, w_up_t, nc, re, x, gates, group_of_tile, *, tb, eb):
    t, d = u.shape
    ne = w_down.shape[0]
    tb = min(tb, t)
    assert eb % PEER_N_KEYS == 0 and t % tb == 0
    nblk = ne // eb
    once = pl.Buffered(1)
    tspec = pl.BlockSpec((2, PEER_HEADS, PEER_N_KEYS, tb), lambda j, i: (0, 0, 0, j), pipeline_mode=once)
    return pl.pallas_call(
        functools.partial(_peer_dense_kernel, rows_per_block=eb // PEER_N_KEYS, n_blocks=nblk),
        grid=(t // tb, nblk),
        in_specs=[pl.BlockSpec((tb, d), lambda j, i: (j, 0), pipeline_mode=once),
                  pl.BlockSpec((eb, d), lambda j, i: (i, 0)),
                  pl.BlockSpec((d, eb), lambda j, i: (0, i)),
                  tspec, tspec,
                  pl.BlockSpec((tb, d), lambda j, i: (j, 0), pipeline_mode=once),
                  pl.BlockSpec((1, 1, d), lambda j, i: (group_of_tile(j, tb), 0, 0))],
        out_specs=pl.BlockSpec((tb, d), lambda j, i: (j, 0)),
        out_shape=jax.ShapeDtypeStruct((t, d), F32),
        scratch_shapes=[pltpu.VMEM((d, tb), F32), pltpu.VMEM((d, tb), BF16)],
        compiler_params=_cparams(("parallel", "arbitrary")),
        name="peer_dense",
    )(u, w_down, w_up_t, nc, re, x, gates)


def peer_residual(x, u, layer, w_q, sub_keys, w_down, w_up, gates, group_of_tile, *, tb=512, eb=1024):
    q = proj(u, w_q, layer=layer, tm=1024, tn=512, name="peer_q")
    nc, re = peer_route(q, sub_keys[layer], tb=256)
    return peer_dense(u, cast_layer_bf16(w_down, layer), transpose_layer_bf16(w_up, layer), nc, re,
                      x, gates, group_of_tile, tb=tb, eb=eb)


def _rope_tables(n_prompt_rows, n_seq, seq_len):
    n_rows = seq_len // GRID_W
    rows = jnp.repeat(jnp.arange(n_rows), GRID_W).astype(F32)
    cols = jnp.tile(jnp.arange(GRID_W), n_rows).astype(F32)
    n_freq = ROPE_AXIS_DIM // 2
    inv = ROPE_THETA ** (-jnp.arange(n_freq, dtype=F32) / n_freq)
    ar, ac = rows[:, None] * inv, cols[:, None] * inv
    cos_t = jnp.concatenate([jnp.cos(ar), jnp.cos(ar), jnp.cos(ac), jnp.cos(ac)], axis=-1)
    sin_t = jnp.concatenate([-jnp.sin(ar), jnp.sin(ar), -jnp.sin(ac), jnp.sin(ac)], axis=-1)
    cos_t = jnp.concatenate([jnp.ones((n_prompt_rows, HEAD_DIM), F32)] + [cos_t] * n_seq, axis=0)
    sin_t = jnp.concatenate([jnp.zeros((n_prompt_rows, HEAD_DIM), F32)] + [sin_t] * n_seq, axis=0)
    return cos_t, sin_t


def _qk_prep_kernel(pq_ref, pk_ref, qg_ref, kg_ref, cos_ref, sin_ref, q_ref, k_ref):
    cos_t = cos_ref[...]
    sin_t = sin_ref[...]
    lane = lax.broadcasted_iota(jnp.int32, cos_t.shape, 1)
    first = (lane % (ROPE_AXIS_DIM)) < (ROPE_AXIS_DIM // 2)
    quarter = ROPE_AXIS_DIM // 2

    def norm_rope(x, gain):
        y = x * lax.rsqrt(jnp.mean(x * x, axis=-1, keepdims=True) + EPS) * gain
        swapped = jnp.where(first, pltpu.roll(y, HEAD_DIM - quarter, 1), pltpu.roll(y, quarter, 1))
        return y * cos_t + swapped * sin_t

    for h in range(N_HEADS):
        hs = slice(h * HEAD_DIM, (h + 1) * HEAD_DIM)
        q_ref[:, hs] = norm_rope(pq_ref[:, hs], qg_ref[...]).astype(q_ref.dtype)
    for h in range(KV_HEADS):
        hs = slice(h * HEAD_DIM, (h + 1) * HEAD_DIM)
        k_ref[:, hs] = norm_rope(pk_ref[:, hs], kg_ref[...])


def qk_prep(p, q_gain, k_gain, cos_t, sin_t, *, tm):
    t = p.shape[0]
    tm = min(tm, t)
    assert FOURIER_WIDTH % ATTN_WIDTH == 0 and (FOURIER_WIDTH + ATTN_WIDTH) % KV_WIDTH == 0
    q_blk, k_blk = FOURIER_WIDTH // ATTN_WIDTH, (FOURIER_WIDTH + ATTN_WIDTH) // KV_WIDTH
    rspec = pl.BlockSpec((tm, HEAD_DIM), lambda i: (i, 0))
    gspec = pl.BlockSpec((1, HEAD_DIM), lambda i: (0, 0))
    return pl.pallas_call(
        _qk_prep_kernel,
        grid=(t // tm,),
        in_specs=[pl.BlockSpec((tm, ATTN_WIDTH), lambda i: (i, q_blk)),
                  pl.BlockSpec((tm, KV_WIDTH), lambda i: (i, k_blk)), gspec, gspec, rspec, rspec],
        out_specs=[pl.BlockSpec((tm, ATTN_WIDTH), lambda i: (i, 0)), pl.BlockSpec((tm, KV_WIDTH), lambda i: (i, 0))],
        out_shape=[jax.ShapeDtypeStruct((t, ATTN_WIDTH), BF16), jax.ShapeDtypeStruct((t, KV_WIDTH), F32)],
        compiler_params=_cparams(("parallel",)),
        name="qk_prep",
    )(p, p, q_gain.reshape(1, HEAD_DIM), k_gain.reshape(1, HEAD_DIM), cos_t, sin_t)


def kernel(x_prompt, x_sample, cache_k, cache_v, state_fwd, state_bwd, c, c_ctx, ada_w, ada_b, norm1, norm2,
           af_w_in, af_q_norm, af_k_norm, af_w_out, dn_w_in, dn_conv_w, dn_a_log, dn_dt_bias, dn_o_norm,
           dn_w_out, peer_w_q, peer_sub_keys, peer_w_down, peer_w_up, final_norm):
    nb, seq, d = x_prompt.shape
    db, dseq, _ = x_sample.shape
    depth = ada_w.shape[0]
    tp = nb * seq
    ts = db * dseq
    tt = tp + ts
    tm = math.gcd(math.gcd(tp, dseq), 1024)

    def group_of_tile(i, tile):
        r = i * tile
        return jnp.where(r < tp, 0, 1 + (r - tp) // dseq)

    conds = jnp.concatenate([c_ctx[None], c, jnp.zeros((8 - 1 - db, d), F32)], axis=0)
    mods = ada_all(conds, ada_w, ada_b)
    mods = mods.reshape(depth, 8, 6, d)

    x = jnp.concatenate([x_prompt.reshape(tp, d), x_sample.reshape(ts, d)], axis=0)
    new_k, new_v, new_sf, new_sb = [], [], [], []
    for i in range(depth):
        j = i // 2
        md = [mods[i, :, n][:, None, :] for n in range(6)]
        sh1, sc1, g1, sh2, sc2, g2 = md
        u = modulate(x, norm1[i], sh1, sc1, group_of_tile, tm=tm)
        if i % 2 == 0:
            p = proj(u, af_w_in, layer=j, tm=tm, tn=512, name="af_in")
            cos_t, sin_t = _rope_tables(tp, db, dseq)
            q, k = qk_prep(p, af_q_norm[j], af_k_norm[j], cos_t, sin_t, tm=tm)
            v = p[:, FOURIER_WIDTH + ATTN_WIDTH + KV_WIDTH:]
            new_k.append(k[:tp].reshape(nb, seq, KV_HEADS, HEAD_DIM))
            new_v.append(v[:tp].reshape(nb, seq, KV_HEADS, HEAD_DIM))
            keys = jnp.concatenate([k[tp:].reshape(db, dseq, KV_WIDTH),
                                    cache_k[:, j].reshape(db, -1, KV_WIDTH)], axis=1)
            vals = jnp.concatenate([v[tp:].reshape(db, dseq, KV_WIDTH),
                                    cache_v[:, j].reshape(db, -1, KV_WIDTH)], axis=1)
            att = attention(q[:tp].reshape(nb, seq, ATTN_WIDTH), k[:tp].reshape(nb, seq, KV_WIDTH),
                            v[:tp].reshape(nb, seq, KV_WIDTH), total_rows=tt, row0=0)
            att = attention(q[tp:].reshape(db, dseq, ATTN_WIDTH), keys, vals, total_rows=tt, row0=tp, prev_out=att)
            r = chan_dft(p, tm=tm)
            fou = seq_dft(r, n_seq=nb, seq_len=seq, row0=0)
            fou = seq_dft(r, n_seq=db, seq_len=dseq, row0=tp, prev_out=fou)
            x = proj_residual([fou, att], af_w_out, x, g1, group_of_tile, layer=j, tm=tm, tn=512, name="af_out")
        else:
            p = proj(u, dn_w_in, layer=j, tm=2 * tm, tn=512, ncols=DN_CONV_CH + DN_V_WIDTH, out_dtype=BF16,
                     name="dn_in")
            ba = proj(u, dn_w_in, layer=j, tm=tm, tn=128, col0=DN_CONV_CH + DN_V_WIDTH, name="dn_in_ba")
            beta_all = jax.nn.sigmoid(ba.reshape(tt, 2, 2, DN_V_HEADS)[:, 0])
            g_all = -jnp.exp(dn_a_log[j]) * jax.nn.softplus(ba.reshape(tt, 2, 2, DN_V_HEADS)[:, 1] + dn_dt_bias[j])
            qkv = dn_prep(p, dn_conv_w[j], n_seq=nb, seq_len=seq, row0=0)
            qkv = dn_prep(p, dn_conv_w[j], n_seq=db, seq_len=dseq, row0=tp, prev_out=qkv)
            zeros = jnp.zeros((nb, DN_V_HEADS, DN_K_DIM, DN_V_DIM), F32)
            on, s_f, s_b = delta_mixer(qkv, p, beta_all[:tp].reshape(nb, seq, 2, DN_V_HEADS),
                                       g_all[:tp].reshape(nb, seq, 2, DN_V_HEADS), dn_o_norm[j], zeros, zeros,
                                       n_seq=nb, seq_len=seq, row0=0)
            new_sf.append(s_f)
            new_sb.append(s_b)
            on, _, _ = delta_mixer(qkv, p, beta_all[tp:].reshape(db, dseq, 2, DN_V_HEADS),
                                   g_all[tp:].reshape(db, dseq, 2, DN_V_HEADS), dn_o_norm[j],
                                   state_fwd[:, j], state_bwd[:, j], n_seq=db, seq_len=dseq, row0=tp, prev_out=on)
            x = proj_residual([on], dn_w_out, x, g1, group_of_tile, layer=j, tm=tm, tn=256, name="dn_out")
        u2 = modulate(x, norm2[i], sh2, sc2, group_of_tile, tm=tm)
        x = peer_residual(x, u2, i, peer_w_q, peer_sub_keys, peer_w_down, peer_w_up, g2, group_of_tile)
    return (rmsnorm_rows(x, final_norm, row0=0, nrows=tp, tm=tm).reshape(nb, seq, d),
            rmsnorm_rows(x, final_norm, row0=tp, nrows=ts, tm=tm).reshape(db, dseq, d),
            jnp.stack(new_k, axis=1), jnp.stack(new_v, axis=1),
            jnp.stack(new_sf, axis=1), jnp.stack(new_sb, axis=1))
```
